```python
import math
import jax, jax.numpy as jnp
from jax import lax
import numpy as np

D_MODEL = 1024
BATCH = 4
SEQ = 8192
DEPTH = 2
DEC_BATCH = 128
DEC_SEQ = 4
PAST_LEN = 16384
PAGE_SIZE = 128

HEAD_DIM = 64
MIX_WIDTH = D_MODEL
SSD_HEADS = 8
SSD_INNER = SSD_HEADS * HEAD_DIM
SSD_GROUPS = 2
SSD_STATE = 128
SSD_CONV = 4
SSD_CHUNK = 128
SSD_CONV_DIM = SSD_INNER + 2 * SSD_GROUPS * SSD_STATE
ATT_HEADS = 8
ATT_KV_HEADS = 2
ATT_GROUP = ATT_HEADS // ATT_KV_HEADS
ATT_WIDTH = ATT_HEADS * HEAD_DIM
KV_WIDTH = ATT_KV_HEADS * HEAD_DIM
WINDOW = 128
ATT_BLOCK = 128
D_FF = 2816
FFN_CONV = 3
SPLITS = [SSD_INNER,
          SSD_INNER + SSD_CONV_DIM,
          SSD_INNER + SSD_CONV_DIM + SSD_HEADS,
          SSD_INNER + SSD_CONV_DIM + SSD_HEADS + ATT_WIDTH,
          SSD_INNER + SSD_CONV_DIM + SSD_HEADS + ATT_WIDTH + KV_WIDTH]
IN_DIM = SSD_INNER + SSD_CONV_DIM + SSD_HEADS + ATT_WIDTH + 2 * KV_WIDTH
EPS = 1e-6

kernel_name = 'hybrid_ssd_swa_sink_convffn_step'


def rmsnorm(x, w):
    xf = x.astype(jnp.float32)
    y = xf * lax.rsqrt(jnp.mean(xf * xf, axis=-1, keepdims=True) + EPS)
    return (y * w.astype(jnp.float32)).astype(x.dtype)


def causal_dwconv(h, prev, w, b):
    k = w.shape[0]
    length = h.shape[1]
    hp = jnp.concatenate([prev, h], axis=1)
    out = hp[:, 0:length] * w[0]
    for i in range(1, k):
        out = out + hp[:, i:i + length] * w[i]
    return out + b, hp[:, length:]


def alibi_slopes():
    s = 2.0 ** (-8.0 * np.arange(1, ATT_HEADS + 1) / ATT_HEADS)
    return jnp.asarray(s.reshape(ATT_KV_HEADS, ATT_GROUP), dtype=jnp.float32)


def sink_probs(scores, rel, valid, slopes, sink):
    s = scores.astype(jnp.float32) - slopes[:, :, None, None] * rel.astype(jnp.float32)
    s = jnp.where(valid, s, -jnp.inf)
    sk = sink.astype(jnp.float32)[:, :, None, None]
    m = jnp.maximum(jnp.max(s, axis=-1, keepdims=True), sk)
    p = jnp.exp(s - m)
    return p / (jnp.sum(p, axis=-1, keepdims=True) + jnp.exp(sk - m))


def swa_prompt(q, k, v, slopes, sink):
    b, s = q.shape[:2]
    nq = ATT_BLOCK
    nb = s // nq
    qb = q.reshape(b, nb, nq, ATT_KV_HEADS, ATT_GROUP, HEAD_DIM)
    kb = k.reshape(b, nb, nq, ATT_KV_HEADS, HEAD_DIM)
    vb = v.reshape(b, nb, nq, ATT_KV_HEADS, HEAD_DIM)

    def with_prev(t):
        prev = jnp.concatenate([jnp.zeros_like(t[:, :1]), t[:, :-1]], axis=1)
        return jnp.concatenate([prev, t], axis=2)

    kk, vv = with_prev(kb), with_prev(vb)
    scores = jnp.einsum('bnqkgd,bnskd->bnkgqs', qb, kk) * (HEAD_DIM ** -0.5)
    qi = jnp.arange(nq)[:, None]
    sj = jnp.arange(2 * nq)[None, :]
    rel = qi + nq - sj
    key_pos = jnp.arange(nb)[:, None, None] * nq + sj[None] - nq
    valid = (rel >= 0) & (rel < WINDOW) & (key_pos >= 0)
    probs = sink_probs(scores, rel, valid[:, None, None], slopes, sink).astype(v.dtype)
    out = jnp.einsum('bnkgqs,bnskd->bnqkgd', probs, vv)
    return out.reshape(b, s, ATT_WIDTH)


def swa_sample(q, k, v, k_buf, v_buf, slopes, sink):
    b, t = q.shape[:2]
    w = k_buf.shape[1]
    kk = jnp.concatenate([k_buf, k], axis=1)
    vv = jnp.concatenate([v_buf, v], axis=1)
    qg = q.reshape(b, t, ATT_KV_HEADS, ATT_GROUP, HEAD_DIM)
    scores = jnp.einsum('btkgd,bskd->bkgts', qg, kk) * (HEAD_DIM ** -0.5)
    rel = jnp.arange(t)[:, None] + w - jnp.arange(w + t)[None, :]
    valid = (rel >= 0) & (rel < WINDOW)
    probs = sink_probs(scores, rel, valid, slopes, sink).astype(v.dtype)
    out = jnp.einsum('bkgts,bskd->btkgd', probs, vv).reshape(b, t, ATT_WIDTH)
    return out, kk[:, t:], vv[:, t:]


def ssd_scan(x, dt, a, bm, cm, h0, chunk):
    b, l, h, p = x.shape
    g, n = bm.shape[2:]
    r = h // g
    nc = l // chunk
    f32 = jnp.float32
    xc = x.reshape(b, nc, chunk, h, p)
    dtc = dt.reshape(b, nc, chunk, h)
    bc = bm.reshape(b, nc, chunk, g, n)
    cc = cm.reshape(b, nc, chunk, g, n)
    acum = jnp.cumsum(dtc * a, axis=2)
    acum_h = jnp.moveaxis(acum, 3, 2)
    seg = acum_h[..., :, None] - acum_h[..., None, :]
    causal = jnp.tril(jnp.ones((chunk, chunk), dtype=bool))
    decay = jnp.exp(jnp.where(causal, seg, -jnp.inf))
    cb = jnp.repeat(jnp.einsum('bclgn,bcsgn->bcgls', cc, bc), r, axis=2)
    xdt = xc * dtc[..., None].astype(x.dtype)
    y_diag = jnp.einsum('bchls,bcshp->bclhp', (cb * decay).astype(x.dtype), xdt)
    tail = jnp.exp(acum[:, :, -1:, :] - acum)
    bh = jnp.repeat(bc, r, axis=3)
    states = jnp.einsum('bclhn,bclhp->bchpn', bh * (tail * dtc)[..., None].astype(x.dtype), xc)
    chunk_decay = jnp.exp(acum[:, :, -1, :])

    def step(carry, inp):
        st, dc = inp
        return carry * dc[:, :, None, None] + st, carry

    final, prev = lax.scan(step, h0.astype(f32),
                           (jnp.moveaxis(states.astype(f32), 1, 0), jnp.moveaxis(chunk_decay, 1, 0)))
    prev = jnp.moveaxis(prev, 0, 1)
    ch = jnp.repeat(cc, r, axis=3)
    y_off = jnp.einsum('bclhn,bchpn->bclhp', ch.astype(f32), prev) * jnp.exp(acum)[..., None]
    y = (y_diag.astype(f32) + y_off).astype(x.dtype).reshape(b, l, h, p)
    return y, final.astype(h0.dtype)


def ssd_mixer(z, xbc, dt_raw, conv_prev, h0, lp):
    b, l = z.shape[:2]
    xbc, conv_new = causal_dwconv(xbc, conv_prev, lp['ssd_conv_w'], lp['ssd_conv_b'])
    xbc = jax.nn.silu(xbc)
    xs = xbc[..., :SSD_INNER].reshape(b, l, SSD_HEADS, HEAD_DIM)
    bm = xbc[..., SSD_INNER:SSD_INNER + SSD_GROUPS * SSD_STATE].reshape(b, l, SSD_GROUPS, SSD_STATE)
    cm = xbc[..., SSD_INNER + SSD_GROUPS * SSD_STATE:].reshape(b, l, SSD_GROUPS, SSD_STATE)
    dt = jax.nn.softplus((dt_raw + lp['dt_bias']).astype(jnp.float32))
    a = -jnp.exp(lp['a_log'].astype(jnp.float32))
    chunk = SSD_CHUNK if l % SSD_CHUNK == 0 else l
    y, h_new = ssd_scan(xs, dt, a, bm, cm, h0, chunk)
    y = y + xs * lp['d_skip'][:, None]
    y = y.reshape(b, l, SSD_INNER) * jax.nn.silu(z)
    gs = SSD_INNER // SSD_GROUPS
    y = rmsnorm(y.reshape(b, l, SSD_GROUPS, gs), lp['ssd_norm'].reshape(SSD_GROUPS, gs)).reshape(b, l, SSD_INNER)
    return y, conv_new, h_new


def hybrid_layer(x, lp, ssd_conv_prev, ssm_prev, ffn_conv_prev, kv_buf):
    b, l, _ = x.shape
    hn = rmsnorm(x, lp['norm1'])
    proj = hn @ lp['w_in']
    z, xbc, dt_raw, q, k, v = jnp.split(proj, SPLITS, axis=-1)
    y_ssd, ssd_conv_new, ssm_new = ssd_mixer(z, xbc, dt_raw, ssd_conv_prev, ssm_prev, lp)
    q = rmsnorm(q.reshape(b, l, ATT_HEADS, HEAD_DIM), lp['q_norm'])
    k = rmsnorm(k.reshape(b, l, ATT_KV_HEADS, HEAD_DIM), lp['k_norm'])
    v = v.reshape(b, l, ATT_KV_HEADS, HEAD_DIM)
    slopes = alibi_slopes()
    sink = lp['sinks'].reshape(ATT_KV_HEADS, ATT_GROUP)
    if kv_buf is None:
        y_att = swa_prompt(q, k, v, slopes, sink)
        k_new, v_new = k[:, -WINDOW:], v[:, -WINDOW:]
    else:
        y_att, k_new, v_new = swa_sample(q, k, v, kv_buf[0], kv_buf[1], slopes, sink)
    x = x + jnp.concatenate([y_ssd, y_att], axis=-1) @ lp['w_out']
    hn = rmsnorm(x, lp['norm2'])
    u, ffn_conv_new = causal_dwconv(hn @ lp['w_up'], ffn_conv_prev, lp['ffn_conv_w'], lp['ffn_conv_b'])
    gate, val = jnp.split(u, 2, axis=-1)
    x = x + (jax.nn.silu(gate) * val) @ lp['w_down']
    return x, (ssm_new, ssd_conv_new, k_new, v_new, ffn_conv_new)


def setup_inputs(seed: int = 0) -> dict:
    key = jax.random.key(seed)
    ks = iter(jax.random.split(key, 40))
    f32 = jnp.float32

    def nrm(shape, scale):
        return jax.random.normal(next(ks), shape, f32) * scale

    w_buf = min(WINDOW, PAST_LEN)
    dt0 = jnp.exp(jax.random.uniform(next(ks), (DEPTH, SSD_HEADS), f32, math.log(1e-3), math.log(1e-1)))
    return {
        'x_prompt': nrm((BATCH, SEQ, D_MODEL), 1.0),
        'x_sample': nrm((DEC_BATCH, DEC_SEQ, D_MODEL), 1.0),
        'state_ssm': nrm((DEPTH, DEC_BATCH, SSD_HEADS, HEAD_DIM, SSD_STATE), 0.1),
        'state_ssd_conv': nrm((DEPTH, DEC_BATCH, SSD_CONV - 1, SSD_CONV_DIM), 1.0),
        'cache_swa_k': nrm((DEPTH, DEC_BATCH, w_buf, ATT_KV_HEADS, HEAD_DIM), 1.0),
        'cache_swa_v': nrm((DEPTH, DEC_BATCH, w_buf, ATT_KV_HEADS, HEAD_DIM), 1.0),
        'state_ffn_conv': nrm((DEPTH, DEC_BATCH, FFN_CONV - 1, 2 * D_FF), 1.0),
        'norm1_w': 1.0 + nrm((DEPTH, D_MODEL), 0.01),
        'w_in': nrm((DEPTH, D_MODEL, IN_DIM), D_MODEL ** -0.5),
        'ssd_conv_w': nrm((DEPTH, SSD_CONV, SSD_CONV_DIM), SSD_CONV ** -0.5),
        'ssd_conv_b': nrm((DEPTH, SSD_CONV_DIM), 0.02),
        'dt_bias': dt0 + jnp.log(-jnp.expm1(-dt0)),
        'a_log': jnp.log(jax.random.uniform(next(ks), (DEPTH, SSD_HEADS), f32, 1.0, 16.0)),
        'd_skip': 1.0 + nrm((DEPTH, SSD_HEADS), 0.1),
        'ssd_norm_w': 1.0 + nrm((DEPTH, SSD_INNER), 0.01),
        'q_norm_w': 1.0 + nrm((DEPTH, HEAD_DIM), 0.01),
        'k_norm_w': 1.0 + nrm((DEPTH, HEAD_DIM), 0.01),
        'attn_sinks': nrm((DEPTH, ATT_HEADS), 0.5),
        'w_out': nrm((DEPTH, MIX_WIDTH, D_MODEL), MIX_WIDTH ** -0.5),
        'norm2_w': 1.0 + nrm((DEPTH, D_MODEL), 0.01),
        'w_up': nrm((DEPTH, D_MODEL, 2 * D_FF), D_MODEL ** -0.5),
        'ffn_conv_w': nrm((DEPTH, FFN_CONV, 2 * D_FF), FFN_CONV ** -0.5),
        'ffn_conv_b': nrm((DEPTH, 2 * D_FF), 0.02),
        'w_down': nrm((DEPTH, D_FF, D_MODEL), D_FF ** -0.5),
    }


def reference(x_prompt, x_sample, state_ssm, state_ssd_conv, cache_swa_k, cache_swa_v, state_ffn_conv,
              norm1_w, w_in, ssd_conv_w, ssd_conv_b, dt_bias, a_log, d_skip, ssd_norm_w,
              q_norm_w, k_norm_w, attn_sinks, w_out, norm2_w, w_up, ffn_conv_w, ffn_conv_b, w_down):
    def layer_params(i):
        return dict(norm1=norm1_w[i], w_in=w_in[i], ssd_conv_w=ssd_conv_w[i], ssd_conv_b=ssd_conv_b[i],
                    dt_bias=dt_bias[i], a_log=a_log[i], d_skip=d_skip[i], ssd_norm=ssd_norm_w[i],
                    q_norm=q_norm_w[i], k_norm=k_norm_w[i], sinks=attn_sinks[i], w_out=w_out[i],
                    norm2=norm2_w[i], w_up=w_up[i], ffn_conv_w=ffn_conv_w[i], ffn_conv_b=ffn_conv_b[i],
                    w_down=w_down[i])

    b = x_prompt.shape[0]
    dtype = x_prompt.dtype
    xp, xs = x_prompt, x_sample
    p_states, s_states = [], []
    for i in range(DEPTH):
        lp = layer_params(i)
        xp, st_p = hybrid_layer(xp, lp,
                                jnp.zeros((b, SSD_CONV - 1, SSD_CONV_DIM), dtype),
                                jnp.zeros((b, SSD_HEADS, HEAD_DIM, SSD_STATE), dtype),
                                jnp.zeros((b, FFN_CONV - 1, 2 * D_FF), dtype),
                                None)
        xs, st_s = hybrid_layer(xs, lp, state_ssd_conv[i], state_ssm[i], state_ffn_conv[i],
                                (cache_swa_k[i], cache_swa_v[i]))
        p_states.append(st_p)
        s_states.append(st_s)

    def stacked(states, j):
        return jnp.stack([st[j] for st in states])

    return (xp, xs,
            stacked(p_states, 0), stacked(p_states, 1), stacked(p_states, 2), stacked(p_states, 3), stacked(p_states, 4),
            stacked(s_states, 0), stacked(s_states, 1), stacked(s_states, 2), stacked(s_states, 3), stacked(s_states, 4))
```

```python
import functools

import numpy as np
import jax
import jax.numpy as jnp
from jax import lax
from jax.experimental import pallas as pl
from jax.experimental.pallas import tpu as pltpu

F32 = jnp.float32
BF16 = jnp.bfloat16

D_MODEL = 1024
HEAD_DIM = 64
SSD_HEADS = 8
SSD_INNER = SSD_HEADS * HEAD_DIM
SSD_GROUPS = 2
SSD_STATE = 128
SSD_CONV = 4
SSD_CHUNK = 128
SSD_CONV_DIM = SSD_INNER + 2 * SSD_GROUPS * SSD_STATE
ATT_HEADS = 8
ATT_KV_HEADS = 2
ATT_GROUP = ATT_HEADS // ATT_KV_HEADS
ATT_WIDTH = ATT_HEADS * HEAD_DIM
KV_WIDTH = ATT_KV_HEADS * HEAD_DIM
WINDOW = 128
D_FF = 2816
FFN_CONV = 3
EPS = 1e-6
NEG = -1e30

LANES = 128
SUBLANES = 8
VMEM_LIMIT = 56 * 1024 * 1024

_Z0, _X0, _Q0, _K0, _V0, _PEND = 0, 512, 1536, 2048, 2176, 2304
ALIBI = tuple(float(2.0 ** (-8.0 * (h + 1) / ATT_HEADS)) for h in range(ATT_HEADS))


def _rms(x, w):
    ms = jnp.mean(x * x, axis=-1, keepdims=True)
    return x * lax.rsqrt(ms + EPS) * w


def _silu(x):
    return x * jax.nn.sigmoid(x)


def _softplus(x):
    return jnp.maximum(x, 0.0) + jnp.log1p(jnp.exp(-jnp.abs(x)))


def _dot(a, b):
    return jnp.dot(a, b, preferred_element_type=F32)


def _dot_nt(a, b):
    return lax.dot_general(a, b, (((1,), (1,)), ((), ())), preferred_element_type=F32)


def _dot_tn(a, b):
    return lax.dot_general(a, b, (((0,), (0,)), ((), ())), preferred_element_type=F32)


def _const_spec(shape):
    nd = len(shape)
    return pl.BlockSpec(shape, lambda *_: (0,) * nd, pipeline_mode=pl.Buffered(1))


def _full_spec(shape):
    nd = len(shape)
    return pl.BlockSpec(shape, lambda *_: (0,) * nd)


_FFN_COLS = 256


def _ffn_kernel(*refs, tm, pad, shift, carry, out_proj):
    refs = list(refs)
    x_ref = refs.pop(0)
    if out_proj:
        y_ref = refs.pop(0)
        wout_ref = refs.pop(0)
    if not carry:
        prev_ref = refs.pop(0)
    n2_ref, wup_ref, cw_ref, cb_ref, wdn_ref, o_ref, tail_ref, hp_ref, act_ref = refs

    if carry:
        @pl.when(pl.program_id(1) == 0)
        def _():
            hp_ref[0:pad, :] = jnp.zeros((pad, 2 * D_FF), F32)
        x = x_ref[0]
    else:
        hp_ref[0:pad, :] = prev_ref[...]
        x = x_ref[...]

    if out_proj:
        x = x + _dot(y_ref[...].astype(BF16), wout_ref[...])
    hn = _rms(x, n2_ref[...]).astype(BF16)
    hp_ref[pad:pad + tm, :] = _dot(hn, wup_ref[...])

    for j in range(0, D_FF, _FFN_COLS):
        def conv(c0):
            cols = slice(c0, c0 + _FFN_COLS)
            acc = hp_ref[pad:pad + tm, cols] * cw_ref[2:3, cols]
            acc = acc + hp_ref[pad - shift:pad - shift + tm, cols] * cw_ref[1:2, cols]
            acc = acc + hp_ref[pad - 2 * shift:pad - 2 * shift + tm, cols] * cw_ref[0:1, cols]
            return acc + cb_ref[:, cols]
        act_ref[:, j:j + _FFN_COLS] = (_silu(conv(j)) * conv(D_FF + j)).astype(BF16)

    out = x + _dot(act_ref[...], wdn_ref[...])
    tail = hp_ref[tm + pad - 2 * shift:tm + pad, :]
    if carry:
        o_ref[0] = out
        tail_ref[0] = tail
        hp_ref[0:pad, :] = hp_ref[tm:tm + pad, :]
    else:
        o_ref[...] = out
        tail_ref[...] = tail


def _ffn_prompt(x, n2, wup, cw, cb, wdn, *, tm):
    b, s, _ = x.shape
    pad = SUBLANES
    kern = functools.partial(_ffn_kernel, tm=tm, pad=pad, shift=1, carry=True, out_proj=False)
    return pl.pallas_call(
        kern,
        grid=(b, s // tm),
        in_specs=[
            pl.BlockSpec((1, tm, D_MODEL), lambda i, t: (i, t, 0)),
            _const_spec((1, D_MODEL)),
            _const_spec((D_MODEL, 2 * D_FF)),
            _const_spec((FFN_CONV, 2 * D_FF)),
            _const_spec((1, 2 * D_FF)),
            _const_spec((D_FF, D_MODEL)),
        ],
        out_specs=[
            pl.BlockSpec((1, tm, D_MODEL), lambda i, t: (i, t, 0)),
            pl.BlockSpec((1, FFN_CONV - 1, 2 * D_FF), lambda i, t: (i, 0, 0)),
        ],
        out_shape=[
            jax.ShapeDtypeStruct((b, s, D_MODEL), F32),
            jax.ShapeDtypeStruct((b, FFN_CONV - 1, 2 * D_FF), F32),
        ],
        scratch_shapes=[
            pltpu.VMEM((tm + pad, 2 * D_FF), F32),
            pltpu.VMEM((tm, D_FF), BF16),
        ],
        compiler_params=pltpu.CompilerParams(
            dimension_semantics=("arbitrary", "arbitrary"), vmem_limit_bytes=VMEM_LIMIT),
        name="ffn_prompt",
    )(x, n2, wup, cw, cb, wdn)


def _ffn_sample(x, y, wout, prev, n2, wup, cw, cb, wdn):
    rows = x.shape[0]
    bs = prev.shape[0] // (FFN_CONV - 1)
    pad = (FFN_CONV - 1) * bs
    kern = functools.partial(_ffn_kernel, tm=rows, pad=pad, shift=bs, carry=False, out_proj=True)
    return pl.pallas_call(
        kern,
        grid=(1,),
        in_specs=[
            _const_spec((rows, D_MODEL)),
            _const_spec((rows, D_MODEL)),
            _const_spec((D_MODEL, D_MODEL)),
            _const_spec((pad, 2 * D_FF)),
            _const_spec((1, D_MODEL)),
            _const_spec((D_MODEL, 2 * D_FF)),
            _const_spec((FFN_CONV, 2 * D_FF)),
            _const_spec((1, 2 * D_FF)),
            _const_spec((D_FF, D_MODEL)),
        ],
        out_specs=[
            _full_spec((rows, D_MODEL)),
            _full_spec((pad, 2 * D_FF)),
        ],
        out_shape=[
            jax.ShapeDtypeStruct((rows, D_MODEL), F32),
            jax.ShapeDtypeStruct((pad, 2 * D_FF), F32),
        ],
        scratch_shapes=[
            pltpu.VMEM((rows + pad, 2 * D_FF), F32),
            pltpu.VMEM((rows, D_FF), BF16),
        ],
        compiler_params=pltpu.CompilerParams(
            dimension_semantics=("arbitrary",), vmem_limit_bytes=VMEM_LIMIT),
        name="ffn_sample",
    )(x, y, wout, prev, n2, wup, cw, cb, wdn)


def _tri_cumsum(x, ltri):
    h1 = x.astype(BF16)
    r1 = x - h1.astype(F32)
    h2 = r1.astype(BF16)
    h3 = (r1 - h2.astype(F32)).astype(BF16)
    return _dot(ltri, h1) + _dot(ltri, h2) + _dot(ltri, h3)


def _half_variants(a, lane_lo):
    ar = pltpu.roll(a, HEAD_DIM, axis=1)
    zero = jnp.zeros_like(a)
    return {
        (0, 0): jnp.where(lane_lo, a, zero).astype(BF16),
        (1, 1): jnp.where(lane_lo, zero, a).astype(BF16),
        (1, 0): jnp.where(lane_lo, ar, zero).astype(BF16),
        (0, 1): jnp.where(lane_lo, zero, ar).astype(BF16),
    }


def _ssd_chunk(xc, dt, z, st_ref, a, dskip, snorm, ltri, causal, lane_lo):
    c = SSD_CHUNK
    xs = xc[:, 0:SSD_INNER]
    acum = _tri_cumsum(dt * a, ltri)
    alast = acum[c - 1:c, :]
    eac = jnp.exp(acum)
    xdt = xs * dt
    xw = (xs * (jnp.exp(alast - acum) * dt)).astype(BF16)
    acum_t = acum.T
    cdec = jnp.exp(alast)
    gw = SSD_INNER // SSD_GROUPS
    ys = []
    for g in range(SSD_GROUPS):
        b0 = SSD_INNER + g * SSD_STATE
        c0 = SSD_INNER + SSD_GROUPS * SSD_STATE + g * SSD_STATE
        bm = xc[:, b0:b0 + SSD_STATE].astype(BF16)
        cm = xc[:, c0:c0 + SSD_STATE].astype(BF16)
        cb = _dot_nt(cm, bm)
        st_g = st_ref[:, g * gw:(g + 1) * gw]
        yoff = _dot(cm, st_g.astype(BF16))
        st_ref[:, g * gw:(g + 1) * gw] = st_g * cdec[:, g * gw:(g + 1) * gw] + _dot_tn(bm, xw[:, g * gw:(g + 1) * gw])
        for m in range(gw // LANES):
            pm = g * (gw // LANES) + m
            lanes = slice(pm * LANES, (pm + 1) * LANES)
            a_pair = acum[:, lanes]
            a_roll = pltpu.roll(a_pair, HEAD_DIM, axis=1)
            ms = []
            for e in range(2):
                h = 2 * pm + e
                colb = jnp.where(lane_lo, a_pair, a_roll) if e == 0 else jnp.where(lane_lo, a_roll, a_pair)
                rowb = jnp.broadcast_to(acum_t[h * HEAD_DIM:h * HEAD_DIM + 1, :], (c, c))
                dec = jnp.exp(jnp.where(causal, colb - rowb, NEG))
                ms.append((cb * dec).astype(BF16))
            xd = xdt[:, lanes]
            rhs = jnp.concatenate([jnp.where(lane_lo, xd, 0.0), jnp.where(lane_lo, 0.0, xd)], axis=0).astype(BF16)
            ydiag = _dot(jnp.concatenate(ms, axis=1), rhs)
            ys.append(ydiag + yoff[:, m * LANES:(m + 1) * LANES] * eac[:, lanes])
    y = (jnp.concatenate(ys, axis=1) + xs * dskip) * _silu(z)
    return jnp.concatenate(
        [_rms(y[:, g * gw:(g + 1) * gw], snorm[:, g * gw:(g + 1) * gw]) for g in range(SSD_GROUPS)], axis=1)


def _norm_k(k, knw, lane_lo):
    k2 = k * k
    r0 = lax.rsqrt(jnp.sum(jnp.where(lane_lo, k2, 0.0), axis=-1, keepdims=True) / HEAD_DIM + EPS)
    r1 = lax.rsqrt(jnp.sum(jnp.where(lane_lo, 0.0, k2), axis=-1, keepdims=True) / HEAD_DIM + EPS)
    return k * jnp.where(lane_lo, r0, r1) * knw


def _attn_block(q, qnw, kvars, vvars, rel, valid, sink_ref, lane_lo):
    qw = q * qnw
    q2 = q * q
    outs = []
    for m in range(ATT_HEADS // 2):
        lanes = slice(m * LANES, (m + 1) * LANES)
        kv = (2 * m) // ATT_GROUP
        qp = qw[:, lanes].astype(BF16)
        q2p = q2[:, lanes]
        ps, invs = [], []
        for e in range(2):
            h = 2 * m + e
            ssq = jnp.sum(jnp.where(lane_lo, q2p, 0.0) if e == 0 else jnp.where(lane_lo, 0.0, q2p),
                          axis=-1, keepdims=True)
            rs = lax.rsqrt(ssq / HEAD_DIM + EPS) * (HEAD_DIM ** -0.5)
            s = _dot_nt(qp, kvars[(kv, e)]) * rs - ALIBI[h] * rel
            s = jnp.where(valid, s, NEG)
            sk = sink_ref[h]
            mx = jnp.maximum(jnp.max(s, axis=-1, keepdims=True), sk)
            p = jnp.exp(s - mx)
            den = jnp.sum(p, axis=-1, keepdims=True) + jnp.exp(sk - mx)
            ps.append(p.astype(BF16))
            invs.append(1.0 / den)
        rhs = jnp.concatenate([vvars[(kv, 0)], vvars[(kv, 1)]], axis=0)
        o = _dot(jnp.concatenate(ps, axis=1), rhs)
        outs.append(o * jnp.where(lane_lo, invs[0], invs[1]))
    return jnp.concatenate(outs, axis=1)


def _mixer_prompt_kernel(sink_ref, x_ref, n1_ref, win_ref, wdt_ref, cw_ref, cb_ref, dtb_ref, alog_ref, dskip_ref,
                         snorm_ref, qnw_ref, knw_ref, wout_ref,
                         o_ref, ssm_ref, conv_ref, kc_ref, vc_ref,
                         proj_ref, dt_ref, cbuf_ref, st_ref, kprev_ref, vprev_ref, y_ref, *, tm):
    t = pl.program_id(1)
    nt = pl.num_programs(1)
    c = SSD_CHUNK
    pad = SUBLANES

    @pl.when(t == 0)
    def _():
        st_ref[...] = jnp.zeros(st_ref.shape, F32)
        cbuf_ref[0:pad, :] = jnp.zeros((pad, SSD_CONV_DIM), F32)
        kprev_ref[...] = jnp.zeros(kprev_ref.shape, F32)
        vprev_ref[...] = jnp.zeros(vprev_ref.shape, F32)

    x = x_ref[0]
    hn = _rms(x, n1_ref[...]).astype(BF16)
    proj_ref[:, 0:_X0] = _dot(hn, win_ref[:, 0:_X0])
    cbuf_ref[pad:pad + tm, :] = _dot(hn, win_ref[:, _X0:_Q0])
    proj_ref[:, _X0:] = _dot(hn, win_ref[:, _Q0:])
    dt_ref[...] = _softplus(_dot(hn, wdt_ref[...]) + dtb_ref[...])

    row = lax.broadcasted_iota(jnp.int32, (c, c), 0)
    col = lax.broadcasted_iota(jnp.int32, (c, c), 1)
    causal = row >= col
    ltri = jnp.where(causal, 1.0, 0.0).astype(BF16)
    lane_lo = col < HEAD_DIM
    a = -jnp.exp(alog_ref[...])

    qi = lax.broadcasted_iota(jnp.int32, (c, 2 * c), 0)
    sj = lax.broadcasted_iota(jnp.int32, (c, 2 * c), 1)
    reli = qi + c - sj
    rel = reli.astype(F32)
    in_window = (reli >= 0) & (reli < WINDOW)
    first_col = jnp.where(t == 0, c, 0)

    kprev = _half_variants(kprev_ref[...], lane_lo)
    vprev = _half_variants(vprev_ref[...], lane_lo)
    kn = v = None
    for ci in range(tm // c):
        r0 = ci * c
        rows = slice(r0, r0 + c)
        acc = cb_ref[...] + cbuf_ref[pad + r0:pad + r0 + c, :] * cw_ref[SSD_CONV - 1:SSD_CONV, :]
        for i in range(1, SSD_CONV):
            acc = acc + cbuf_ref[pad + r0 - i:pad + r0 - i + c, :] * cw_ref[SSD_CONV - 1 - i:SSD_CONV - i, :]
        xc = _silu(acc)
        y_ssd = _ssd_chunk(xc, dt_ref[rows, :], proj_ref[rows, _Z0:_Z0 + SSD_INNER], st_ref, a, dskip_ref[...],
                           snorm_ref[...], ltri, causal, lane_lo)
        y_ref[rows, 0:SSD_INNER] = y_ssd.astype(BF16)
        qo, ko, vo = _Q0 - SSD_CONV_DIM, _K0 - SSD_CONV_DIM, _V0 - SSD_CONV_DIM
        q = proj_ref[rows, qo:qo + ATT_WIDTH]
        kn = _norm_k(proj_ref[rows, ko:ko + KV_WIDTH], knw_ref[...], lane_lo)
        v = proj_ref[rows, vo:vo + KV_WIDTH]
        kcur = _half_variants(kn, lane_lo)
        vcur = _half_variants(v, lane_lo)
        kall = {key: jnp.concatenate([kprev[key], kcur[key]], axis=0) for key in kcur}
        vall = {key: jnp.concatenate([vprev[key], vcur[key]], axis=0) for key in vcur}
        valid = in_window & (sj >= first_col) if ci == 0 else in_window
        y_att = _attn_block(q, qnw_ref[...], kall, vall, rel, valid, sink_ref, lane_lo)
        y_ref[rows, SSD_INNER:] = y_att.astype(BF16)
        kprev, vprev = kcur, vcur

    o_ref[0] = x + _dot(y_ref[...], wout_ref[...])
    cbuf_ref[0:pad, :] = cbuf_ref[tm:tm + pad, :]
    kprev_ref[...] = kn
    vprev_ref[...] = v

    @pl.when(t == nt - 1)
    def _():
        ssm_ref[0] = st_ref[...].T
        conv_ref[0] = cbuf_ref[tm + pad - (SSD_CONV - 1):tm + pad, :]
        kc_ref[0] = kn
        vc_ref[0] = v


def _mixer_prompt(x, sinks, n1, win, wdt, cw, cb, dtb, alog, dskip, snorm, qnw, knw, wout, *, tm):
    b, s, _ = x.shape
    pw = _PEND - SSD_CONV_DIM
    kern = functools.partial(_mixer_prompt_kernel, tm=tm)
    consts = [n1, win, wdt, cw, cb, dtb, alog, dskip, snorm, qnw, knw, wout]
    return pl.pallas_call(
        kern,
        grid_spec=pltpu.PrefetchScalarGridSpec(
            num_scalar_prefetch=1,
            grid=(b, s // tm),
            in_specs=[pl.BlockSpec((1, tm, D_MODEL), lambda i, t, *_: (i, t, 0))]
            + [_const_spec(w.shape) for w in consts],
            out_specs=[
                pl.BlockSpec((1, tm, D_MODEL), lambda i, t, *_: (i, t, 0)),
                pl.BlockSpec((1, SSD_INNER, SSD_STATE), lambda i, t, *_: (i, 0, 0)),
                pl.BlockSpec((1, SSD_CONV - 1, SSD_CONV_DIM), lambda i, t, *_: (i, 0, 0)),
                pl.BlockSpec((1, WINDOW, KV_WIDTH), lambda i, t, *_: (i, 0, 0)),
                pl.BlockSpec((1, WINDOW, KV_WIDTH), lambda i, t, *_: (i, 0, 0)),
            ],
            scratch_shapes=[
                pltpu.VMEM((tm, pw), F32),
                pltpu.VMEM((tm, SSD_INNER), F32),
                pltpu.VMEM((tm + SUBLANES, SSD_CONV_DIM), F32),
                pltpu.VMEM((SSD_STATE, SSD_INNER), F32),
                pltpu.VMEM((WINDOW, KV_WIDTH), F32),
                pltpu.VMEM((WINDOW, KV_WIDTH), F32),
                pltpu.VMEM((tm, D_MODEL), BF16),
            ],
        ),
        out_shape=[
            jax.ShapeDtypeStruct((b, s, D_MODEL), F32),
            jax.ShapeDtypeStruct((b, SSD_INNER, SSD_STATE), F32),
            jax.ShapeDtypeStruct((b, SSD_CONV - 1, SSD_CONV_DIM), F32),
            jax.ShapeDtypeStruct((b, WINDOW, KV_WIDTH), F32),
            jax.ShapeDtypeStruct((b, WINDOW, KV_WIDTH), F32),
        ],
        compiler_params=pltpu.CompilerParams(
            dimension_semantics=("arbitrary", "arbitrary"), vmem_limit_bytes=VMEM_LIMIT),
        name="mixer_prompt",
    )(sinks, x, *consts)


_S_Z0, _S_X0, _S_DT0, _S_Q0, _S_K0, _S_V0, _S_END = 0, 512, 1536, 2048, 2560, 2688, 2816


def _sample_in_kernel(x_ref, prev_ref, n1_ref, win_ref, wdt_ref, cw_ref, cb_ref, dtb_ref, o_ref, tail_ref, cbuf_ref,
                      *, rows, bs):
    pad = (SSD_CONV - 1) * bs
    hn = _rms(x_ref[...], n1_ref[...]).astype(BF16)
    o_ref[:, _S_Z0:_S_X0] = _dot(hn, win_ref[:, _Z0:_X0])
    cbuf_ref[0:pad, :] = prev_ref[...]
    cbuf_ref[pad:pad + rows, :] = _dot(hn, win_ref[:, _X0:_Q0])
    acc = cb_ref[...] + cbuf_ref[pad:pad + rows, :] * cw_ref[SSD_CONV - 1:SSD_CONV, :]
    for i in range(1, SSD_CONV):
        acc = acc + cbuf_ref[pad - i * bs:pad - i * bs + rows, :] * cw_ref[SSD_CONV - 1 - i:SSD_CONV - i, :]
    o_ref[:, _S_X0:_S_DT0] = _silu(acc)
    o_ref[:, _S_DT0:_S_Q0] = _softplus(_dot(hn, wdt_ref[...]) + dtb_ref[...])
    o_ref[:, _S_Q0:_S_END] = _dot(hn, win_ref[:, _Q0:_PEND])
    tail_ref[...] = cbuf_ref[rows:rows + pad, :]


def _sample_in(x, prev, n1, win, wdt, cw, cb, dtb):
    rows = x.shape[0]
    pad = prev.shape[0]
    bs = pad // (SSD_CONV - 1)
    args = (x, prev, n1, win, wdt, cw, cb, dtb)
    return pl.pallas_call(
        functools.partial(_sample_in_kernel, rows=rows, bs=bs),
        grid=(1,),
        in_specs=[_const_spec(a.shape) for a in args],
        out_specs=[_full_spec((rows, _S_END)), _full_spec((pad, SSD_CONV_DIM))],
        out_shape=[jax.ShapeDtypeStruct((rows, _S_END), F32), jax.ShapeDtypeStruct((pad, SSD_CONV_DIM), F32)],
        scratch_shapes=[pltpu.VMEM((rows + pad, SSD_CONV_DIM), F32)],
        compiler_params=pltpu.CompilerParams(dimension_semantics=("arbitrary",), vmem_limit_bytes=VMEM_LIMIT),
        name="sample_in",
    )(*args)


def _sample_mix_kernel(sink_ref, p_ref, st_ref, kc_ref, vc_ref, alog_ref, dskip_ref, snorm_ref, qnw_ref, knw_ref,
                       y_ref, sto_ref, kn_ref, vn_ref, *, tlen):
    nseq = SUBLANES // tlen
    w = WINDOW
    pk = p_ref[...]
    z = pk[:, _S_Z0:_S_X0]
    xs = pk[:, _S_X0:_S_X0 + SSD_INNER]
    dt = pk[:, _S_DT0:_S_Q0]
    q = pk[:, _S_Q0:_S_K0]
    k = pk[:, _S_K0:_S_V0]
    v = pk[:, _S_V0:_S_END]
    gw = SSD_INNER // SSD_GROUPS
    bms = [pk[:, _S_X0 + SSD_INNER + g * SSD_STATE:_S_X0 + SSD_INNER + (g + 1) * SSD_STATE]
           for g in range(SSD_GROUPS)]
    c0 = _S_X0 + SSD_INNER + SSD_GROUPS * SSD_STATE
    cms = [pk[:, c0 + g * SSD_STATE:c0 + (g + 1) * SSD_STATE] for g in range(SSD_GROUPS)]

    rowi = lax.broadcasted_iota(jnp.int32, (SUBLANES, SSD_INNER), 0)
    assert tlen & (tlen - 1) == 0 and SUBLANES % tlen == 0
    tshift = tlen.bit_length() - 1
    sshift = SUBLANES.bit_length() - 1
    tpos = rowi & (tlen - 1)
    seq = rowi >> tshift
    seq_g = lax.broadcasted_iota(jnp.int32, (SUBLANES, gw), 0) >> tshift
    lane_lo = lax.broadcasted_iota(jnp.int32, (SUBLANES, LANES), 1) < HEAD_DIM

    dta = dt * (-jnp.exp(alog_ref[...]))
    acum = dta
    for s in range(1, tlen):
        acum = acum + jnp.where(tpos >= s, pltpu.roll(dta, s, axis=0), 0.0)
    tot = acum[tlen - 1:tlen, :]
    for b in range(1, nseq):
        tot = jnp.where(seq == b, acum[(b + 1) * tlen - 1:(b + 1) * tlen, :], tot)
    eac = jnp.exp(acum)
    cdec = jnp.exp(tot)
    xdt = xs * dt
    xw = xs * (jnp.exp(tot - acum) * dt)

    ydiag = jnp.zeros((SUBLANES, SSD_INNER), F32)
    for s in range(tlen):
        xsh = pltpu.roll(xdt, s, axis=0) if s else xdt
        ash = pltpu.roll(acum, s, axis=0) if s else acum
        cbs = []
        for g in range(SSD_GROUPS):
            bsh = pltpu.roll(bms[g], s, axis=0) if s else bms[g]
            cbs.append(jnp.broadcast_to(jnp.sum(cms[g] * bsh, axis=-1, keepdims=True), (SUBLANES, gw)))
        term = jnp.concatenate(cbs, axis=1) * jnp.exp(acum - ash) * xsh
        ydiag = ydiag + jnp.where(tpos >= s, term, 0.0)

    yoff = []
    for g in range(SSD_GROUPS):
        cm = cms[g].astype(BF16)
        bm = bms[g].astype(BF16)
        yo = None
        for b in range(nseq):
            h0 = st_ref[b, g * gw:(g + 1) * gw, :]
            yb = _dot_nt(cm, h0.astype(BF16))
            yo = yb if yo is None else jnp.where(seq_g == b, yb, yo)
            xwb = jnp.where(seq_g == b, xw[:, g * gw:(g + 1) * gw], 0.0).astype(BF16)
            cseq = jnp.broadcast_to(cdec[b * tlen:b * tlen + 1, :], (SUBLANES, SSD_INNER))
            blocks = []
            for hh in range(gw // HEAD_DIM):
                pm = (g * gw + hh * HEAD_DIM) // LANES
                cpair = cseq[:, pm * LANES:(pm + 1) * LANES]
                croll = pltpu.roll(cpair, HEAD_DIM, axis=1)
                chead = jnp.where(lane_lo, cpair, croll) if hh % 2 == 0 else jnp.where(lane_lo, croll, cpair)
                blocks += [chead] * (HEAD_DIM // SUBLANES)
            sto_ref[b, g * gw:(g + 1) * gw, :] = h0 * jnp.concatenate(blocks, axis=0) + _dot_tn(xwb, bm)
        yoff.append(yo)
    y = (ydiag + jnp.concatenate(yoff, axis=1) * eac + xs * dskip_ref[...]) * _silu(z)
    y_ssd = jnp.concatenate(
        [_rms(y[:, g * gw:(g + 1) * gw], snorm_ref[:, g * gw:(g + 1) * gw]) for g in range(SSD_GROUPS)], axis=1)

    kn = _norm_k(k, knw_ref[...], lane_lo)
    kn_ref[...] = kn
    vn_ref[...] = v
    qw = q * qnw_ref[...]
    q2 = q * q
    pieces = []
    for h in range(ATT_HEADS):
        kv = h // ATT_GROUP
        lanes = slice((h // 2) * LANES, (h // 2 + 1) * LANES)
        own = lane_lo if h % 2 == 0 else jnp.logical_not(lane_lo)
        ssq = jnp.sum(jnp.where(own, q2[:, lanes], 0.0), axis=-1, keepdims=True)
        qh = jnp.where(own, qw[:, lanes], 0.0) * (lax.rsqrt(ssq / HEAD_DIM + EPS) * (HEAD_DIM ** -0.5))
        pieces.append(qh if h % 2 == kv else pltpu.roll(qh, HEAD_DIM, axis=1))
    qrows = jnp.concatenate(pieces, axis=0).astype(BF16)

    nq = ATT_HEADS * SUBLANES
    ncol = w + SUBLANES
    ri = lax.broadcasted_iota(jnp.int32, (nq, ncol), 0)
    cj = lax.broadcasted_iota(jnp.int32, (nq, ncol), 1)
    rt = ri & (tlen - 1)
    rseq = (ri & (SUBLANES - 1)) >> tshift
    rhead = ri >> sshift
    nj = jnp.maximum(cj - w, 0)
    in_cache = cj < w
    rel = jnp.where(in_cache, rt + w - cj, rt - (nj & (tlen - 1))).astype(F32)
    ok_new = ((nj >> tshift) == rseq) & ((nj & (tlen - 1)) <= rt)
    ok = (in_cache & (cj > rt)) | (jnp.logical_not(in_cache) & ok_new)
    slope = jnp.zeros((nq, ncol), F32)
    for h in range(ATT_HEADS):
        slope = jnp.where(rhead == h, ALIBI[h], slope)
    rh1 = lax.broadcasted_iota(jnp.int32, (nq, 1), 0) >> sshift
    sk = jnp.zeros((nq, 1), F32)
    for h in range(ATT_HEADS):
        sk = jnp.where(rh1 == h, sink_ref[h], sk)
    bias = slope * rel
    rseq1 = (lax.broadcasted_iota(jnp.int32, (nq, LANES), 0) & (SUBLANES - 1)) >> tshift

    out = jnp.zeros((nq, LANES), F32)
    for b in range(nseq):
        kc = jnp.concatenate([kc_ref[b], kn], axis=0).astype(BF16)
        vc = jnp.concatenate([vc_ref[b], v], axis=0).astype(BF16)
        s = jnp.where(ok, _dot_nt(qrows, kc) - bias, NEG)
        mx = jnp.maximum(jnp.max(s, axis=-1, keepdims=True), sk)
        p = jnp.exp(s - mx)
        den = jnp.sum(p, axis=-1, keepdims=True) + jnp.exp(sk - mx)
        o = _dot(p.astype(BF16), vc) / den
        out = jnp.where(rseq1 == b, o, out)
    pairs = []
    for m in range(ATT_HEADS // 2):
        halves = []
        for e in range(2):
            h = 2 * m + e
            blk = out[h * SUBLANES:(h + 1) * SUBLANES, :]
            halves.append(blk if h // ATT_GROUP == e else pltpu.roll(blk, HEAD_DIM, axis=1))
        pairs.append(jnp.where(lane_lo, halves[0], halves[1]))
    y_ref[...] = jnp.concatenate([y_ssd] + pairs, axis=1)


def _sample_mix(packed, sinks, st, kc, vc, alog, dskip, snorm, qnw, knw, *, tlen):
    rows = packed.shape[0]
    nseq = SUBLANES // tlen
    consts = [alog, dskip, snorm, qnw, knw]
    return pl.pallas_call(
        functools.partial(_sample_mix_kernel, tlen=tlen),
        grid_spec=pltpu.PrefetchScalarGridSpec(
            num_scalar_prefetch=1,
            grid=(rows // SUBLANES,),
            in_specs=[
                pl.BlockSpec((SUBLANES, _S_END), lambda i, *_: (i, 0)),
                pl.BlockSpec((nseq, SSD_INNER, SSD_STATE), lambda i, *_: (i, 0, 0)),
                pl.BlockSpec((nseq, WINDOW, KV_WIDTH), lambda i, *_: (i, 0, 0)),
                pl.BlockSpec((nseq, WINDOW, KV_WIDTH), lambda i, *_: (i, 0, 0)),
            ] + [_const_spec(a.shape) for a in consts],
            out_specs=[
                pl.BlockSpec((SUBLANES, D_MODEL), lambda i, *_: (i, 0)),
                pl.BlockSpec((nseq, SSD_INNER, SSD_STATE), lambda i, *_: (i, 0, 0)),
                pl.BlockSpec((SUBLANES, KV_WIDTH), lambda i, *_: (i, 0)),
                pl.BlockSpec((SUBLANES, KV_WIDTH), lambda i, *_: (i, 0)),
            ],
        ),
        out_shape=[
            jax.ShapeDtypeStruct((rows, D_MODEL), F32),
            jax.ShapeDtypeStruct(st.shape, F32),
            jax.ShapeDtypeStruct((rows, KV_WIDTH), F32),
            jax.ShapeDtypeStruct((rows, KV_WIDTH), F32),
        ],
        compiler_params=pltpu.CompilerParams(dimension_semantics=("arbitrary",), vmem_limit_bytes=VMEM_LIMIT),
        name="sample_mix",
    )(sinks, packed, st, kc, vc, *consts)


def _prep_layer(p, i):
    w_in = p['w_in'][i]
    s0, s1, s2, s3, s4 = (SSD_INNER, SSD_INNER + SSD_CONV_DIM, SSD_INNER + SSD_CONV_DIM + SSD_HEADS,
                          SSD_INNER + SSD_CONV_DIM + SSD_HEADS + ATT_WIDTH,
                          SSD_INNER + SSD_CONV_DIM + SSD_HEADS + ATT_WIDTH + KV_WIDTH)
    win = jnp.concatenate([w_in[:, :s1], w_in[:, s2:]], axis=1).astype(BF16)
    wdt = jnp.repeat(w_in[:, s1:s2], HEAD_DIM, axis=1).astype(BF16)

    def per_head(v):
        return jnp.repeat(v, HEAD_DIM)[None, :]

    mixer = (
        p['attn_sinks'][i],
        p['norm1_w'][i][None, :], win, wdt, p['ssd_conv_w'][i], p['ssd_conv_b'][i][None, :],
        per_head(p['dt_bias'][i]), per_head(p['a_log'][i]), per_head(p['d_skip'][i]), p['ssd_norm_w'][i][None, :],
        jnp.tile(p['q_norm_w'][i], ATT_HEADS)[None, :], jnp.tile(p['k_norm_w'][i], ATT_KV_HEADS)[None, :],
        p['w_out'][i].astype(BF16),
    )
    ffn = (
        p['norm2_w'][i][None, :], p['w_up'][i].astype(BF16), p['ffn_conv_w'][i], p['ffn_conv_b'][i][None, :],
        p['w_down'][i].astype(BF16),
    )
    return {'mixer': mixer, 'ffn': ffn}


_TM_MIXER = 256
_TM_FFN = 256


def _to_time_major(a):
    bs, t, c = a.shape
    return a.transpose(1, 0, 2).reshape(t * bs, c)


def _to_seq_major(a, bs):
    return a.reshape(a.shape[0] // bs, bs, a.shape[1]).transpose(1, 0, 2)


def kernel(x_prompt, x_sample, state_ssm, state_ssd_conv, cache_swa_k, cache_swa_v, state_ffn_conv, norm1_w, w_in,
           ssd_conv_w, ssd_conv_b, dt_bias, a_log, d_skip, ssd_norm_w, q_norm_w, k_norm_w, attn_sinks, w_out, norm2_w,
           w_up, ffn_conv_w, ffn_conv_b, w_down):
    p = dict(norm1_w=norm1_w, w_in=w_in, ssd_conv_w=ssd_conv_w, ssd_conv_b=ssd_conv_b, dt_bias=dt_bias, a_log=a_log,
             d_skip=d_skip, ssd_norm_w=ssd_norm_w, q_norm_w=q_norm_w, k_norm_w=k_norm_w, attn_sinks=attn_sinks,
             w_out=w_out, norm2_w=norm2_w, w_up=w_up, ffn_conv_w=ffn_conv_w, ffn_conv_b=ffn_conv_b, w_down=w_down)
    depth = w_in.shape[0]
    b = x_prompt.shape[0]
    bs, tlen, _ = x_sample.shape
    assert cache_swa_k.shape[2] == WINDOW
    head_shape = (SSD_HEADS, HEAD_DIM, SSD_STATE)
    kv_shape = (ATT_KV_HEADS, HEAD_DIM)

    xp = x_prompt
    xs = _to_time_major(x_sample)
    p_states, s_states = [], []
    for i in range(depth):
        w = _prep_layer(p, i)
        sinks, n1, win, wdt, cw, cb, dtb, alog, dskip, snorm, qnw, knw, wout = w['mixer']

        x1, ssm, conv, kc, vc = _mixer_prompt(xp, *w['mixer'], tm=_TM_MIXER)
        xp, ffn_tail = _ffn_prompt(x1, *w['ffn'], tm=_TM_FFN)
        p_states.append((ssm.reshape((b,) + head_shape), conv, kc.reshape((b, WINDOW) + kv_shape),
                         vc.reshape((b, WINDOW) + kv_shape), ffn_tail))

        packed, conv_tail = _sample_in(xs, _to_time_major(state_ssd_conv[i]), n1, win, wdt, cw, cb, dtb)
        packed = _to_seq_major(packed, bs).reshape(bs * tlen, -1)
        y, ssm_s, k_rows, v_rows = _sample_mix(
            packed, sinks, state_ssm[i].reshape(bs, SSD_INNER, SSD_STATE),
            cache_swa_k[i].reshape(bs, WINDOW, KV_WIDTH), cache_swa_v[i].reshape(bs, WINDOW, KV_WIDTH),
            alog, dskip, snorm, qnw, knw, tlen=tlen)
        y = _to_time_major(y.reshape(bs, tlen, D_MODEL))
        xs, ffn_tail_s = _ffn_sample(xs, y, wout, _to_time_major(state_ffn_conv[i]), *w['ffn'])
        k_new = jnp.concatenate([cache_swa_k[i][:, tlen:], k_rows.reshape((bs, tlen) + kv_shape)], axis=1)
        v_new = jnp.concatenate([cache_swa_v[i][:, tlen:], v_rows.reshape((bs, tlen) + kv_shape)], axis=1)
        s_states.append((ssm_s.reshape((bs,) + head_shape), _to_seq_major(conv_tail, bs), k_new, v_new,
                         _to_seq_major(ffn_tail_s, bs)))

    def stacked(states, j):
        return jnp.stack([st[j] for st in states])

    return (xp, _to_seq_major(xs, bs)) + tuple(stacked(p_states, j) for j in range(5)) + tuple(
        stacked(s_states, j) for j in range(5))
```

```python
import functools

import numpy as np
import jax
import jax.numpy as jnp
from jax import lax
from jax.experimental import pallas as pl
from jax.experimental.pallas import tpu as pltpu

F32 = jnp.float32
BF16 = jnp.bfloat16

D_MODEL = 1024
HEAD_DIM = 64
SSD_HEADS = 8
SSD_INNER = SSD_HEADS * HEAD_DIM
SSD_GROUPS = 2
SSD_STATE = 128
SSD_CONV = 4
SSD_CHUNK = 128
SSD_CONV_DIM = SSD_INNER + 2 * SSD_GROUPS * SSD_STATE
ATT_HEADS = 8
ATT_KV_HEADS = 2
ATT_GROUP = ATT_HEADS // ATT_KV_HEADS
ATT_WIDTH = ATT_HEADS * HEAD_DIM
KV_WIDTH = ATT_KV_HEADS * HEAD_DIM
WINDOW = 128
D_FF = 2816
FFN_CONV = 3
EPS = 1e-6
NEG = -1e30

LANES = 128
SUBLANES = 8
VMEM_LIMIT = 56 * 1024 * 1024

_Z0, _X0, _Q0, _K0, _V0, _PEND = 0, 512, 1536, 2048, 2176, 2304
ALIBI = tuple(float(2.0 ** (-8.0 * (h + 1) / ATT_HEADS)) for h in range(ATT_HEADS))


def _rms(x, w):
    ms = jnp.mean(x * x, axis=-1, keepdims=True)
    return x * lax.rsqrt(ms + EPS) * w


def _silu(x):
    return x * jax.nn.sigmoid(x)


def _softplus(x):
    return jnp.maximum(x, 0.0) + jnp.log1p(jnp.exp(-jnp.abs(x)))


def _dot(a, b):
    return jnp.dot(a, b, preferred_element_type=F32)


def _dot_nt(a, b):
    return lax.dot_general(a, b, (((1,), (1,)), ((), ())), preferred_element_type=F32)


def _dot_tn(a, b):
    return lax.dot_general(a, b, (((0,), (0,)), ((), ())), preferred_element_type=F32)


def _const_spec(shape):
    nd = len(shape)
    return pl.BlockSpec(shape, lambda *_: (0,) * nd, pipeline_mode=pl.Buffered(1))


def _full_spec(shape):
    nd = len(shape)
    return pl.BlockSpec(shape, lambda *_: (0,) * nd)


_FFN_COLS = 256


def _ffn_kernel(*refs, tm, pad, shift, carry, out_proj):
    refs = list(refs)
    x_ref = refs.pop(0)
    if out_proj:
        y_ref = refs.pop(0)
        wout_ref = refs.pop(0)
    if not carry:
        prev_ref = refs.pop(0)
    n2_ref, wup_ref, cw_ref, cb_ref, wdn_ref, o_ref, tail_ref, hp_ref, act_ref = refs

    if carry:
        @pl.when(pl.program_id(1) == 0)
        def _():
            hp_ref[0:pad, :] = jnp.zeros((pad, 2 * D_FF), F32)
        x = x_ref[0]
    else:
        hp_ref[0:pad, :] = prev_ref[...]
        x = x_ref[...]

    if out_proj:
        x = x + _dot(y_ref[...].astype(BF16), wout_ref[...])
    hn = _rms(x, n2_ref[...]).astype(BF16)
    hp_ref[pad:pad + tm, :] = _dot(hn, wup_ref[...])

    for j in range(0, D_FF, _FFN_COLS):
        def conv(c0):
            cols = slice(c0, c0 + _FFN_COLS)
            acc = hp_ref[pad:pad + tm, cols] * cw_ref[2:3, cols]
            acc = acc + hp_ref[pad - shift:pad - shift + tm, cols] * cw_ref[1:2, cols]
            acc = acc + hp_ref[pad - 2 * shift:pad - 2 * shift + tm, cols] * cw_ref[0:1, cols]
            return acc + cb_ref[:, cols]
        act_ref[:, j:j + _FFN_COLS] = (_silu(conv(j)) * conv(D_FF + j)).astype(BF16)

    out = x + _dot(act_ref[...], wdn_ref[...])
    tail = hp_ref[tm + pad - 2 * shift:tm + pad, :]
    if carry:
        o_ref[0] = out
        tail_ref[0] = tail
        hp_ref[0:pad, :] = hp_ref[tm:tm + pad, :]
    else:
        o_ref[...] = out
        tail_ref[...] = tail


def _ffn_prompt(x, n2, wup, cw, cb, wdn, *, tm):
    b, s, _ = x.shape
    pad = SUBLANES
    kern = functools.partial(_ffn_kernel, tm=tm, pad=pad, shift=1, carry=True, out_proj=False)
    return pl.pallas_call(
        kern,
        grid=(b, s // tm),
        in_specs=[
            pl.BlockSpec((1, tm, D_MODEL), lambda i, t: (i, t, 0)),
            _const_spec((1, D_MODEL)),
            _const_spec((D_MODEL, 2 * D_FF)),
            _const_spec((FFN_CONV, 2 * D_FF)),
            _const_spec((1, 2 * D_FF)),
            _const_spec((D_FF, D_MODEL)),
        ],
        out_specs=[
            pl.BlockSpec((1, tm, D_MODEL), lambda i, t: (i, t, 0)),
            pl.BlockSpec((1, FFN_CONV - 1, 2 * D_FF), lambda i, t: (i, 0, 0)),
        ],
        out_shape=[
            jax.ShapeDtypeStruct((b, s, D_MODEL), F32),
            jax.ShapeDtypeStruct((b, FFN_CONV - 1, 2 * D_FF), F32),
        ],
        scratch_shapes=[
            pltpu.VMEM((tm + pad, 2 * D_FF), F32),
            pltpu.VMEM((tm, D_FF), BF16),
        ],
        compiler_params=pltpu.CompilerParams(
            dimension_semantics=("arbitrary", "arbitrary"), vmem_limit_bytes=VMEM_LIMIT),
        name="ffn_prompt",
    )(x, n2, wup, cw, cb, wdn)


def _ffn_sample(x, y, wout, prev, n2, wup, cw, cb, wdn):
    rows = x.shape[0]
    bs = prev.shape[0] // (FFN_CONV - 1)
    pad = (FFN_CONV - 1) * bs
    kern = functools.partial(_ffn_kernel, tm=rows, pad=pad, shift=bs, carry=False, out_proj=True)
    return pl.pallas_call(
        kern,
        grid=(1,),
        in_specs=[
            _const_spec((rows, D_MODEL)),
            _const_spec((rows, D_MODEL)),
            _const_spec((D_MODEL, D_MODEL)),
            _const_spec((pad, 2 * D_FF)),
            _const_spec((1, D_MODEL)),
            _const_spec((D_MODEL, 2 * D_FF)),
            _const_spec((FFN_CONV, 2 * D_FF)),
            _const_spec((1, 2 * D_FF)),
            _const_spec((D_FF, D_MODEL)),
        ],
        out_specs=[
            _full_spec((rows, D_MODEL)),
            _full_spec((pad, 2 * D_FF)),
        ],
        out_shape=[
            jax.ShapeDtypeStruct((rows, D_MODEL), F32),
            jax.ShapeDtypeStruct((pad, 2 * D_FF), F32),
        ],
        scratch_shapes=[
            pltpu.VMEM((rows + pad, 2 * D_FF), F32),
            pltpu.VMEM((rows, D_FF), BF16),
        ],
        compiler_params=pltpu.CompilerParams(
            dimension_semantics=("arbitrary",), vmem_limit_bytes=VMEM_LIMIT),
        name="ffn_sample",
    )(x, y, wout, prev, n2, wup, cw, cb, wdn)


def _tri_cumsum(x, ltri):
    h1 = x.astype(BF16)
    r1 = x - h1.astype(F32)
    h2 = r1.astype(BF16)
    h3 = (r1 - h2.astype(F32)).astype(BF16)
    return _dot(ltri, h1) + _dot(ltri, h2) + _dot(ltri, h3)


def _half_variants(a, lane_lo):
    ar = pltpu.roll(a, HEAD_DIM, axis=1)
    zero = jnp.zeros_like(a)
    return {
        (0, 0): jnp.where(lane_lo, a, zero).astype(BF16),
        (1, 1): jnp.where(lane_lo, zero, a).astype(BF16),
        (1, 0): jnp.where(lane_lo, ar, zero).astype(BF16),
        (0, 1): jnp.where(lane_lo, zero, ar).astype(BF16),
    }


def _ssd_chunk(xc, dt, z, st_ref, a, dskip, snorm, ltri, causal, lane_lo):
    c = SSD_CHUNK
    xs = xc[:, 0:SSD_INNER]
    acum = _tri_cumsum(dt * a, ltri)
    alast = acum[c - 1:c, :]
    eac = jnp.exp(acum)
    xdt = xs * dt
    xw = (xs * (jnp.exp(alast - acum) * dt)).astype(BF16)
    acum_t = acum.T
    cdec = jnp.exp(alast)
    gw = SSD_INNER // SSD_GROUPS
    ys = []
    for g in range(SSD_GROUPS):
        b0 = SSD_INNER + g * SSD_STATE
        c0 = SSD_INNER + SSD_GROUPS * SSD_STATE + g * SSD_STATE
        bm = xc[:, b0:b0 + SSD_STATE].astype(BF16)
        cm = xc[:, c0:c0 + SSD_STATE].astype(BF16)
        cb = _dot_nt(cm, bm)
        st_g = st_ref[:, g * gw:(g + 1) * gw]
        yoff = _dot(cm, st_g.astype(BF16))
        st_ref[:, g * gw:(g + 1) * gw] = st_g * cdec[:, g * gw:(g + 1) * gw] + _dot_tn(bm, xw[:, g * gw:(g + 1) * gw])
        for m in range(gw // LANES):
            pm = g * (gw // LANES) + m
            lanes = slice(pm * LANES, (pm + 1) * LANES)
            a_pair = acum[:, lanes]
            a_roll = pltpu.roll(a_pair, HEAD_DIM, axis=1)
            ms = []
            for e in range(2):
                h = 2 * pm + e
                colb = jnp.where(lane_lo, a_pair, a_roll) if e == 0 else jnp.where(lane_lo, a_roll, a_pair)
                rowb = jnp.broadcast_to(acum_t[h * HEAD_DIM:h * HEAD_DIM + 1, :], (c, c))
                dec = jnp.exp(jnp.where(causal, colb - rowb, NEG))
                ms.append((cb * dec).astype(BF16))
            xd = xdt[:, lanes]
            rhs = jnp.concatenate([jnp.where(lane_lo, xd, 0.0), jnp.where(lane_lo, 0.0, xd)], axis=0).astype(BF16)
            ydiag = _dot(jnp.concatenate(ms, axis=1), rhs)
            ys.append(ydiag + yoff[:, m * LANES:(m + 1) * LANES] * eac[:, lanes])
    y = (jnp.concatenate(ys, axis=1) + xs * dskip) * _silu(z)
    return jnp.concatenate(
        [_rms(y[:, g * gw:(g + 1) * gw], snorm[:, g * gw:(g + 1) * gw]) for g in range(SSD_GROUPS)], axis=1)


def _norm_k(k, knw, lane_lo):
    k2 = k * k
    r0 = lax.rsqrt(jnp.sum(jnp.where(lane_lo, k2, 0.0), axis=-1, keepdims=True) / HEAD_DIM + EPS)
    r1 = lax.rsqrt(jnp.sum(jnp.where(lane_lo, 0.0, k2), axis=-1, keepdims=True) / HEAD_DIM + EPS)
    return k * jnp.where(lane_lo, r0, r1) * knw


def _attn_block(q, qnw, kvars, vvars, rel, valid, sink_ref, lane_lo):
    qw = q * qnw
    q2 = q * q
    outs = []
    for m in range(ATT_HEADS // 2):
        lanes = slice(m * LANES, (m + 1) * LANES)
        kv = (2 * m) // ATT_GROUP
        qp = qw[:, lanes].astype(BF16)
        q2p = q2[:, lanes]
        ps, invs = [], []
        for e in range(2):
            h = 2 * m + e
            ssq = jnp.sum(jnp.where(lane_lo, q2p, 0.0) if e == 0 else jnp.where(lane_lo, 0.0, q2p),
                          axis=-1, keepdims=True)
            rs = lax.rsqrt(ssq / HEAD_DIM + EPS) * (HEAD_DIM ** -0.5)
            s = _dot_nt(qp, kvars[(kv, e)]) * rs - ALIBI[h] * rel
            s = jnp.where(valid, s, NEG)
            sk = sink_ref[h]
            mx = jnp.maximum(jnp.max(s, axis=-1, keepdims=True), sk)
            p = jnp.exp(s - mx)
            den = jnp.sum(p, axis=-1, keepdims=True) + jnp.exp(sk - mx)
            ps.append(p.astype(BF16))
            invs.append(1.0 / den)
        rhs = jnp.concatenate([vvars[(kv, 0)], vvars[(kv, 1)]], axis=0)
        o = _dot(jnp.concatenate(ps, axis=1), rhs)
        outs.append(o * jnp.where(lane_lo, invs[0], invs[1]))
    return jnp.concatenate(outs, axis=1)


def _mixer_prompt_kernel(sink_ref, x_ref, n1_ref, win_ref, wdt_ref, cw_ref, cb_ref, dtb_ref, alog_ref, dskip_ref,
                         snorm_ref, qnw_ref, knw_ref, wout_ref,
                         o_ref, ssm_ref, conv_ref, kc_ref, vc_ref,
                         proj_ref, dt_ref, cbuf_ref, st_ref, kprev_ref, vprev_ref, y_ref, *, tm):
    t = pl.program_id(1)
    nt = pl.num_programs(1)
    c = SSD_CHUNK
    pad = SUBLANES

    @pl.when(t == 0)
    def _():
        st_ref[...] = jnp.zeros(st_ref.shape, F32)
        cbuf_ref[0:pad, :] = jnp.zeros((pad, SSD_CONV_DIM), F32)
        kprev_ref[...] = jnp.zeros(kprev_ref.shape, F32)
        vprev_ref[...] = jnp.zeros(vprev_ref.shape, F32)

    x = x_ref[0]
    hn = _rms(x, n1_ref[...]).astype(BF16)
    proj_ref[:, 0:_X0] = _dot(hn, win_ref[:, 0:_X0])
    cbuf_ref[pad:pad + tm, :] = _dot(hn, win_ref[:, _X0:_Q0])
    proj_ref[:, _X0:] = _dot(hn, win_ref[:, _Q0:])
    dt_ref[...] = _softplus(_dot(hn, wdt_ref[...]) + dtb_ref[...])

    row = lax.broadcasted_iota(jnp.int32, (c, c), 0)
    col = lax.broadcasted_iota(jnp.int32, (c, c), 1)
    causal = row >= col
    ltri = jnp.where(causal, 1.0, 0.0).astype(BF16)
    lane_lo = col < HEAD_DIM
    a = -jnp.exp(alog_ref[...])

    qi = lax.broadcasted_iota(jnp.int32, (c, 2 * c), 0)
    sj = lax.broadcasted_iota(jnp.int32, (c, 2 * c), 1)
    reli = qi + c - sj
    rel = reli.astype(F32)
    in_window = (reli >= 0) & (reli < WINDOW)
    first_col = jnp.where(t == 0, c, 0)

    kprev = _half_variants(kprev_ref[...], lane_lo)
    vprev = _half_variants(vprev_ref[...], lane_lo)
    kn = v = None
    for ci in range(tm // c):
        r0 = ci * c
        rows = slice(r0, r0 + c)
        acc = cb_ref[...] + cbuf_ref[pad + r0:pad + r0 + c, :] * cw_ref[SSD_CONV - 1:SSD_CONV, :]
        for i in range(1, SSD_CONV):
            acc = acc + cbuf_ref[pad + r0 - i:pad + r0 - i + c, :] * cw_ref[SSD_CONV - 1 - i:SSD_CONV - i, :]
        xc = _silu(acc)
        y_ssd = _ssd_chunk(xc, dt_ref[rows, :], proj_ref[rows, _Z0:_Z0 + SSD_INNER], st_ref, a, dskip_ref[...],
                           snorm_ref[...], ltri, causal, lane_lo)
        y_ref[rows, 0:SSD_INNER] = y_ssd.astype(BF16)
        qo, ko, vo = _Q0 - SSD_CONV_DIM, _K0 - SSD_CONV_DIM, _V0 - SSD_CONV_DIM
        q = proj_ref[rows, qo:qo + ATT_WIDTH]
        kn = _norm_k(proj_ref[rows, ko:ko + KV_WIDTH], knw_ref[...], lane_lo)
        v = proj_ref[rows, vo:vo + KV_WIDTH]
        kcur = _half_variants(kn, lane_lo)
        vcur = _half_variants(v, lane_lo)
        kall = {key: jnp.concatenate([kprev[key], kcur[key]], axis=0) for key in kcur}
        vall = {key: jnp.concatenate([vprev[key], vcur[key]], axis=0) for key in vcur}
        valid = in_window & (sj >= first_col) if ci == 0 else in_window
        y_att = _attn_block(q, qnw_ref[...], kall, vall, rel, valid, sink_ref, lane_lo)
        y_ref[rows, SSD_INNER:] = y_att.astype(BF16)
        kprev, vprev = kcur, vcur

    o_ref[0] = x + _dot(y_ref[...], wout_ref[...])
    cbuf_ref[0:pad, :] = cbuf_ref[tm:tm + pad, :]
    kprev_ref[...] = kn
    vprev_ref[...] = v

    @pl.when(t == nt - 1)
    def _():
        ssm_ref[0] = st_ref[...].T
        conv_ref[0] = cbuf_ref[tm + pad - (SSD_CONV - 1):tm + pad, :]
        kc_ref[0] = kn
        vc_ref[0] = v


def _mixer_prompt(x, sinks, n1, win, wdt, cw, cb, dtb, alog, dskip, snorm, qnw, knw, wout, *, tm):
    b, s, _ = x.shape
    pw = _PEND - SSD_CONV_DIM
    kern = functools.partial(_mixer_prompt_kernel, tm=tm)
    consts = [n1, win, wdt, cw, cb, dtb, alog, dskip, snorm, qnw, knw, wout]
    return pl.pallas_call(
        kern,
        grid_spec=pltpu.PrefetchScalarGridSpec(
            num_scalar_prefetch=1,
            grid=(b, s // tm),
            in_specs=[pl.BlockSpec((1, tm, D_MODEL), lambda i, t, *_: (i, t, 0))]
            + [_const_spec(w.shape) for w in consts],
            out_specs=[
                pl.BlockSpec((1, tm, D_MODEL), lambda i, t, *_: (i, t, 0)),
                pl.BlockSpec((1, SSD_INNER, SSD_STATE), lambda i, t, *_: (i, 0, 0)),
                pl.BlockSpec((1, SSD_CONV - 1, SSD_CONV_DIM), lambda i, t, *_: (i, 0, 0)),
                pl.BlockSpec((1, WINDOW, KV_WIDTH), lambda i, t, *_: (i, 0, 0)),
                pl.BlockSpec((1, WINDOW, KV_WIDTH), lambda i, t, *_: (i, 0, 0)),
            ],
            scratch_shapes=[
                pltpu.VMEM((tm, pw), F32),
                pltpu.VMEM((tm, SSD_INNER), F32),
                pltpu.VMEM((tm + SUBLANES, SSD_CONV_DIM), F32),
                pltpu.VMEM((SSD_STATE, SSD_INNER), F32),
                pltpu.VMEM((WINDOW, KV_WIDTH), F32),
                pltpu.VMEM((WINDOW, KV_WIDTH), F32),
                pltpu.VMEM((tm, D_MODEL), BF16),
            ],
        ),
        out_shape=[
            jax.ShapeDtypeStruct((b, s, D_MODEL), F32),
            jax.ShapeDtypeStruct((b, SSD_INNER, SSD_STATE), F32),
            jax.ShapeDtypeStruct((b, SSD_CONV - 1, SSD_CONV_DIM), F32),
            jax.ShapeDtypeStruct((b, WINDOW, KV_WIDTH), F32),
            jax.ShapeDtypeStruct((b, WINDOW, KV_WIDTH), F32),
        ],
        compiler_params=pltpu.CompilerParams(
            dimension_semantics=("arbitrary", "arbitrary"), vmem_limit_bytes=VMEM_LIMIT),
        name="mixer_prompt",
    )(sinks, x, *consts)


_S_Z0, _S_X0, _S_DT0, _S_Q0, _S_K0, _S_V0, _S_END = 0, 512, 1536, 2048, 2560, 2688, 2816


def _sample_in_kernel(x_ref, prev_ref, n1_ref, win_ref, wdt_ref, cw_ref, cb_ref, dtb_ref, o_ref, tail_ref, cbuf_ref,
                      *, rows, bs):
    pad = (SSD_CONV - 1) * bs
    hn = _rms(x_ref[...], n1_ref[...]).astype(BF16)
    o_ref[:, _S_Z0:_S_X0] = _dot(hn, win_ref[:, _Z0:_X0])
    cbuf_ref[0:pad, :] = prev_ref[...]
    cbuf_ref[pad:pad + rows, :] = _dot(hn, win_ref[:, _X0:_Q0])
    acc = cb_ref[...] + cbuf_ref[pad:pad + rows, :] * cw_ref[SSD_CONV - 1:SSD_CONV, :]
    for i in range(1, SSD_CONV):
        acc = acc + cbuf_ref[pad - i * bs:pad - i * bs + rows, :] * cw_ref[SSD_CONV - 1 - i:SSD_CONV - i, :]
    o_ref[:, _S_X0:_S_DT0] = _silu(acc)
    o_ref[:, _S_DT0:_S_Q0] = _softplus(_dot(hn, wdt_ref[...]) + dtb_ref[...])
    o_ref[:, _S_Q0:_S_END] = _dot(hn, win_ref[:, _Q0:_PEND])
    tail_ref[...] = cbuf_ref[rows:rows + pad, :]


def _sample_in(x, prev, n1, win, wdt, cw, cb, dtb):
    rows = x.shape[0]
    pad = prev.shape[0]
    bs = pad // (SSD_CONV - 1)
    args = (x, prev, n1, win, wdt, cw, cb, dtb)
    return pl.pallas_call(
        functools.partial(_sample_in_kernel, rows=rows, bs=bs),
        grid=(1,),
        in_specs=[_const_spec(a.shape) for a in args],
        out_specs=[_full_spec((rows, _S_END)), _full_spec((pad, SSD_CONV_DIM))],
        out_shape=[jax.ShapeDtypeStruct((rows, _S_END), F32), jax.ShapeDtypeStruct((pad, SSD_CONV_DIM), F32)],
        scratch_shapes=[pltpu.VMEM((rows + pad, SSD_CONV_DIM), F32)],
        compiler_params=pltpu.CompilerParams(dimension_semantics=("arbitrary",), vmem_limit_bytes=VMEM_LIMIT),
        name="sample_in",
    )(*args)


def _sample_mix_kernel(sink_ref, p_ref, st_ref, kc_ref, vc_ref, alog_ref, dskip_ref, snorm_ref, qnw_ref, knw_ref,
                       *rest, tlen):
    y_ref, sto_ref, ko_ref, vo_ref = rest[-4:]
    nseq = SUBLANES // tlen
    w = WINDOW
    pk = p_ref[...]
    z = pk[:, _S_Z0:_S_X0]
    xs = pk[:, _S_X0:_S_X0 + SSD_INNER]
    dt = pk[:, _S_DT0:_S_Q0]
    q = pk[:, _S_Q0:_S_K0]
    k = pk[:, _S_K0:_S_V0]
    v = pk[:, _S_V0:_S_END]
    gw = SSD_INNER // SSD_GROUPS
    bms = [pk[:, _S_X0 + SSD_INNER + g * SSD_STATE:_S_X0 + SSD_INNER + (g + 1) * SSD_STATE]
           for g in range(SSD_GROUPS)]
    c0 = _S_X0 + SSD_INNER + SSD_GROUPS * SSD_STATE
    cms = [pk[:, c0 + g * SSD_STATE:c0 + (g + 1) * SSD_STATE] for g in range(SSD_GROUPS)]

    rowi = lax.broadcasted_iota(jnp.int32, (SUBLANES, SSD_INNER), 0)
    assert tlen & (tlen - 1) == 0 and SUBLANES % tlen == 0
    tshift = tlen.bit_length() - 1
    sshift = SUBLANES.bit_length() - 1
    tpos = rowi & (tlen - 1)
    seq = rowi >> tshift
    seq_g = lax.broadcasted_iota(jnp.int32, (SUBLANES, gw), 0) >> tshift
    lane_lo = lax.broadcasted_iota(jnp.int32, (SUBLANES, LANES), 1) < HEAD_DIM

    dta = dt * (-jnp.exp(alog_ref[...]))
    acum = dta
    for s in range(1, tlen):
        acum = acum + jnp.where(tpos >= s, pltpu.roll(dta, s, axis=0), 0.0)
    tot = acum[tlen - 1:tlen, :]
    for b in range(1, nseq):
        tot = jnp.where(seq == b, acum[(b + 1) * tlen - 1:(b + 1) * tlen, :], tot)
    eac = jnp.exp(acum)
    cdec = jnp.exp(tot)
    xdt = xs * dt
    xw = xs * (jnp.exp(tot - acum) * dt)

    ydiag = jnp.zeros((SUBLANES, SSD_INNER), F32)
    for s in range(tlen):
        xsh = pltpu.roll(xdt, s, axis=0) if s else xdt
        ash = pltpu.roll(acum, s, axis=0) if s else acum
        cbs = []
        for g in range(SSD_GROUPS):
            bsh = pltpu.roll(bms[g], s, axis=0) if s else bms[g]
            cbs.append(jnp.broadcast_to(jnp.sum(cms[g] * bsh, axis=-1, keepdims=True), (SUBLANES, gw)))
        term = jnp.concatenate(cbs, axis=1) * jnp.exp(acum - ash) * xsh
        ydiag = ydiag + jnp.where(tpos >= s, term, 0.0)

    yoff = []
    for g in range(SSD_GROUPS):
        cm = cms[g].astype(BF16)
        bm = bms[g].astype(BF16)
        yo = None
        for b in range(nseq):
            h0 = st_ref[b, g * gw:(g + 1) * gw, :]
            yb = _dot_nt(cm, h0.astype(BF16))
            yo = yb if yo is None else jnp.where(seq_g == b, yb, yo)
            xwb = jnp.where(seq_g == b, xw[:, g * gw:(g + 1) * gw], 0.0).astype(BF16)
            cseq = jnp.broadcast_to(cdec[b * tlen:b * tlen + 1, :], (SUBLANES, SSD_INNER))
            blocks = []
            for hh in range(gw // HEAD_DIM):
                pm = (g * gw + hh * HEAD_DIM) // LANES
                cpair = cseq[:, pm * LANES:(pm + 1) * LANES]
                croll = pltpu.roll(cpair, HEAD_DIM, axis=1)
                chead = jnp.where(lane_lo, cpair, croll) if hh % 2 == 0 else jnp.where(lane_lo, croll, cpair)
                blocks += [chead] * (HEAD_DIM // SUBLANES)
            sto_ref[b, g * gw:(g + 1) * gw, :] = h0 * jnp.concatenate(blocks, axis=0) + _dot_tn(xwb, bm)
        yoff.append(yo)
    y = (ydiag + jnp.concatenate(yoff, axis=1) * eac + xs * dskip_ref[...]) * _silu(z)
    y_ssd = jnp.concatenate(
        [_rms(y[:, g * gw:(g + 1) * gw], snorm_ref[:, g * gw:(g + 1) * gw]) for g in range(SSD_GROUPS)], axis=1)

    kn = _norm_k(k, knw_ref[...], lane_lo)
    for b in range(nseq):
        ko_ref[b, 0:w - tlen, :] = kc_ref[b, tlen:w, :]
        vo_ref[b, 0:w - tlen, :] = vc_ref[b, tlen:w, :]
        ko_ref[b, w - tlen:w, :] = kn[b * tlen:(b + 1) * tlen, :]
        vo_ref[b, w - tlen:w, :] = v[b * tlen:(b + 1) * tlen, :]
    qw = q * qnw_ref[...]
    q2 = q * q
    pieces = []
    for h in range(ATT_HEADS):
        kv = h // ATT_GROUP
        lanes = slice((h // 2) * LANES, (h // 2 + 1) * LANES)
        own = lane_lo if h % 2 == 0 else jnp.logical_not(lane_lo)
        ssq = jnp.sum(jnp.where(own, q2[:, lanes], 0.0), axis=-1, keepdims=True)
        qh = jnp.where(own, qw[:, lanes], 0.0) * (lax.rsqrt(ssq / HEAD_DIM + EPS) * (HEAD_DIM ** -0.5))
        pieces.append(qh if h % 2 == kv else pltpu.roll(qh, HEAD_DIM, axis=1))
    qrows = jnp.concatenate(pieces, axis=0).astype(BF16)

    nq = ATT_HEADS * SUBLANES
    ncol = w + SUBLANES
    ri = lax.broadcasted_iota(jnp.int32, (nq, ncol), 0)
    cj = lax.broadcasted_iota(jnp.int32, (nq, ncol), 1)
    rt = ri & (tlen - 1)
    rseq = (ri & (SUBLANES - 1)) >> tshift
    rhead = ri >> sshift
    nj = jnp.maximum(cj - w, 0)
    in_cache = cj < w
    rel = jnp.where(in_cache, rt + w - cj, rt - (nj & (tlen - 1))).astype(F32)
    ok_new = ((nj >> tshift) == rseq) & ((nj & (tlen - 1)) <= rt)
    ok = (in_cache & (cj > rt)) | (jnp.logical_not(in_cache) & ok_new)
    slope = jnp.zeros((nq, ncol), F32)
    for h in range(ATT_HEADS):
        slope = jnp.where(rhead == h, ALIBI[h], slope)
    rh1 = lax.broadcasted_iota(jnp.int32, (nq, 1), 0) >> sshift
    sk = jnp.zeros((nq, 1), F32)
    for h in range(ATT_HEADS):
        sk = jnp.where(rh1 == h, sink_ref[h], sk)
    bias = slope * rel
    rseq1 = (lax.broadcasted_iota(jnp.int32, (nq, LANES), 0) & (SUBLANES - 1)) >> tshift

    out = jnp.zeros((nq, LANES), F32)
    for b in range(nseq):
        kc = jnp.concatenate([kc_ref[b], kn], axis=0).astype(BF16)
        vc = jnp.concatenate([vc_ref[b], v], axis=0).astype(BF16)
        s = jnp.where(ok, _dot_nt(qrows, kc) - bias, NEG)
        mx = jnp.maximum(jnp.max(s, axis=-1, keepdims=True), sk)
        p = jnp.exp(s - mx)
        den = jnp.sum(p, axis=-1, keepdims=True) + jnp.exp(sk - mx)
        o = _dot(p.astype(BF16), vc) / den
        out = jnp.where(rseq1 == b, o, out)
    pairs = []
    for m in range(ATT_HEADS // 2):
        halves = []
        for e in range(2):
            h = 2 * m + e
            blk = out[h * SUBLANES:(h + 1) * SUBLANES, :]
            halves.append(blk if h // ATT_GROUP == e else pltpu.roll(blk, HEAD_DIM, axis=1))
        pairs.append(jnp.where(lane_lo, halves[0], halves[1]))
    y_ref[...] = jnp.concatenate([y_ssd] + pairs, axis=1)


def _sample_mix(layer, packed, sinks, st, kc, vc, alog, dskip, snorm, qnw, knw, prev_outs, *, tlen):
    rows = packed.shape[0]
    nseq = SUBLANES // tlen
    consts = [alog, dskip, snorm, qnw, knw]
    carried = [] if prev_outs is None else list(prev_outs)
    n_in = 1 + 4 + len(consts)

    def slab(shape):
        return pl.BlockSpec((None, nseq) + shape, lambda i, *_: (layer, i, 0, 0))

    return pl.pallas_call(
        functools.partial(_sample_mix_kernel, tlen=tlen),
        grid_spec=pltpu.PrefetchScalarGridSpec(
            num_scalar_prefetch=1,
            grid=(rows // SUBLANES,),
            in_specs=[
                pl.BlockSpec((SUBLANES, _S_END), lambda i, *_: (i, 0)),
                slab((SSD_INNER, SSD_STATE)), slab((WINDOW, KV_WIDTH)), slab((WINDOW, KV_WIDTH)),
            ] + [_const_spec(a.shape) for a in consts] + [pl.BlockSpec(memory_space=pl.ANY)] * len(carried),
            out_specs=[
                pl.BlockSpec((SUBLANES, D_MODEL), lambda i, *_: (i, 0)),
                slab((SSD_INNER, SSD_STATE)), slab((WINDOW, KV_WIDTH)), slab((WINDOW, KV_WIDTH)),
            ],
        ),
        out_shape=[
            jax.ShapeDtypeStruct((rows, D_MODEL), F32),
            jax.ShapeDtypeStruct(st.shape, F32),
            jax.ShapeDtypeStruct(kc.shape, F32),
            jax.ShapeDtypeStruct(vc.shape, F32),
        ],
        input_output_aliases={n_in + j: 1 + j for j in range(len(carried))},
        compiler_params=pltpu.CompilerParams(dimension_semantics=("arbitrary",), vmem_limit_bytes=VMEM_LIMIT),
        name="sample_mix",
    )(sinks, packed, st, kc, vc, *consts, *carried)


def _prep_layer(p, i):
    w_in = p['w_in'][i]
    s0, s1, s2, s3, s4 = (SSD_INNER, SSD_INNER + SSD_CONV_DIM, SSD_INNER + SSD_CONV_DIM + SSD_HEADS,
                          SSD_INNER + SSD_CONV_DIM + SSD_HEADS + ATT_WIDTH,
                          SSD_INNER + SSD_CONV_DIM + SSD_HEADS + ATT_WIDTH + KV_WIDTH)
    win = jnp.concatenate([w_in[:, :s1], w_in[:, s2:]], axis=1).astype(BF16)
    wdt = jnp.repeat(w_in[:, s1:s2], HEAD_DIM, axis=1).astype(BF16)

    def per_head(v):
        return jnp.repeat(v, HEAD_DIM)[None, :]

    mixer = (
        p['attn_sinks'][i],
        p['norm1_w'][i][None, :], win, wdt, p['ssd_conv_w'][i], p['ssd_conv_b'][i][None, :],
        per_head(p['dt_bias'][i]), per_head(p['a_log'][i]), per_head(p['d_skip'][i]), p['ssd_norm_w'][i][None, :],
        jnp.tile(p['q_norm_w'][i], ATT_HEADS)[None, :], jnp.tile(p['k_norm_w'][i], ATT_KV_HEADS)[None, :],
        p['w_out'][i].astype(BF16),
    )
    ffn = (
        p['norm2_w'][i][None, :], p['w_up'][i].astype(BF16), p['ffn_conv_w'][i], p['ffn_conv_b'][i][None, :],
        p['w_down'][i].astype(BF16),
    )
    return {'mixer': mixer, 'ffn': ffn}


_TM_MIXER = 256
_TM_FFN = 512


def _to_time_major(a):
    bs, t, c = a.shape
    return a.transpose(1, 0, 2).reshape(t * bs, c)


def _to_seq_major(a, bs):
    return a.reshape(a.shape[0] // bs, bs, a.shape[1]).transpose(1, 0, 2)


def kernel(x_prompt, x_sample, state_ssm, state_ssd_conv, cache_swa_k, cache_swa_v, state_ffn_conv, norm1_w, w_in,
           ssd_conv_w, ssd_conv_b, dt_bias, a_log, d_skip, ssd_norm_w, q_norm_w, k_norm_w, attn_sinks, w_out, norm2_w,
           w_up, ffn_conv_w, ffn_conv_b, w_down):
    p = dict(norm1_w=norm1_w, w_in=w_in, ssd_conv_w=ssd_conv_w, ssd_conv_b=ssd_conv_b, dt_bias=dt_bias, a_log=a_log,
             d_skip=d_skip, ssd_norm_w=ssd_norm_w, q_norm_w=q_norm_w, k_norm_w=k_norm_w, attn_sinks=attn_sinks,
             w_out=w_out, norm2_w=norm2_w, w_up=w_up, ffn_conv_w=ffn_conv_w, ffn_conv_b=ffn_conv_b, w_down=w_down)
    depth = w_in.shape[0]
    b = x_prompt.shape[0]
    bs, tlen, _ = x_sample.shape
    assert cache_swa_k.shape[2] == WINDOW
    head_shape = (SSD_HEADS, HEAD_DIM, SSD_STATE)
    kv_shape = (ATT_KV_HEADS, HEAD_DIM)

    xp = x_prompt
    xs = _to_time_major(x_sample)
    ssm_all = state_ssm.reshape(depth, bs, SSD_INNER, SSD_STATE)
    kc_all = cache_swa_k.reshape(depth, bs, WINDOW, KV_WIDTH)
    vc_all = cache_swa_v.reshape(depth, bs, WINDOW, KV_WIDTH)
    s_outs = None
    p_states, s_states = [], []
    for i in range(depth):
        w = _prep_layer(p, i)
        sinks, n1, win, wdt, cw, cb, dtb, alog, dskip, snorm, qnw, knw, wout = w['mixer']

        x1, ssm, conv, kc, vc = _mixer_prompt(xp, *w['mixer'], tm=_TM_MIXER)
        xp, ffn_tail = _ffn_prompt(x1, *w['ffn'], tm=_TM_FFN)
        p_states.append((ssm.reshape((b,) + head_shape), conv, kc.reshape((b, WINDOW) + kv_shape),
                         vc.reshape((b, WINDOW) + kv_shape), ffn_tail))

        packed, conv_tail = _sample_in(xs, _to_time_major(state_ssd_conv[i]), n1, win, wdt, cw, cb, dtb)
        packed = _to_seq_major(packed, bs).reshape(bs * tlen, -1)
        y, *s_outs = _sample_mix(i, packed, sinks, ssm_all, kc_all, vc_all, alog, dskip, snorm, qnw, knw, s_outs,
                                 tlen=tlen)
        y = _to_time_major(y.reshape(bs, tlen, D_MODEL))
        xs, ffn_tail_s = _ffn_sample(xs, y, wout, _to_time_major(state_ffn_conv[i]), *w['ffn'])
        s_states.append((_to_seq_major(conv_tail, bs), _to_seq_major(ffn_tail_s, bs)))

    def stacked(states, j):
        return jnp.stack([st[j] for st in states])

    ssm_s, k_s, v_s = s_outs
    return (xp, _to_seq_major(xs, bs)) + tuple(stacked(p_states, j) for j in range(5)) + (
        ssm_s.reshape((depth, bs) + head_shape), stacked(s_states, 0),
        k_s.reshape((depth, bs, WINDOW) + kv_shape), v_s.reshape((depth, bs, WINDOW) + kv_shape),
        stacked(s_states, 1))
```

```python
import functools

import numpy as np
import jax
import jax.numpy as jnp
from jax import lax
from jax.experimental import pallas as pl
from jax.experimental.pallas import tpu as pltpu

F32 = jnp.float32
BF16 = jnp.bfloat16

D_MODEL = 1024
HEAD_DIM = 64
SSD_HEADS = 8
SSD_INNER = SSD_HEADS * HEAD_DIM
SSD_GROUPS = 2
SSD_STATE = 128
SSD_CONV = 4
SSD_CHUNK = 128
SSD_CONV_DIM = SSD_INNER + 2 * SSD_GROUPS * SSD_STATE
ATT_HEADS = 8
ATT_KV_HEADS = 2
ATT_GROUP = ATT_HEADS // ATT_KV_HEADS
ATT_WIDTH = ATT_HEADS * HEAD_DIM
KV_WIDTH = ATT_KV_HEADS * HEAD_DIM
WINDOW = 128
D_FF = 2816
FFN_CONV = 3
EPS = 1e-6
NEG = -1e30
LOG2E = 1.4426950408889634

LANES = 128
SUBLANES = 8
VMEM_LIMIT = 56 * 1024 * 1024

_Z0, _X0, _Q0, _K0, _V0, _PEND = 0, 512, 1536, 2048, 2176, 2304
ALIBI = tuple(float(2.0 ** (-8.0 * (h + 1) / ATT_HEADS)) for h in range(ATT_HEADS))


def _rms(x, w):
    ms = jnp.mean(x * x, axis=-1, keepdims=True)
    return x * lax.rsqrt(ms + EPS) * w


def _silu(x):
    return x * jax.nn.sigmoid(x)


def _softplus(x):
    return jnp.maximum(x, 0.0) + jnp.log(1.0 + jnp.exp(-jnp.abs(x)))


def _dot(a, b):
    return jnp.dot(a, b, preferred_element_type=F32)


def _dot_nt(a, b):
    return lax.dot_general(a, b, (((1,), (1,)), ((), ())), preferred_element_type=F32)


def _dot_tn(a, b):
    return lax.dot_general(a, b, (((0,), (0,)), ((), ())), preferred_element_type=F32)


def _const_spec(shape):
    nd = len(shape)
    return pl.BlockSpec(shape, lambda *_: (0,) * nd, pipeline_mode=pl.Buffered(1))


def _full_spec(shape):
    nd = len(shape)
    return pl.BlockSpec(shape, lambda *_: (0,) * nd)


_FFN_COLS = 256


def _ffn_kernel(*refs, tm, pad, shift, carry, out_proj):
    refs = list(refs)
    x_ref = refs.pop(0)
    if out_proj:
        y_ref = refs.pop(0)
        wout_ref = refs.pop(0)
    if not carry:
        prev_ref = refs.pop(0)
    n2_ref, wup_ref, cw_ref, cb_ref, wdn_ref, o_ref, tail_ref, hp_ref, act_ref = refs

    if carry:
        @pl.when(pl.program_id(1) == 0)
        def _():
            hp_ref[0:pad, :] = jnp.zeros((pad, 2 * D_FF), F32)
        x = x_ref[0]
    else:
        hp_ref[0:pad, :] = prev_ref[...]
        x = x_ref[...]

    if out_proj:
        x = x + _dot(y_ref[...].astype(BF16), wout_ref[...])
    hn = _rms(x, n2_ref[...]).astype(BF16)
    hp_ref[pad:pad + tm, :] = _dot(hn, wup_ref[...])

    for j in range(0, D_FF, _FFN_COLS):
        def conv(c0):
            cols = slice(c0, c0 + _FFN_COLS)
            acc = hp_ref[pad:pad + tm, cols] * cw_ref[2:3, cols]
            acc = acc + hp_ref[pad - shift:pad - shift + tm, cols] * cw_ref[1:2, cols]
            acc = acc + hp_ref[pad - 2 * shift:pad - 2 * shift + tm, cols] * cw_ref[0:1, cols]
            return acc + cb_ref[:, cols]
        act_ref[:, j:j + _FFN_COLS] = (_silu(conv(j)) * conv(D_FF + j)).astype(BF16)

    out = x + _dot(act_ref[...], wdn_ref[...])
    tail = hp_ref[tm + pad - 2 * shift:tm + pad, :]
    if carry:
        o_ref[0] = out
        tail_ref[0] = tail
        hp_ref[0:pad, :] = hp_ref[tm:tm + pad, :]
    else:
        o_ref[...] = out
        tail_ref[...] = tail


def _ffn_prompt(x, n2, wup, cw, cb, wdn, *, tm):
    b, s, _ = x.shape
    pad = SUBLANES
    kern = functools.partial(_ffn_kernel, tm=tm, pad=pad, shift=1, carry=True, out_proj=False)
    return pl.pallas_call(
        kern,
        grid=(b, s // tm),
        in_specs=[
            pl.BlockSpec((1, tm, D_MODEL), lambda i, t: (i, t, 0)),
            _const_spec((1, D_MODEL)),
            _const_spec((D_MODEL, 2 * D_FF)),
            _const_spec((FFN_CONV, 2 * D_FF)),
            _const_spec((1, 2 * D_FF)),
            _const_spec((D_FF, D_MODEL)),
        ],
        out_specs=[
            pl.BlockSpec((1, tm, D_MODEL), lambda i, t: (i, t, 0)),
            pl.BlockSpec((1, FFN_CONV - 1, 2 * D_FF), lambda i, t: (i, 0, 0)),
        ],
        out_shape=[
            jax.ShapeDtypeStruct((b, s, D_MODEL), F32),
            jax.ShapeDtypeStruct((b, FFN_CONV - 1, 2 * D_FF), F32),
        ],
        scratch_shapes=[
            pltpu.VMEM((tm + pad, 2 * D_FF), F32),
            pltpu.VMEM((tm, D_FF), BF16),
        ],
        compiler_params=pltpu.CompilerParams(
            dimension_semantics=("arbitrary", "arbitrary"), vmem_limit_bytes=VMEM_LIMIT),
        name="ffn_prompt",
    )(x, n2, wup, cw, cb, wdn)


def _ffn_sample(x, y, wout, prev, n2, wup, cw, cb, wdn):
    rows = x.shape[0]
    bs = prev.shape[0] // (FFN_CONV - 1)
    pad = (FFN_CONV - 1) * bs
    kern = functools.partial(_ffn_kernel, tm=rows, pad=pad, shift=bs, carry=False, out_proj=True)
    return pl.pallas_call(
        kern,
        grid=(1,),
        in_specs=[
            _const_spec((rows, D_MODEL)),
            _const_spec((rows, D_MODEL)),
            _const_spec((D_MODEL, D_MODEL)),
            _const_spec((pad, 2 * D_FF)),
            _const_spec((1, D_MODEL)),
            _const_spec((D_MODEL, 2 * D_FF)),
            _const_spec((FFN_CONV, 2 * D_FF)),
            _const_spec((1, 2 * D_FF)),
            _const_spec((D_FF, D_MODEL)),
        ],
        out_specs=[
            _full_spec((rows, D_MODEL)),
            _full_spec((pad, 2 * D_FF)),
        ],
        out_shape=[
            jax.ShapeDtypeStruct((rows, D_MODEL), F32),
            jax.ShapeDtypeStruct((pad, 2 * D_FF), F32),
        ],
        scratch_shapes=[
            pltpu.VMEM((rows + pad, 2 * D_FF), F32),
            pltpu.VMEM((rows, D_FF), BF16),
        ],
        compiler_params=pltpu.CompilerParams(
            dimension_semantics=("arbitrary",), vmem_limit_bytes=VMEM_LIMIT),
        name="ffn_sample",
    )(x, y, wout, prev, n2, wup, cw, cb, wdn)


def _tri_cumsum(x, ltri):
    hi = x.astype(BF16)
    lo = (x - hi.astype(F32)).astype(BF16)
    return _dot(ltri, hi) + _dot(ltri, lo)


def _half_variants(a, lane_lo):
    ar = pltpu.roll(a, HEAD_DIM, axis=1)
    zero = jnp.zeros_like(a)
    return {
        (0, 0): jnp.where(lane_lo, a, zero).astype(BF16),
        (1, 1): jnp.where(lane_lo, zero, a).astype(BF16),
        (1, 0): jnp.where(lane_lo, ar, zero).astype(BF16),
        (0, 1): jnp.where(lane_lo, zero, ar).astype(BF16),
    }


def _conv_silu(cbuf_ref, cw_ref, cb_ref, r0, cols):
    pad, c = SUBLANES, SSD_CHUNK
    acc = cb_ref[:, cols] + cbuf_ref[pad + r0:pad + r0 + c, cols] * cw_ref[SSD_CONV - 1:SSD_CONV, cols]
    for i in range(1, SSD_CONV):
        acc = acc + cbuf_ref[pad + r0 - i:pad + r0 - i + c, cols] * cw_ref[SSD_CONV - 1 - i:SSD_CONV - i, cols]
    return _silu(acc)


_CHUNK_PIECES = 11


def _chunk_pieces(r0, bufs, kprev, vprev, first_mask, bias_ref, st_ref, sink_ref, cw_ref, cb_ref, a, dskip, snorm,
                  qnw, knw, ltri, causal, lane_lo):
    proj_ref, cbuf_ref, dt_ref, y_ref = bufs
    c = SSD_CHUNK
    rows = slice(r0, r0 + c)
    gw = SSD_INNER // SSD_GROUPS
    qo, ko, vo = _Q0 - SSD_CONV_DIM, _K0 - SSD_CONV_DIM, _V0 - SSD_CONV_DIM

    dt = dt_ref[rows, :]
    acum = _tri_cumsum(dt * a, ltri)
    bc = _conv_silu(cbuf_ref, cw_ref, cb_ref, r0, slice(SSD_INNER, SSD_CONV_DIM))
    bms = [bc[:, g * SSD_STATE:(g + 1) * SSD_STATE].astype(BF16) for g in range(SSD_GROUPS)]
    cms = [bc[:, (SSD_GROUPS + g) * SSD_STATE:(SSD_GROUPS + g + 1) * SSD_STATE].astype(BF16)
           for g in range(SSD_GROUPS)]
    cbs = [_dot_nt(cms[g], bms[g]) for g in range(SSD_GROUPS)]
    yoffs = [_dot(cms[g], st_ref[:, g * gw:(g + 1) * gw].astype(BF16)) for g in range(SSD_GROUPS)]

    q = proj_ref[rows, qo:qo + ATT_WIDTH]
    kn = _norm_k(proj_ref[rows, ko:ko + KV_WIDTH], knw, lane_lo)
    v = proj_ref[rows, vo:vo + KV_WIDTH]
    kcur = _half_variants(kn, lane_lo)
    vcur = _half_variants(v, lane_lo)
    qn = []
    for m in range(ATT_HEADS // 2):
        qp = q[:, m * LANES:(m + 1) * LANES]
        q2p = qp * qp
        rs_lo = lax.rsqrt(jnp.sum(jnp.where(lane_lo, q2p, 0.0), axis=-1, keepdims=True) / HEAD_DIM + EPS)
        rs_hi = lax.rsqrt(jnp.sum(jnp.where(lane_lo, 0.0, q2p), axis=-1, keepdims=True) / HEAD_DIM + EPS)
        qn.append((qp * qnw[:, m * LANES:(m + 1) * LANES]
                   * (jnp.where(lane_lo, rs_lo, rs_hi) * (HEAD_DIM ** -0.5 * LOG2E))).astype(BF16))
    scores = {}
    for kv in range(ATT_KV_HEADS):
        qs = jnp.concatenate([qn[2 * kv], qn[2 * kv + 1]], axis=0)
        for e in range(2):
            sr = _dot_nt(qs, jnp.concatenate([kprev[(kv, e)], kcur[(kv, e)]], axis=0))
            scores[ATT_GROUP * kv + e] = sr[0:c]
            scores[ATT_GROUP * kv + 2 + e] = sr[c:2 * c]
    vall = {key: jnp.concatenate([vprev[key], vcur[key]], axis=0) for key in vcur}
    yield

    xs = _conv_silu(cbuf_ref, cw_ref, cb_ref, r0, slice(0, SSD_INNER))
    alast = acum[c - 1:c, :]
    xw = (xs * (jnp.exp(alast - acum) * dt)).astype(BF16)
    cdec = jnp.exp(alast)
    for g in range(SSD_GROUPS):
        st_ref[:, g * gw:(g + 1) * gw] = (st_ref[:, g * gw:(g + 1) * gw] * cdec[:, g * gw:(g + 1) * gw]
                                          + _dot_tn(bms[g], xw[:, g * gw:(g + 1) * gw]))
    yield
    eac = jnp.exp(acum)
    xdt = xs * dt
    acum_t = acum.T
    ys = []
    for pm in range(SSD_INNER // LANES):
        g, m = divmod(pm, gw // LANES)
        lanes = slice(pm * LANES, (pm + 1) * LANES)
        a_pair = acum[:, lanes]
        a_roll = pltpu.roll(a_pair, HEAD_DIM, axis=1)
        ms = []
        for e in range(2):
            h = 2 * pm + e
            colb = jnp.where(lane_lo, a_pair, a_roll) if e == 0 else jnp.where(lane_lo, a_roll, a_pair)
            rowb = jnp.broadcast_to(acum_t[h * HEAD_DIM:h * HEAD_DIM + 1, :], (c, c))
            dec = jnp.exp(jnp.where(causal, colb - rowb, NEG))
            ms.append((cbs[g] * dec).astype(BF16))
        xd = xdt[:, lanes]
        rhs = jnp.concatenate([jnp.where(lane_lo, xd, 0.0), jnp.where(lane_lo, 0.0, xd)], axis=0).astype(BF16)
        yield
        ydiag = _dot(jnp.concatenate(ms, axis=1), rhs)
        ys.append(ydiag + yoffs[g][:, m * LANES:(m + 1) * LANES] * eac[:, lanes])

    outs = []
    for m in range(ATT_HEADS // 2):
        kv = (2 * m) // ATT_GROUP
        ps, invs = [], []
        for e in range(2):
            h = 2 * m + e
            s = scores[h] + bias_ref[h]
            if first_mask is not None:
                s = jnp.where(first_mask, s, NEG)
            sk = sink_ref[h] * LOG2E
            mx = jnp.maximum(jnp.max(s, axis=-1, keepdims=True), sk)
            p = jnp.exp2(s - mx)
            den = jnp.sum(p, axis=-1, keepdims=True) + jnp.exp2(sk - mx)
            ps.append(p.astype(BF16))
            invs.append(1.0 / den)
        yield
        o = _dot(jnp.concatenate(ps, axis=1), jnp.concatenate([vall[(kv, 0)], vall[(kv, 1)]], axis=0))
        outs.append(o * jnp.where(lane_lo, invs[0], invs[1]))

    y = (jnp.concatenate(ys, axis=1) + xs * dskip) * _silu(proj_ref[rows, _Z0:_Z0 + SSD_INNER])
    y_ref[rows, 0:SSD_INNER] = jnp.concatenate(
        [_rms(y[:, g * gw:(g + 1) * gw], snorm[:, g * gw:(g + 1) * gw]) for g in range(SSD_GROUPS)],
        axis=1).astype(BF16)
    y_ref[rows, SSD_INNER:] = jnp.concatenate(outs, axis=1).astype(BF16)
    yield
    return kn, v, kcur, vcur


def _norm_k(k, knw, lane_lo):
    k2 = k * k
    r0 = lax.rsqrt(jnp.sum(jnp.where(lane_lo, k2, 0.0), axis=-1, keepdims=True) / HEAD_DIM + EPS)
    r1 = lax.rsqrt(jnp.sum(jnp.where(lane_lo, 0.0, k2), axis=-1, keepdims=True) / HEAD_DIM + EPS)
    return k * jnp.where(lane_lo, r0, r1) * knw


def _weave(main, n_main, side):
    done = 0
    for k in range(n_main + 1):
        try:
            next(main)
        except StopIteration as stop:
            for thunk in side[done:]:
                thunk()
            return stop.value
        while done < len(side) and done * n_main < (k + 1) * len(side):
            side[done]()
            done += 1
    raise AssertionError("main generator has more pieces than declared")


_PROJ_COLS = 256


def _project_pieces(load_x, n1_ref, win_ref, wdt_ref, dtb_ref, bufs, tm):
    proj_ref, cbuf_ref, dt_ref, _ = bufs
    pad = SUBLANES
    box = {}

    def norm():
        box['hn'] = _rms(load_x(), n1_ref[...]).astype(BF16)

    def slab(c0, c1):
        def run():
            res = _dot(box['hn'], win_ref[:, c0:c1])
            if c0 < _X0:
                proj_ref[:, c0:c1] = res
            elif c0 < _Q0:
                cbuf_ref[pad:pad + tm, c0 - _X0:c1 - _X0] = res
            else:
                proj_ref[:, c0 - SSD_CONV_DIM:c1 - SSD_CONV_DIM] = res
        return run

    def step_sizes(c0, c1):
        def run():
            dt_ref[:, c0:c1] = _softplus(_dot(box['hn'], wdt_ref[:, c0:c1]) + dtb_ref[:, c0:c1])
        return run

    slabs = []
    for seg0, seg1 in ((_Z0, _X0), (_X0, _Q0), (_Q0, _PEND)):
        for c0 in range(seg0, seg1, _PROJ_COLS):
            slabs.append(slab(c0, min(c0 + _PROJ_COLS, seg1)))
    return [norm] + slabs + [step_sizes(c0, c0 + _PROJ_COLS) for c0 in range(0, SSD_INNER, _PROJ_COLS)]


def _outproj_pieces(load_x, y_ref, wout_ref, store):
    def slab(c0, c1):
        def run():
            store(c0, c1, load_x(c0, c1) + _dot(y_ref[...], wout_ref[:, c0:c1]))
        return run
    return [slab(c0, c0 + 2 * LANES) for c0 in range(0, D_MODEL, 2 * LANES)]


def _score_bias(bias_ref):
    c = SSD_CHUNK
    qi = lax.broadcasted_iota(jnp.int32, (c, 2 * c), 0)
    sj = lax.broadcasted_iota(jnp.int32, (c, 2 * c), 1)
    reli = qi + c - sj
    rel = reli.astype(F32)
    in_window = (reli >= 0) & (reli < WINDOW)
    for h in range(ATT_HEADS):
        bias_ref[h] = jnp.where(in_window, (-ALIBI[h] * LOG2E) * rel, NEG)


def _mix_tile(bufs, first_col, kprev, vprev, st_ref, bias_ref, sink_ref, cw_ref, cb_ref, alog_ref, dskip_ref,
              snorm_ref, qnw_ref, knw_ref, tm):
    c = SSD_CHUNK
    row = lax.broadcasted_iota(jnp.int32, (c, c), 0)
    col = lax.broadcasted_iota(jnp.int32, (c, c), 1)
    causal = row >= col
    ltri = jnp.where(causal, 1.0, 0.0).astype(BF16)
    lane_lo = col < HEAD_DIM
    a = -jnp.exp(alog_ref[...])

    kn = v = None
    for ci in range(tm // c):
        first_mask = None
        if ci == 0 and first_col is not None:
            first_mask = lax.broadcasted_iota(jnp.int32, (c, 2 * c), 1) >= first_col
        kn, v, kprev, vprev = yield from _chunk_pieces(
            ci * c, bufs, kprev, vprev, first_mask, bias_ref, st_ref, sink_ref, cw_ref, cb_ref, a, dskip_ref[...],
            snorm_ref[...], qnw_ref[...], knw_ref[...], ltri, causal, lane_lo)
    return kn, v, kprev, vprev


def _mixer_prompt_kernel(sink_ref, xcur_ref, xnext_ref, n1_ref, win_ref, wdt_ref, cw_ref, cb_ref, dtb_ref, alog_ref,
                         dskip_ref, snorm_ref, qnw_ref, knw_ref, wout_ref,
                         o_ref, ssm_ref, conv_ref, kc_ref, vc_ref,
                         proj_a, cbuf_a, dt_a, y_a, proj_b, cbuf_b, dt_b, y_b, st_ref, kprev_ref, vprev_ref, bias_ref,
                         *, tm):
    i = pl.program_id(0)
    s = pl.program_id(1)
    ns = pl.num_programs(1)
    pad = SUBLANES
    bufs_a = (proj_a, cbuf_a, dt_a, y_a)
    bufs_b = (proj_b, cbuf_b, dt_b, y_b)
    proj_args = (n1_ref, win_ref, wdt_ref, dtb_ref)
    mix_args = (st_ref, bias_ref, sink_ref, cw_ref, cb_ref, alog_ref, dskip_ref, snorm_ref, qnw_ref, knw_ref, tm)

    @pl.when((i == 0) & (s == 0))
    def _():
        _score_bias(bias_ref)
        for piece in _project_pieces(lambda: xcur_ref[0, 0:tm, :], *proj_args, bufs_a, tm):
            piece()

    @pl.when(s == 0)
    def _():
        st_ref[...] = jnp.zeros(st_ref.shape, F32)
        cbuf_a[0:pad, :] = jnp.zeros((pad, SSD_CONV_DIM), F32)
        kprev_ref[...] = jnp.zeros(kprev_ref.shape, F32)
        vprev_ref[...] = jnp.zeros(vprev_ref.shape, F32)

    @pl.when(s > 0)
    def _():
        cbuf_a[0:pad, :] = cbuf_b[tm:tm + pad, :]

    lane_lo = lax.broadcasted_iota(jnp.int32, (SSD_CHUNK, LANES), 1) < HEAD_DIM
    first_col = jnp.where(s == 0, SSD_CHUNK, 0)

    n_mix = _CHUNK_PIECES * (tm // SSD_CHUNK)

    def outproj(r0, y_ref):
        def store(c0, c1, val):
            o_ref[0, r0:r0 + tm, c0:c1] = val
        return _outproj_pieces(lambda c0, c1: xcur_ref[0, r0:r0 + tm, c0:c1], y_ref, wout_ref, store)

    side = _project_pieces(lambda: xcur_ref[0, tm:2 * tm, :], *proj_args, bufs_b, tm)
    _, _, kprev, vprev = _weave(
        _mix_tile(bufs_a, first_col, _half_variants(kprev_ref[...], lane_lo),
                  _half_variants(vprev_ref[...], lane_lo), *mix_args), n_mix, side)
    cbuf_b[0:pad, :] = cbuf_a[tm:tm + pad, :]
    side = outproj(0, y_a) + _project_pieces(lambda: xnext_ref[0], *proj_args, bufs_a, tm)
    kn, v, _, _ = _weave(_mix_tile(bufs_b, None, kprev, vprev, *mix_args), n_mix, side)
    for piece in outproj(tm, y_b):
        piece()
    kprev_ref[...] = kn
    vprev_ref[...] = v

    @pl.when(s == ns - 1)
    def _():
        ssm_ref[0] = st_ref[...].T
        conv_ref[0] = cbuf_b[tm + pad - (SSD_CONV - 1):tm + pad, :]
        kc_ref[0] = kn
        vc_ref[0] = v


def _mixer_prompt(x, sinks, n1, win, wdt, cw, cb, dtb, alog, dskip, snorm, qnw, knw, wout, *, tm):
    b, s, _ = x.shape
    pw = _PEND - SSD_CONV_DIM
    nt = s // tm
    ns = nt // 2
    kern = functools.partial(_mixer_prompt_kernel, tm=tm)
    consts = [n1, win, wdt, cw, cb, dtb, alog, dskip, snorm, qnw, knw, wout]

    def next_tile(i, t, *_):
        flat = jnp.minimum((i * ns + t) * 2 + 2, b * nt - 1)
        return (flat // nt, flat % nt, 0)

    bufset = [
        pltpu.VMEM((tm, pw), F32),
        pltpu.VMEM((tm + SUBLANES, SSD_CONV_DIM), F32),
        pltpu.VMEM((tm, SSD_INNER), F32),
        pltpu.VMEM((tm, D_MODEL), BF16),
    ]
    return pl.pallas_call(
        kern,
        grid_spec=pltpu.PrefetchScalarGridSpec(
            num_scalar_prefetch=1,
            grid=(b, ns),
            in_specs=[pl.BlockSpec((1, 2 * tm, D_MODEL), lambda i, t, *_: (i, t, 0)),
                      pl.BlockSpec((1, tm, D_MODEL), next_tile)]
            + [_const_spec(w.shape) for w in consts],
            out_specs=[
                pl.BlockSpec((1, 2 * tm, D_MODEL), lambda i, t, *_: (i, t, 0)),
                pl.BlockSpec((1, SSD_INNER, SSD_STATE), lambda i, t, *_: (i, 0, 0)),
                pl.BlockSpec((1, SSD_CONV - 1, SSD_CONV_DIM), lambda i, t, *_: (i, 0, 0)),
                pl.BlockSpec((1, WINDOW, KV_WIDTH), lambda i, t, *_: (i, 0, 0)),
                pl.BlockSpec((1, WINDOW, KV_WIDTH), lambda i, t, *_: (i, 0, 0)),
            ],
            scratch_shapes=bufset + bufset + [
                pltpu.VMEM((SSD_STATE, SSD_INNER), F32),
                pltpu.VMEM((WINDOW, KV_WIDTH), F32),
                pltpu.VMEM((WINDOW, KV_WIDTH), F32),
                pltpu.VMEM((ATT_HEADS, SSD_CHUNK, 2 * SSD_CHUNK), F32),
            ],
        ),
        out_shape=[
            jax.ShapeDtypeStruct((b, s, D_MODEL), F32),
            jax.ShapeDtypeStruct((b, SSD_INNER, SSD_STATE), F32),
            jax.ShapeDtypeStruct((b, SSD_CONV - 1, SSD_CONV_DIM), F32),
            jax.ShapeDtypeStruct((b, WINDOW, KV_WIDTH), F32),
            jax.ShapeDtypeStruct((b, WINDOW, KV_WIDTH), F32),
        ],
        compiler_params=pltpu.CompilerParams(
            dimension_semantics=("arbitrary", "arbitrary"), vmem_limit_bytes=VMEM_LIMIT),
        name="mixer_prompt",
    )(sinks, x, x, *consts)


_S_Z0, _S_X0, _S_DT0, _S_Q0, _S_K0, _S_V0, _S_END = 0, 512, 1536, 2048, 2560, 2688, 2816


def _sample_in_kernel(x_ref, prev_ref, n1_ref, win_ref, wdt_ref, cw_ref, cb_ref, dtb_ref, o_ref, tail_ref, cbuf_ref,
                      *, rows, bs):
    pad = (SSD_CONV - 1) * bs
    hn = _rms(x_ref[...], n1_ref[...]).astype(BF16)
    o_ref[:, _S_Z0:_S_X0] = _dot(hn, win_ref[:, _Z0:_X0])
    cbuf_ref[0:pad, :] = prev_ref[...]
    cbuf_ref[pad:pad + rows, :] = _dot(hn, win_ref[:, _X0:_Q0])
    acc = cb_ref[...] + cbuf_ref[pad:pad + rows, :] * cw_ref[SSD_CONV - 1:SSD_CONV, :]
    for i in range(1, SSD_CONV):
        acc = acc + cbuf_ref[pad - i * bs:pad - i * bs + rows, :] * cw_ref[SSD_CONV - 1 - i:SSD_CONV - i, :]
    o_ref[:, _S_X0:_S_DT0] = _silu(acc)
    o_ref[:, _S_DT0:_S_Q0] = _softplus(_dot(hn, wdt_ref[...]) + dtb_ref[...])
    o_ref[:, _S_Q0:_S_END] = _dot(hn, win_ref[:, _Q0:_PEND])
    tail_ref[...] = cbuf_ref[rows:rows + pad, :]


def _sample_in(x, prev, n1, win, wdt, cw, cb, dtb):
    rows = x.shape[0]
    pad = prev.shape[0]
    bs = pad // (SSD_CONV - 1)
    args = (x, prev, n1, win, wdt, cw, cb, dtb)
    return pl.pallas_call(
        functools.partial(_sample_in_kernel, rows=rows, bs=bs),
        grid=(1,),
        in_specs=[_const_spec(a.shape) for a in args],
        out_specs=[_full_spec((rows, _S_END)), _full_spec((pad, SSD_CONV_DIM))],
        out_shape=[jax.ShapeDtypeStruct((rows, _S_END), F32), jax.ShapeDtypeStruct((pad, SSD_CONV_DIM), F32)],
        scratch_shapes=[pltpu.VMEM((rows + pad, SSD_CONV_DIM), F32)],
        compiler_params=pltpu.CompilerParams(dimension_semantics=("arbitrary",), vmem_limit_bytes=VMEM_LIMIT),
        name="sample_in",
    )(*args)


def _sample_mix_kernel(sink_ref, p_ref, st_ref, kc_ref, vc_ref, alog_ref, dskip_ref, snorm_ref, qnw_ref, knw_ref,
                       *rest, tlen, layer, all_layers):
    y_ref, sto_ref, ko_ref, vo_ref = rest[-4:]
    if all_layers:
        for ref in (sto_ref, ko_ref, vo_ref):
            for other in range(ref.shape[0]):
                if other != layer:
                    ref[other] = jnp.zeros(ref.shape[1:], F32)
        sto_ref, ko_ref, vo_ref = sto_ref.at[layer], ko_ref.at[layer], vo_ref.at[layer]
    nseq = SUBLANES // tlen
    w = WINDOW
    pk = p_ref[...]
    z = pk[:, _S_Z0:_S_X0]
    xs = pk[:, _S_X0:_S_X0 + SSD_INNER]
    dt = pk[:, _S_DT0:_S_Q0]
    q = pk[:, _S_Q0:_S_K0]
    k = pk[:, _S_K0:_S_V0]
    v = pk[:, _S_V0:_S_END]
    gw = SSD_INNER // SSD_GROUPS
    bms = [pk[:, _S_X0 + SSD_INNER + g * SSD_STATE:_S_X0 + SSD_INNER + (g + 1) * SSD_STATE]
           for g in range(SSD_GROUPS)]
    c0 = _S_X0 + SSD_INNER + SSD_GROUPS * SSD_STATE
    cms = [pk[:, c0 + g * SSD_STATE:c0 + (g + 1) * SSD_STATE] for g in range(SSD_GROUPS)]

    rowi = lax.broadcasted_iota(jnp.int32, (SUBLANES, SSD_INNER), 0)
    assert tlen & (tlen - 1) == 0 and SUBLANES % tlen == 0
    tshift = tlen.bit_length() - 1
    sshift = SUBLANES.bit_length() - 1
    tpos = rowi & (tlen - 1)
    seq = rowi >> tshift
    seq_g = lax.broadcasted_iota(jnp.int32, (SUBLANES, gw), 0) >> tshift
    lane_lo = lax.broadcasted_iota(jnp.int32, (SUBLANES, LANES), 1) < HEAD_DIM

    dta = dt * (-jnp.exp(alog_ref[...]))
    acum = dta
    for s in range(1, tlen):
        acum = acum + jnp.where(tpos >= s, pltpu.roll(dta, s, axis=0), 0.0)
    tot = acum[tlen - 1:tlen, :]
    for b in range(1, nseq):
        tot = jnp.where(seq == b, acum[(b + 1) * tlen - 1:(b + 1) * tlen, :], tot)
    eac = jnp.exp(acum)
    cdec = jnp.exp(tot)
    xdt = xs * dt
    xw = xs * (jnp.exp(tot - acum) * dt)

    ydiag = jnp.zeros((SUBLANES, SSD_INNER), F32)
    for s in range(tlen):
        xsh = pltpu.roll(xdt, s, axis=0) if s else xdt
        ash = pltpu.roll(acum, s, axis=0) if s else acum
        cbs = []
        for g in range(SSD_GROUPS):
            bsh = pltpu.roll(bms[g], s, axis=0) if s else bms[g]
            cbs.append(jnp.broadcast_to(jnp.sum(cms[g] * bsh, axis=-1, keepdims=True), (SUBLANES, gw)))
        term = jnp.concatenate(cbs, axis=1) * jnp.exp(acum - ash) * xsh
        ydiag = ydiag + jnp.where(tpos >= s, term, 0.0)

    yoff = []
    for g in range(SSD_GROUPS):
        cm = cms[g].astype(BF16)
        bm = bms[g].astype(BF16)
        yo = None
        for b in range(nseq):
            h0 = st_ref[b, g * gw:(g + 1) * gw, :]
            yb = _dot_nt(cm, h0.astype(BF16))
            yo = yb if yo is None else jnp.where(seq_g == b, yb, yo)
            xwb = jnp.where(seq_g == b, xw[:, g * gw:(g + 1) * gw], 0.0).astype(BF16)
            cseq = jnp.broadcast_to(cdec[b * tlen:b * tlen + 1, :], (SUBLANES, SSD_INNER))
            blocks = []
            for hh in range(gw // HEAD_DIM):
                pm = (g * gw + hh * HEAD_DIM) // LANES
                cpair = cseq[:, pm * LANES:(pm + 1) * LANES]
                croll = pltpu.roll(cpair, HEAD_DIM, axis=1)
                chead = jnp.where(lane_lo, cpair, croll) if hh % 2 == 0 else jnp.where(lane_lo, croll, cpair)
                blocks += [chead] * (HEAD_DIM // SUBLANES)
            sto_ref[b, g * gw:(g + 1) * gw, :] = h0 * jnp.concatenate(blocks, axis=0) + _dot_tn(xwb, bm)
        yoff.append(yo)
    y = (ydiag + jnp.concatenate(yoff, axis=1) * eac + xs * dskip_ref[...]) * _silu(z)
    y_ssd = jnp.concatenate(
        [_rms(y[:, g * gw:(g + 1) * gw], snorm_ref[:, g * gw:(g + 1) * gw]) for g in range(SSD_GROUPS)], axis=1)

    kn = _norm_k(k, knw_ref[...], lane_lo)
    for b in range(nseq):
        ko_ref[b, 0:w - tlen, :] = kc_ref[b, tlen:w, :]
        vo_ref[b, 0:w - tlen, :] = vc_ref[b, tlen:w, :]
        ko_ref[b, w - tlen:w, :] = kn[b * tlen:(b + 1) * tlen, :]
        vo_ref[b, w - tlen:w, :] = v[b * tlen:(b + 1) * tlen, :]
    qw = q * qnw_ref[...]
    q2 = q * q
    pieces = []
    for h in range(ATT_HEADS):
        kv = h // ATT_GROUP
        lanes = slice((h // 2) * LANES, (h // 2 + 1) * LANES)
        own = lane_lo if h % 2 == 0 else jnp.logical_not(lane_lo)
        ssq = jnp.sum(jnp.where(own, q2[:, lanes], 0.0), axis=-1, keepdims=True)
        qh = jnp.where(own, qw[:, lanes], 0.0) * (lax.rsqrt(ssq / HEAD_DIM + EPS) * (HEAD_DIM ** -0.5))
        pieces.append(qh if h % 2 == kv else pltpu.roll(qh, HEAD_DIM, axis=1))
    qrows = jnp.concatenate(pieces, axis=0).astype(BF16)

    nq = ATT_HEADS * SUBLANES
    ncol = w + SUBLANES
    ri = lax.broadcasted_iota(jnp.int32, (nq, ncol), 0)
    cj = lax.broadcasted_iota(jnp.int32, (nq, ncol), 1)
    rt = ri & (tlen - 1)
    rseq = (ri & (SUBLANES - 1)) >> tshift
    rhead = ri >> sshift
    nj = jnp.maximum(cj - w, 0)
    in_cache = cj < w
    rel = jnp.where(in_cache, rt + w - cj, rt - (nj & (tlen - 1))).astype(F32)
    ok_new = ((nj >> tshift) == rseq) & ((nj & (tlen - 1)) <= rt)
    ok = (in_cache & (cj > rt)) | (jnp.logical_not(in_cache) & ok_new)
    slope = jnp.zeros((nq, ncol), F32)
    for h in range(ATT_HEADS):
        slope = jnp.where(rhead == h, ALIBI[h], slope)
    rh1 = lax.broadcasted_iota(jnp.int32, (nq, 1), 0) >> sshift
    sk = jnp.zeros((nq, 1), F32)
    for h in range(ATT_HEADS):
        sk = jnp.where(rh1 == h, sink_ref[h], sk)
    bias = slope * rel
    rseq1 = (lax.broadcasted_iota(jnp.int32, (nq, LANES), 0) & (SUBLANES - 1)) >> tshift

    out = jnp.zeros((nq, LANES), F32)
    for b in range(nseq):
        kc = jnp.concatenate([kc_ref[b], kn], axis=0).astype(BF16)
        vc = jnp.concatenate([vc_ref[b], v], axis=0).astype(BF16)
        s = jnp.where(ok, _dot_nt(qrows, kc) - bias, NEG)
        mx = jnp.maximum(jnp.max(s, axis=-1, keepdims=True), sk)
        p = jnp.exp(s - mx)
        den = jnp.sum(p, axis=-1, keepdims=True) + jnp.exp(sk - mx)
        o = _dot(p.astype(BF16), vc) / den
        out = jnp.where(rseq1 == b, o, out)
    pairs = []
    for m in range(ATT_HEADS // 2):
        halves = []
        for e in range(2):
            h = 2 * m + e
            blk = out[h * SUBLANES:(h + 1) * SUBLANES, :]
            halves.append(blk if h // ATT_GROUP == e else pltpu.roll(blk, HEAD_DIM, axis=1))
        pairs.append(jnp.where(lane_lo, halves[0], halves[1]))
    y_ref[...] = jnp.concatenate([y_ssd] + pairs, axis=1)


def _sample_mix(layer, packed, sinks, st, kc, vc, alog, dskip, snorm, qnw, knw, prev_outs, *, tlen):
    rows = packed.shape[0]
    nseq = SUBLANES // tlen
    consts = [alog, dskip, snorm, qnw, knw]
    carried = [] if prev_outs is None else list(prev_outs)
    n_in = 1 + 4 + len(consts)

    def slab(shape):
        return pl.BlockSpec((None, nseq) + shape, lambda i, *_: (layer, i, 0, 0))

    def out_slab(shape):
        if prev_outs is None:
            return pl.BlockSpec((st.shape[0], nseq) + shape, lambda i, *_: (0, i, 0, 0))
        return slab(shape)

    return pl.pallas_call(
        functools.partial(_sample_mix_kernel, tlen=tlen, layer=layer, all_layers=prev_outs is None),
        grid_spec=pltpu.PrefetchScalarGridSpec(
            num_scalar_prefetch=1,
            grid=(rows // SUBLANES,),
            in_specs=[
                pl.BlockSpec((SUBLANES, _S_END), lambda i, *_: (i, 0)),
                slab((SSD_INNER, SSD_STATE)), slab((WINDOW, KV_WIDTH)), slab((WINDOW, KV_WIDTH)),
            ] + [_const_spec(a.shape) for a in consts] + [pl.BlockSpec(memory_space=pl.ANY)] * len(carried),
            out_specs=[
                pl.BlockSpec((SUBLANES, D_MODEL), lambda i, *_: (i, 0)),
                out_slab((SSD_INNER, SSD_STATE)), out_slab((WINDOW, KV_WIDTH)), out_slab((WINDOW, KV_WIDTH)),
            ],
        ),
        out_shape=[
            jax.ShapeDtypeStruct((rows, D_MODEL), F32),
            jax.ShapeDtypeStruct(st.shape, F32),
            jax.ShapeDtypeStruct(kc.shape, F32),
            jax.ShapeDtypeStruct(vc.shape, F32),
        ],
        input_output_aliases={n_in + j: 1 + j for j in range(len(carried))},
        compiler_params=pltpu.CompilerParams(dimension_semantics=("arbitrary",), vmem_limit_bytes=VMEM_LIMIT),
        name="sample_mix",
    )(sinks, packed, st, kc, vc, *consts, *carried)


def _prep_layer(p, i):
    w_in = p['w_in'][i]
    s0, s1, s2, s3, s4 = (SSD_INNER, SSD_INNER + SSD_CONV_DIM, SSD_INNER + SSD_CONV_DIM + SSD_HEADS,
                          SSD_INNER + SSD_CONV_DIM + SSD_HEADS + ATT_WIDTH,
                          SSD_INNER + SSD_CONV_DIM + SSD_HEADS + ATT_WIDTH + KV_WIDTH)
    win = jnp.concatenate([w_in[:, :s1], w_in[:, s2:]], axis=1).astype(BF16)
    wdt = jnp.repeat(w_in[:, s1:s2], HEAD_DIM, axis=1).astype(BF16)

    def per_head(v):
        return jnp.repeat(v, HEAD_DIM)[None, :]

    mixer = (
        p['attn_sinks'][i],
        p['norm1_w'][i][None, :], win, wdt, p['ssd_conv_w'][i], p['ssd_conv_b'][i][None, :],
        per_head(p['dt_bias'][i]), per_head(p['a_log'][i]), per_head(p['d_skip'][i]), p['ssd_norm_w'][i][None, :],
        jnp.tile(p['q_norm_w'][i], ATT_HEADS)[None, :], jnp.tile(p['k_norm_w'][i], ATT_KV_HEADS)[None, :],
        p['w_out'][i].astype(BF16),
    )
    ffn = (
        p['norm2_w'][i][None, :], p['w_up'][i].astype(BF16), p['ffn_conv_w'][i], p['ffn_conv_b'][i][None, :],
        p['w_down'][i].astype(BF16),
    )
    return {'mixer': mixer, 'ffn': ffn}


_TM_MIXER = 256
_TM_FFN = 512


def _to_time_major(a):
    bs, t, c = a.shape
    return a.transpose(1, 0, 2).reshape(t * bs, c)


def _to_seq_major(a, bs):
    return a.reshape(a.shape[0] // bs, bs, a.shape[1]).transpose(1, 0, 2)


def kernel(x_prompt, x_sample, state_ssm, state_ssd_conv, cache_swa_k, cache_swa_v, state_ffn_conv, norm1_w, w_in,
           ssd_conv_w, ssd_conv_b, dt_bias, a_log, d_skip, ssd_norm_w, q_norm_w, k_norm_w, attn_sinks, w_out, norm2_w,
           w_up, ffn_conv_w, ffn_conv_b, w_down):
    p = dict(norm1_w=norm1_w, w_in=w_in, ssd_conv_w=ssd_conv_w, ssd_conv_b=ssd_conv_b, dt_bias=dt_bias, a_log=a_log,
             d_skip=d_skip, ssd_norm_w=ssd_norm_w, q_norm_w=q_norm_w, k_norm_w=k_norm_w, attn_sinks=attn_sinks,
             w_out=w_out, norm2_w=norm2_w, w_up=w_up, ffn_conv_w=ffn_conv_w, ffn_conv_b=ffn_conv_b, w_down=w_down)
    depth = w_in.shape[0]
    b = x_prompt.shape[0]
    bs, tlen, _ = x_sample.shape
    assert cache_swa_k.shape[2] == WINDOW
    head_shape = (SSD_HEADS, HEAD_DIM, SSD_STATE)
    kv_shape = (ATT_KV_HEADS, HEAD_DIM)

    xp = x_prompt
    xs = _to_time_major(x_sample)
    ssm_all = state_ssm.reshape(depth, bs, SSD_INNER, SSD_STATE)
    kc_all = cache_swa_k.reshape(depth, bs, WINDOW, KV_WIDTH)
    vc_all = cache_swa_v.reshape(depth, bs, WINDOW, KV_WIDTH)
    s_outs = None
    p_states, s_states = [], []
    for i in range(depth):
        w = _prep_layer(p, i)
        sinks, n1, win, wdt, cw, cb, dtb, alog, dskip, snorm, qnw, knw, wout = w['mixer']

        x1, ssm, conv, kc, vc = _mixer_prompt(xp, *w['mixer'], tm=_TM_MIXER)
        xp, ffn_tail = _ffn_prompt(x1, *w['ffn'], tm=_TM_FFN)
        p_states.append((ssm.reshape((b,) + head_shape), conv, kc.reshape((b, WINDOW) + kv_shape),
                         vc.reshape((b, WINDOW) + kv_shape), ffn_tail))

        packed, conv_tail = _sample_in(xs, _to_time_major(state_ssd_conv[i]), n1, win, wdt, cw, cb, dtb)
        packed = _to_seq_major(packed, bs).reshape(bs * tlen, -1)
        y, *s_outs = _sample_mix(i, packed, sinks, ssm_all, kc_all, vc_all, alog, dskip, snorm, qnw, knw, s_outs,
                                 tlen=tlen)
        y = _to_time_major(y.reshape(bs, tlen, D_MODEL))
        xs, ffn_tail_s = _ffn_sample(xs, y, wout, _to_time_major(state_ffn_conv[i]), *w['ffn'])
        s_states.append((_to_seq_major(conv_tail, bs), _to_seq_major(ffn_tail_s, bs)))

    def stacked(states, j):
        return jnp.stack([st[j] for st in states])

    ssm_s, k_s, v_s = s_outs
    return (xp, _to_seq_major(xs, bs)) + tuple(stacked(p_states, j) for j in range(5)) + (
        ssm_s.reshape((depth, bs) + head_shape), stacked(s_states, 0),
        k_s.reshape((depth, bs, WINDOW) + kv_shape), v_s.reshape((depth, bs, WINDOW) + kv_shape),
        stacked(s_states, 1))
```

```python
import functools

import numpy as np
import jax
import jax.numpy as jnp
from jax import lax
from jax.experimental import pallas as pl
from jax.experimental.pallas import tpu as pltpu

F32 = jnp.float32
BF16 = jnp.bfloat16

D_MODEL = 1024
HEAD_DIM = 64
SSD_HEADS = 8
SSD_INNER = SSD_HEADS * HEAD_DIM
SSD_GROUPS = 2
SSD_STATE = 128
SSD_CONV = 4
SSD_CHUNK = 128
SSD_CONV_DIM = SSD_INNER + 2 * SSD_GROUPS * SSD_STATE
ATT_HEADS = 8
ATT_KV_HEADS = 2
ATT_GROUP = ATT_HEADS // ATT_KV_HEADS
ATT_WIDTH = ATT_HEADS * HEAD_DIM
KV_WIDTH = ATT_KV_HEADS * HEAD_DIM
WINDOW = 128
D_FF = 2816
FFN_CONV = 3
EPS = 1e-6
NEG = -1e30
LOG2E = 1.4426950408889634

LANES = 128
SUBLANES = 8
VMEM_LIMIT = 56 * 1024 * 1024

_Z0, _X0, _Q0, _K0, _V0, _PEND = 0, 512, 1536, 2048, 2176, 2304
ALIBI = tuple(float(2.0 ** (-8.0 * (h + 1) / ATT_HEADS)) for h in range(ATT_HEADS))


def _rms(x, w):
    ms = jnp.mean(x * x, axis=-1, keepdims=True)
    return x * lax.rsqrt(ms + EPS) * w


def _silu(x):
    return x * jax.nn.sigmoid(x)


def _softplus(x):
    return jnp.maximum(x, 0.0) + jnp.log(1.0 + jnp.exp(-jnp.abs(x)))


def _dot(a, b):
    return jnp.dot(a, b, preferred_element_type=F32)


def _dot_nt(a, b):
    return lax.dot_general(a, b, (((1,), (1,)), ((), ())), preferred_element_type=F32)


def _dot_tn(a, b):
    return lax.dot_general(a, b, (((0,), (0,)), ((), ())), preferred_element_type=F32)


def _const_spec(shape):
    nd = len(shape)
    return pl.BlockSpec(shape, lambda *_: (0,) * nd, pipeline_mode=pl.Buffered(1))


def _full_spec(shape):
    nd = len(shape)
    return pl.BlockSpec(shape, lambda *_: (0,) * nd)


_FFN_COLS = 256


def _ffn_kernel(*refs, tm, pad, shift, carry, out_proj):
    refs = list(refs)
    x_ref = refs.pop(0)
    if out_proj:
        y_ref = refs.pop(0)
        wout_ref = refs.pop(0)
    if not carry:
        prev_ref = refs.pop(0)
    n2_ref, wup_ref, cw_ref, cb_ref, wdn_ref, o_ref, tail_ref, hp_ref, act_ref = refs

    if carry:
        @pl.when(pl.program_id(1) == 0)
        def _():
            hp_ref[0:pad, :] = jnp.zeros((pad, 2 * D_FF), F32)
        x = x_ref[0]
    else:
        hp_ref[0:pad, :] = prev_ref[...]
        x = x_ref[...]

    if out_proj:
        x = x + _dot(y_ref[...].astype(BF16), wout_ref[...])
    hn = _rms(x, n2_ref[...]).astype(BF16)
    hp_ref[pad:pad + tm, :] = _dot(hn, wup_ref[...])

    for j in range(0, D_FF, _FFN_COLS):
        def conv(c0):
            cols = slice(c0, c0 + _FFN_COLS)
            acc = hp_ref[pad:pad + tm, cols] * cw_ref[2:3, cols]
            acc = acc + hp_ref[pad - shift:pad - shift + tm, cols] * cw_ref[1:2, cols]
            acc = acc + hp_ref[pad - 2 * shift:pad - 2 * shift + tm, cols] * cw_ref[0:1, cols]
            return acc + cb_ref[:, cols]
        act_ref[:, j:j + _FFN_COLS] = (_silu(conv(j)) * conv(D_FF + j)).astype(BF16)

    out = x + _dot(act_ref[...], wdn_ref[...])
    tail = hp_ref[tm + pad - 2 * shift:tm + pad, :]
    if carry:
        o_ref[0] = out
        tail_ref[0] = tail
        hp_ref[0:pad, :] = hp_ref[tm:tm + pad, :]
    else:
        o_ref[...] = out
        tail_ref[...] = tail


def _ffn_prompt(x, n2, wup, cw, cb, wdn, *, tm):
    b, s, _ = x.shape
    pad = SUBLANES
    kern = functools.partial(_ffn_kernel, tm=tm, pad=pad, shift=1, carry=True, out_proj=False)
    return pl.pallas_call(
        kern,
        grid=(b, s // tm),
        in_specs=[
            pl.BlockSpec((1, tm, D_MODEL), lambda i, t: (i, t, 0)),
            _const_spec((1, D_MODEL)),
            _const_spec((D_MODEL, 2 * D_FF)),
            _const_spec((FFN_CONV, 2 * D_FF)),
            _const_spec((1, 2 * D_FF)),
            _const_spec((D_FF, D_MODEL)),
        ],
        out_specs=[
            pl.BlockSpec((1, tm, D_MODEL), lambda i, t: (i, t, 0)),
            pl.BlockSpec((1, FFN_CONV - 1, 2 * D_FF), lambda i, t: (i, 0, 0)),
        ],
        out_shape=[
            jax.ShapeDtypeStruct((b, s, D_MODEL), F32),
            jax.ShapeDtypeStruct((b, FFN_CONV - 1, 2 * D_FF), F32),
        ],
        scratch_shapes=[
            pltpu.VMEM((tm + pad, 2 * D_FF), F32),
            pltpu.VMEM((tm, D_FF), BF16),
        ],
        compiler_params=pltpu.CompilerParams(
            dimension_semantics=("arbitrary", "arbitrary"), vmem_limit_bytes=VMEM_LIMIT),
        name="ffn_prompt",
    )(x, n2, wup, cw, cb, wdn)


def _ffn_sample(x, y, wout, prev, n2, wup, cw, cb, wdn):
    rows = x.shape[0]
    bs = prev.shape[0] // (FFN_CONV - 1)
    pad = (FFN_CONV - 1) * bs
    kern = functools.partial(_ffn_kernel, tm=rows, pad=pad, shift=bs, carry=False, out_proj=True)
    return pl.pallas_call(
        kern,
        grid=(1,),
        in_specs=[
            _const_spec((rows, D_MODEL)),
            _const_spec((rows, D_MODEL)),
            _const_spec((D_MODEL, D_MODEL)),
            _const_spec((pad, 2 * D_FF)),
            _const_spec((1, D_MODEL)),
            _const_spec((D_MODEL, 2 * D_FF)),
            _const_spec((FFN_CONV, 2 * D_FF)),
            _const_spec((1, 2 * D_FF)),
            _const_spec((D_FF, D_MODEL)),
        ],
        out_specs=[
            _full_spec((rows, D_MODEL)),
            _full_spec((pad, 2 * D_FF)),
        ],
        out_shape=[
            jax.ShapeDtypeStruct((rows, D_MODEL), F32),
            jax.ShapeDtypeStruct((pad, 2 * D_FF), F32),
        ],
        scratch_shapes=[
            pltpu.VMEM((rows + pad, 2 * D_FF), F32),
            pltpu.VMEM((rows, D_FF), BF16),
        ],
        compiler_params=pltpu.CompilerParams(
            dimension_semantics=("arbitrary",), vmem_limit_bytes=VMEM_LIMIT),
        name="ffn_sample",
    )(x, y, wout, prev, n2, wup, cw, cb, wdn)


def _tri_cumsum(x, ltri):
    hi = x.astype(BF16)
    lo = (x - hi.astype(F32)).astype(BF16)
    return _dot(ltri, hi) + _dot(ltri, lo)


def _half_variants(a, lane_lo):
    ar = pltpu.roll(a, HEAD_DIM, axis=1)
    zero = jnp.zeros_like(a)
    return {
        (0, 0): jnp.where(lane_lo, a, zero).astype(BF16),
        (1, 1): jnp.where(lane_lo, zero, a).astype(BF16),
        (1, 0): jnp.where(lane_lo, ar, zero).astype(BF16),
        (0, 1): jnp.where(lane_lo, zero, ar).astype(BF16),
    }


def _conv_silu(cbuf_ref, cw_ref, cb_ref, r0, cols):
    pad, c = SUBLANES, SSD_CHUNK
    acc = cb_ref[:, cols] + cbuf_ref[pad + r0:pad + r0 + c, cols] * cw_ref[SSD_CONV - 1:SSD_CONV, cols]
    for i in range(1, SSD_CONV):
        acc = acc + cbuf_ref[pad + r0 - i:pad + r0 - i + c, cols] * cw_ref[SSD_CONV - 1 - i:SSD_CONV - i, cols]
    return _silu(acc)


_CHUNK_WEIGHTS = (3.0, 3.0, 4.0) + (1.0,) * 4 + (1.5,) * 4 + (1.5,)


def _chunk_pieces(r0, bufs, kprev, vprev, first_mask, bias_ref, st_ref, sink_ref, cw_ref, cb_ref, a, dskip, snorm,
                  qnw, knw, ltri, causal, lane_lo):
    proj_ref, cbuf_ref, dt_ref, y_ref = bufs
    c = SSD_CHUNK
    rows = slice(r0, r0 + c)
    gw = SSD_INNER // SSD_GROUPS
    qo, ko, vo = _Q0 - SSD_CONV_DIM, _K0 - SSD_CONV_DIM, _V0 - SSD_CONV_DIM

    dt = dt_ref[rows, :]
    acum = _tri_cumsum(dt * a, ltri)
    bc = _conv_silu(cbuf_ref, cw_ref, cb_ref, r0, slice(SSD_INNER, SSD_CONV_DIM))
    bms = [bc[:, g * SSD_STATE:(g + 1) * SSD_STATE].astype(BF16) for g in range(SSD_GROUPS)]
    cms = [bc[:, (SSD_GROUPS + g) * SSD_STATE:(SSD_GROUPS + g + 1) * SSD_STATE].astype(BF16)
           for g in range(SSD_GROUPS)]
    cbs = [_dot_nt(cms[g], bms[g]) for g in range(SSD_GROUPS)]
    yoffs = [_dot(cms[g], st_ref[:, g * gw:(g + 1) * gw].astype(BF16)) for g in range(SSD_GROUPS)]
    yield

    q = proj_ref[rows, qo:qo + ATT_WIDTH]
    kn = _norm_k(proj_ref[rows, ko:ko + KV_WIDTH], knw, lane_lo)
    v = proj_ref[rows, vo:vo + KV_WIDTH]
    kcur = _half_variants(kn, lane_lo)
    vcur = _half_variants(v, lane_lo)
    qn = []
    for m in range(ATT_HEADS // 2):
        qp = q[:, m * LANES:(m + 1) * LANES]
        q2p = qp * qp
        rs_lo = lax.rsqrt(jnp.sum(jnp.where(lane_lo, q2p, 0.0), axis=-1, keepdims=True) / HEAD_DIM + EPS)
        rs_hi = lax.rsqrt(jnp.sum(jnp.where(lane_lo, 0.0, q2p), axis=-1, keepdims=True) / HEAD_DIM + EPS)
        qn.append((qp * qnw[:, m * LANES:(m + 1) * LANES]
                   * (jnp.where(lane_lo, rs_lo, rs_hi) * (HEAD_DIM ** -0.5 * LOG2E))).astype(BF16))
    scores = {}
    for kv in range(ATT_KV_HEADS):
        qs = jnp.concatenate([qn[2 * kv], qn[2 * kv + 1]], axis=0)
        for e in range(2):
            sr = _dot_nt(qs, jnp.concatenate([kprev[(kv, e)], kcur[(kv, e)]], axis=0))
            scores[ATT_GROUP * kv + e] = sr[0:c]
            scores[ATT_GROUP * kv + 2 + e] = sr[c:2 * c]
    vall = {key: jnp.concatenate([vprev[key], vcur[key]], axis=0) for key in vcur}
    yield

    xs = _conv_silu(cbuf_ref, cw_ref, cb_ref, r0, slice(0, SSD_INNER))
    alast = acum[c - 1:c, :]
    xw = (xs * (jnp.exp(alast - acum) * dt)).astype(BF16)
    cdec = jnp.exp(alast)
    for g in range(SSD_GROUPS):
        st_ref[:, g * gw:(g + 1) * gw] = (st_ref[:, g * gw:(g + 1) * gw] * cdec[:, g * gw:(g + 1) * gw]
                                          + _dot_tn(bms[g], xw[:, g * gw:(g + 1) * gw]))
    yield
    eac = jnp.exp(acum)
    xdt = xs * dt
    acum_t = acum.T
    ys = []
    for pm in range(SSD_INNER // LANES):
        g, m = divmod(pm, gw // LANES)
        lanes = slice(pm * LANES, (pm + 1) * LANES)
        a_pair = acum[:, lanes]
        a_roll = pltpu.roll(a_pair, HEAD_DIM, axis=1)
        ms = []
        for e in range(2):
            h = 2 * pm + e
            colb = jnp.where(lane_lo, a_pair, a_roll) if e == 0 else jnp.where(lane_lo, a_roll, a_pair)
            rowb = jnp.broadcast_to(acum_t[h * HEAD_DIM:h * HEAD_DIM + 1, :], (c, c))
            dec = jnp.exp(jnp.where(causal, colb - rowb, NEG))
            ms.append((cbs[g] * dec).astype(BF16))
        xd = xdt[:, lanes]
        rhs = jnp.concatenate([jnp.where(lane_lo, xd, 0.0), jnp.where(lane_lo, 0.0, xd)], axis=0).astype(BF16)
        yield
        ydiag = _dot(jnp.concatenate(ms, axis=1), rhs)
        ys.append(ydiag + yoffs[g][:, m * LANES:(m + 1) * LANES] * eac[:, lanes])

    outs = []
    for m in range(ATT_HEADS // 2):
        kv = (2 * m) // ATT_GROUP
        ps, invs = [], []
        for e in range(2):
            h = 2 * m + e
            s = scores[h] + bias_ref[h]
            if first_mask is not None:
                s = jnp.where(first_mask, s, NEG)
            sk = sink_ref[h] * LOG2E
            mx = jnp.maximum(jnp.max(s, axis=-1, keepdims=True), sk)
            p = jnp.exp2(s - mx)
            den = jnp.sum(p, axis=-1, keepdims=True) + jnp.exp2(sk - mx)
            ps.append(p.astype(BF16))
            invs.append(1.0 / den)
        yield
        o = _dot(jnp.concatenate(ps, axis=1), jnp.concatenate([vall[(kv, 0)], vall[(kv, 1)]], axis=0))
        outs.append(o * jnp.where(lane_lo, invs[0], invs[1]))

    y = (jnp.concatenate(ys, axis=1) + xs * dskip) * _silu(proj_ref[rows, _Z0:_Z0 + SSD_INNER])
    y_ref[rows, 0:SSD_INNER] = jnp.concatenate(
        [_rms(y[:, g * gw:(g + 1) * gw], snorm[:, g * gw:(g + 1) * gw]) for g in range(SSD_GROUPS)],
        axis=1).astype(BF16)
    y_ref[rows, SSD_INNER:] = jnp.concatenate(outs, axis=1).astype(BF16)
    yield
    return kn, v, kcur, vcur


def _norm_k(k, knw, lane_lo):
    k2 = k * k
    r0 = lax.rsqrt(jnp.sum(jnp.where(lane_lo, k2, 0.0), axis=-1, keepdims=True) / HEAD_DIM + EPS)
    r1 = lax.rsqrt(jnp.sum(jnp.where(lane_lo, 0.0, k2), axis=-1, keepdims=True) / HEAD_DIM + EPS)
    return k * jnp.where(lane_lo, r0, r1) * knw


def _weave(main, weights, side):
    done = 0
    total = float(sum(weights))
    acc = 0.0
    for k in range(len(weights) + 1):
        try:
            next(main)
        except StopIteration as stop:
            for thunk in side[done:]:
                thunk()
            return stop.value
        acc += weights[k]
        while done < len(side) and done * total < acc * len(side):
            side[done]()
            done += 1
    raise AssertionError("main generator has more pieces than declared")


_PROJ_COLS = 256


def _project_pieces(load_x, n1_ref, win_ref, wdt_ref, dtb_ref, bufs, tm):
    proj_ref, cbuf_ref, dt_ref, _ = bufs
    pad = SUBLANES
    box = {}

    def norm():
        box['hn'] = _rms(load_x(), n1_ref[...]).astype(BF16)

    def slab(c0, c1):
        def run():
            res = _dot(box['hn'], win_ref[:, c0:c1])
            if c0 < _X0:
                proj_ref[:, c0:c1] = res
            elif c0 < _Q0:
                cbuf_ref[pad:pad + tm, c0 - _X0:c1 - _X0] = res
            else:
                proj_ref[:, c0 - SSD_CONV_DIM:c1 - SSD_CONV_DIM] = res
        return run

    def step_sizes(c0, c1):
        def run():
            dt_ref[:, c0:c1] = _softplus(_dot(box['hn'], wdt_ref[:, c0:c1]) + dtb_ref[:, c0:c1])
        return run

    slabs = []
    for seg0, seg1 in ((_Z0, _X0), (_X0, _Q0), (_Q0, _PEND)):
        for c0 in range(seg0, seg1, _PROJ_COLS):
            slabs.append(slab(c0, min(c0 + _PROJ_COLS, seg1)))
    return [norm] + slabs + [step_sizes(c0, c0 + _PROJ_COLS) for c0 in range(0, SSD_INNER, _PROJ_COLS)]


def _outproj_pieces(load_x, y_ref, wout_ref, store):
    def slab(c0, c1):
        def run():
            store(c0, c1, load_x(c0, c1) + _dot(y_ref[...], wout_ref[:, c0:c1]))
        return run
    return [slab(c0, c0 + 2 * LANES) for c0 in range(0, D_MODEL, 2 * LANES)]


def _score_bias(bias_ref):
    c = SSD_CHUNK
    qi = lax.broadcasted_iota(jnp.int32, (c, 2 * c), 0)
    sj = lax.broadcasted_iota(jnp.int32, (c, 2 * c), 1)
    reli = qi + c - sj
    rel = reli.astype(F32)
    in_window = (reli >= 0) & (reli < WINDOW)
    for h in range(ATT_HEADS):
        bias_ref[h] = jnp.where(in_window, (-ALIBI[h] * LOG2E) * rel, NEG)


def _mix_tile(bufs, first_col, kprev, vprev, st_ref, bias_ref, sink_ref, cw_ref, cb_ref, alog_ref, dskip_ref,
              snorm_ref, qnw_ref, knw_ref, tm):
    c = SSD_CHUNK
    row = lax.broadcasted_iota(jnp.int32, (c, c), 0)
    col = lax.broadcasted_iota(jnp.int32, (c, c), 1)
    causal = row >= col
    ltri = jnp.where(causal, 1.0, 0.0).astype(BF16)
    lane_lo = col < HEAD_DIM
    a = -jnp.exp(alog_ref[...])

    kn = v = None
    for ci in range(tm // c):
        first_mask = None
        if ci == 0 and first_col is not None:
            first_mask = lax.broadcasted_iota(jnp.int32, (c, 2 * c), 1) >= first_col
        kn, v, kprev, vprev = yield from _chunk_pieces(
            ci * c, bufs, kprev, vprev, first_mask, bias_ref, st_ref, sink_ref, cw_ref, cb_ref, a, dskip_ref[...],
            snorm_ref[...], qnw_ref[...], knw_ref[...], ltri, causal, lane_lo)
    return kn, v, kprev, vprev


def _mixer_prompt_kernel(sink_ref, xcur_ref, xnext_ref, n1_ref, win_ref, wdt_ref, cw_ref, cb_ref, dtb_ref, alog_ref,
                         dskip_ref, snorm_ref, qnw_ref, knw_ref, wout_ref,
                         o_ref, ssm_ref, conv_ref, kc_ref, vc_ref,
                         proj_a, cbuf_a, dt_a, y_a, proj_b, cbuf_b, dt_b, y_b, st_ref, kprev_ref, vprev_ref, bias_ref,
                         *, tm):
    i = pl.program_id(0)
    s = pl.program_id(1)
    ns = pl.num_programs(1)
    pad = SUBLANES
    bufs_a = (proj_a, cbuf_a, dt_a, y_a)
    bufs_b = (proj_b, cbuf_b, dt_b, y_b)
    proj_args = (n1_ref, win_ref, wdt_ref, dtb_ref)
    mix_args = (st_ref, bias_ref, sink_ref, cw_ref, cb_ref, alog_ref, dskip_ref, snorm_ref, qnw_ref, knw_ref, tm)

    @pl.when((i == 0) & (s == 0))
    def _():
        _score_bias(bias_ref)
        for piece in _project_pieces(lambda: xcur_ref[0, 0:tm, :], *proj_args, bufs_a, tm):
            piece()

    @pl.when(s == 0)
    def _():
        st_ref[...] = jnp.zeros(st_ref.shape, F32)
        cbuf_a[0:pad, :] = jnp.zeros((pad, SSD_CONV_DIM), F32)
        kprev_ref[...] = jnp.zeros(kprev_ref.shape, F32)
        vprev_ref[...] = jnp.zeros(vprev_ref.shape, F32)

    @pl.when(s > 0)
    def _():
        cbuf_a[0:pad, :] = cbuf_b[tm:tm + pad, :]

    lane_lo = lax.broadcasted_iota(jnp.int32, (SSD_CHUNK, LANES), 1) < HEAD_DIM
    first_col = jnp.where(s == 0, SSD_CHUNK, 0)

    n_mix = _CHUNK_WEIGHTS * (tm // SSD_CHUNK)

    def outproj(r0, y_ref):
        def store(c0, c1, val):
            o_ref[0, r0:r0 + tm, c0:c1] = val
        return _outproj_pieces(lambda c0, c1: xcur_ref[0, r0:r0 + tm, c0:c1], y_ref, wout_ref, store)

    side = _project_pieces(lambda: xcur_ref[0, tm:2 * tm, :], *proj_args, bufs_b, tm)
    _, _, kprev, vprev = _weave(
        _mix_tile(bufs_a, first_col, _half_variants(kprev_ref[...], lane_lo),
                  _half_variants(vprev_ref[...], lane_lo), *mix_args), n_mix, side)
    cbuf_b[0:pad, :] = cbuf_a[tm:tm + pad, :]
    side = outproj(0, y_a) + _project_pieces(lambda: xnext_ref[0], *proj_args, bufs_a, tm)
    kn, v, _, _ = _weave(_mix_tile(bufs_b, None, kprev, vprev, *mix_args), n_mix, side)
    for piece in outproj(tm, y_b):
        piece()
    kprev_ref[...] = kn
    vprev_ref[...] = v

    @pl.when(s == ns - 1)
    def _():
        ssm_ref[0] = st_ref[...].T
        conv_ref[0] = cbuf_b[tm + pad - (SSD_CONV - 1):tm + pad, :]
        kc_ref[0] = kn
        vc_ref[0] = v


def _mixer_prompt(x, sinks, n1, win, wdt, cw, cb, dtb, alog, dskip, snorm, qnw, knw, wout, *, tm):
    b, s, _ = x.shape
    pw = _PEND - SSD_CONV_DIM
    nt = s // tm
    ns = nt // 2
    kern = functools.partial(_mixer_prompt_kernel, tm=tm)
    consts = [n1, win, wdt, cw, cb, dtb, alog, dskip, snorm, qnw, knw, wout]

    def next_tile(i, t, *_):
        flat = jnp.minimum((i * ns + t) * 2 + 2, b * nt - 1)
        return (flat // nt, flat % nt, 0)

    bufset = [
        pltpu.VMEM((tm, pw), F32),
        pltpu.VMEM((tm + SUBLANES, SSD_CONV_DIM), F32),
        pltpu.VMEM((tm, SSD_INNER), F32),
        pltpu.VMEM((tm, D_MODEL), BF16),
    ]
    return pl.pallas_call(
        kern,
        grid_spec=pltpu.PrefetchScalarGridSpec(
            num_scalar_prefetch=1,
            grid=(b, ns),
            in_specs=[pl.BlockSpec((1, 2 * tm, D_MODEL), lambda i, t, *_: (i, t, 0)),
                      pl.BlockSpec((1, tm, D_MODEL), next_tile)]
            + [_const_spec(w.shape) for w in consts],
            out_specs=[
                pl.BlockSpec((1, 2 * tm, D_MODEL), lambda i, t, *_: (i, t, 0)),
                pl.BlockSpec((1, SSD_INNER, SSD_STATE), lambda i, t, *_: (i, 0, 0)),
                pl.BlockSpec((1, SSD_CONV - 1, SSD_CONV_DIM), lambda i, t, *_: (i, 0, 0)),
                pl.BlockSpec((1, WINDOW, KV_WIDTH), lambda i, t, *_: (i, 0, 0)),
                pl.BlockSpec((1, WINDOW, KV_WIDTH), lambda i, t, *_: (i, 0, 0)),
            ],
            scratch_shapes=bufset + bufset + [
                pltpu.VMEM((SSD_STATE, SSD_INNER), F32),
                pltpu.VMEM((WINDOW, KV_WIDTH), F32),
                pltpu.VMEM((WINDOW, KV_WIDTH), F32),
                pltpu.VMEM((ATT_HEADS, SSD_CHUNK, 2 * SSD_CHUNK), F32),
            ],
        ),
        out_shape=[
            jax.ShapeDtypeStruct((b, s, D_MODEL), F32),
            jax.ShapeDtypeStruct((b, SSD_INNER, SSD_STATE), F32),
            jax.ShapeDtypeStruct((b, SSD_CONV - 1, SSD_CONV_DIM), F32),
            jax.ShapeDtypeStruct((b, WINDOW, KV_WIDTH), F32),
            jax.ShapeDtypeStruct((b, WINDOW, KV_WIDTH), F32),
        ],
        compiler_params=pltpu.CompilerParams(
            dimension_semantics=("arbitrary", "arbitrary"), vmem_limit_bytes=VMEM_LIMIT),
        name="mixer_prompt",
    )(sinks, x, x, *consts)


_S_Z0, _S_X0, _S_DT0, _S_Q0, _S_K0, _S_V0, _S_END = 0, 512, 1536, 2048, 2560, 2688, 2816


def _sample_in_kernel(x_ref, prev_ref, n1_ref, win_ref, wdt_ref, cw_ref, cb_ref, dtb_ref, o_ref, tail_ref, cbuf_ref,
                      *, rows, bs):
    pad = (SSD_CONV - 1) * bs
    hn = _rms(x_ref[...], n1_ref[...]).astype(BF16)
    o_ref[:, _S_Z0:_S_X0] = _dot(hn, win_ref[:, _Z0:_X0])
    cbuf_ref[0:pad, :] = prev_ref[...]
    cbuf_ref[pad:pad + rows, :] = _dot(hn, win_ref[:, _X0:_Q0])
    acc = cb_ref[...] + cbuf_ref[pad:pad + rows, :] * cw_ref[SSD_CONV - 1:SSD_CONV, :]
    for i in range(1, SSD_CONV):
        acc = acc + cbuf_ref[pad - i * bs:pad - i * bs + rows, :] * cw_ref[SSD_CONV - 1 - i:SSD_CONV - i, :]
    o_ref[:, _S_X0:_S_DT0] = _silu(acc)
    o_ref[:, _S_DT0:_S_Q0] = _softplus(_dot(hn, wdt_ref[...]) + dtb_ref[...])
    o_ref[:, _S_Q0:_S_END] = _dot(hn, win_ref[:, _Q0:_PEND])
    tail_ref[...] = cbuf_ref[rows:rows + pad, :]


def _sample_in(x, prev, n1, win, wdt, cw, cb, dtb):
    rows = x.shape[0]
    pad = prev.shape[0]
    bs = pad // (SSD_CONV - 1)
    args = (x, prev, n1, win, wdt, cw, cb, dtb)
    return pl.pallas_call(
        functools.partial(_sample_in_kernel, rows=rows, bs=bs),
        grid=(1,),
        in_specs=[_const_spec(a.shape) for a in args],
        out_specs=[_full_spec((rows, _S_END)), _full_spec((pad, SSD_CONV_DIM))],
        out_shape=[jax.ShapeDtypeStruct((rows, _S_END), F32), jax.ShapeDtypeStruct((pad, SSD_CONV_DIM), F32)],
        scratch_shapes=[pltpu.VMEM((rows + pad, SSD_CONV_DIM), F32)],
        compiler_params=pltpu.CompilerParams(dimension_semantics=("arbitrary",), vmem_limit_bytes=VMEM_LIMIT),
        name="sample_in",
    )(*args)


_SAMPLE_TILES = 4


def _sample_mix_kernel(sink_ref, p_ref, st_ref, kc_ref, vc_ref, alog_ref, dskip_ref, snorm_ref, qnw_ref, knw_ref,
                       *rest, tlen, layer, all_layers):
    y_ref, sto_ref, ko_ref, vo_ref = rest[-4:]
    if all_layers:
        for ref in (sto_ref, ko_ref, vo_ref):
            for other in range(ref.shape[0]):
                if other != layer:
                    ref[other] = jnp.zeros(ref.shape[1:], F32)
        sto_ref, ko_ref, vo_ref = sto_ref.at[layer], ko_ref.at[layer], vo_ref.at[layer]
    nseq = SUBLANES // tlen
    for j in range(_SAMPLE_TILES):
        rows = pl.ds(j * SUBLANES, SUBLANES)
        seqs = pl.ds(j * nseq, nseq)
        _sample_mix_tile(sink_ref, p_ref.at[rows], st_ref.at[seqs], kc_ref.at[seqs], vc_ref.at[seqs], alog_ref,
                         dskip_ref, snorm_ref, qnw_ref, knw_ref, y_ref.at[rows], sto_ref.at[seqs], ko_ref.at[seqs],
                         vo_ref.at[seqs], tlen)


def _sample_mix_tile(sink_ref, p_ref, st_ref, kc_ref, vc_ref, alog_ref, dskip_ref, snorm_ref, qnw_ref, knw_ref,
                     y_ref, sto_ref, ko_ref, vo_ref, tlen):
    nseq = SUBLANES // tlen
    w = WINDOW
    pk = p_ref[...]
    z = pk[:, _S_Z0:_S_X0]
    xs = pk[:, _S_X0:_S_X0 + SSD_INNER]
    dt = pk[:, _S_DT0:_S_Q0]
    q = pk[:, _S_Q0:_S_K0]
    k = pk[:, _S_K0:_S_V0]
    v = pk[:, _S_V0:_S_END]
    gw = SSD_INNER // SSD_GROUPS
    bms = [pk[:, _S_X0 + SSD_INNER + g * SSD_STATE:_S_X0 + SSD_INNER + (g + 1) * SSD_STATE]
           for g in range(SSD_GROUPS)]
    c0 = _S_X0 + SSD_INNER + SSD_GROUPS * SSD_STATE
    cms = [pk[:, c0 + g * SSD_STATE:c0 + (g + 1) * SSD_STATE] for g in range(SSD_GROUPS)]

    rowi = lax.broadcasted_iota(jnp.int32, (SUBLANES, SSD_INNER), 0)
    assert tlen & (tlen - 1) == 0 and SUBLANES % tlen == 0
    tshift = tlen.bit_length() - 1
    sshift = SUBLANES.bit_length() - 1
    tpos = rowi & (tlen - 1)
    seq = rowi >> tshift
    seq_g = lax.broadcasted_iota(jnp.int32, (SUBLANES, gw), 0) >> tshift
    lane_lo = lax.broadcasted_iota(jnp.int32, (SUBLANES, LANES), 1) < HEAD_DIM

    dta = dt * (-jnp.exp(alog_ref[...]))
    acum = dta
    for s in range(1, tlen):
        acum = acum + jnp.where(tpos >= s, pltpu.roll(dta, s, axis=0), 0.0)
    tot = acum[tlen - 1:tlen, :]
    for b in range(1, nseq):
        tot = jnp.where(seq == b, acum[(b + 1) * tlen - 1:(b + 1) * tlen, :], tot)
    eac = jnp.exp(acum)
    cdec = jnp.exp(tot)
    xdt = xs * dt
    xw = xs * (jnp.exp(tot - acum) * dt)

    ydiag = jnp.zeros((SUBLANES, SSD_INNER), F32)
    for s in range(tlen):
        xsh = pltpu.roll(xdt, s, axis=0) if s else xdt
        ash = pltpu.roll(acum, s, axis=0) if s else acum
        cbs = []
        for g in range(SSD_GROUPS):
            bsh = pltpu.roll(bms[g], s, axis=0) if s else bms[g]
            cbs.append(jnp.broadcast_to(jnp.sum(cms[g] * bsh, axis=-1, keepdims=True), (SUBLANES, gw)))
        term = jnp.concatenate(cbs, axis=1) * jnp.exp(acum - ash) * xsh
        ydiag = ydiag + jnp.where(tpos >= s, term, 0.0)

    yoff = []
    for g in range(SSD_GROUPS):
        cm = cms[g].astype(BF16)
        bm = bms[g].astype(BF16)
        yo = None
        for b in range(nseq):
            h0 = st_ref[b, g * gw:(g + 1) * gw, :]
            yb = _dot_nt(cm, h0.astype(BF16))
            yo = yb if yo is None else jnp.where(seq_g == b, yb, yo)
            xwb = jnp.where(seq_g == b, xw[:, g * gw:(g + 1) * gw], 0.0).astype(BF16)
            cseq = jnp.broadcast_to(cdec[b * tlen:b * tlen + 1, :], (SUBLANES, SSD_INNER))
            blocks = []
            for hh in range(gw // HEAD_DIM):
                pm = (g * gw + hh * HEAD_DIM) // LANES
                cpair = cseq[:, pm * LANES:(pm + 1) * LANES]
                croll = pltpu.roll(cpair, HEAD_DIM, axis=1)
                chead = jnp.where(lane_lo, cpair, croll) if hh % 2 == 0 else jnp.where(lane_lo, croll, cpair)
                blocks += [chead] * (HEAD_DIM // SUBLANES)
            sto_ref[b, g * gw:(g + 1) * gw, :] = h0 * jnp.concatenate(blocks, axis=0) + _dot_tn(xwb, bm)
        yoff.append(yo)
    y = (ydiag + jnp.concatenate(yoff, axis=1) * eac + xs * dskip_ref[...]) * _silu(z)
    y_ssd = jnp.concatenate(
        [_rms(y[:, g * gw:(g + 1) * gw], snorm_ref[:, g * gw:(g + 1) * gw]) for g in range(SSD_GROUPS)], axis=1)

    kn = _norm_k(k, knw_ref[...], lane_lo)
    for b in range(nseq):
        ko_ref[b, 0:w - tlen, :] = kc_ref[b, tlen:w, :]
        vo_ref[b, 0:w - tlen, :] = vc_ref[b, tlen:w, :]
        ko_ref[b, w - tlen:w, :] = kn[b * tlen:(b + 1) * tlen, :]
        vo_ref[b, w - tlen:w, :] = v[b * tlen:(b + 1) * tlen, :]
    qw = q * qnw_ref[...]
    q2 = q * q
    pieces = []
    for h in range(ATT_HEADS):
        kv = h // ATT_GROUP
        lanes = slice((h // 2) * LANES, (h // 2 + 1) * LANES)
        own = lane_lo if h % 2 == 0 else jnp.logical_not(lane_lo)
        ssq = jnp.sum(jnp.where(own, q2[:, lanes], 0.0), axis=-1, keepdims=True)
        qh = jnp.where(own, qw[:, lanes], 0.0) * (lax.rsqrt(ssq / HEAD_DIM + EPS) * (HEAD_DIM ** -0.5))
        pieces.append(qh if h % 2 == kv else pltpu.roll(qh, HEAD_DIM, axis=1))
    qrows = jnp.concatenate(pieces, axis=0).astype(BF16)

    nq = ATT_HEADS * SUBLANES
    ncol = w + SUBLANES
    ri = lax.broadcasted_iota(jnp.int32, (nq, ncol), 0)
    cj = lax.broadcasted_iota(jnp.int32, (nq, ncol), 1)
    rt = ri & (tlen - 1)
    rseq = (ri & (SUBLANES - 1)) >> tshift
    rhead = ri >> sshift
    nj = jnp.maximum(cj - w, 0)
    in_cache = cj < w
    rel = jnp.where(in_cache, rt + w - cj, rt - (nj & (tlen - 1))).astype(F32)
    ok_new = ((nj >> tshift) == rseq) & ((nj & (tlen - 1)) <= rt)
    ok = (in_cache & (cj > rt)) | (jnp.logical_not(in_cache) & ok_new)
    slope = jnp.zeros((nq, ncol), F32)
    for h in range(ATT_HEADS):
        slope = jnp.where(rhead == h, ALIBI[h], slope)
    rh1 = lax.broadcasted_iota(jnp.int32, (nq, 1), 0) >> sshift
    sk = jnp.zeros((nq, 1), F32)
    for h in range(ATT_HEADS):
        sk = jnp.where(rh1 == h, sink_ref[h], sk)
    bias = slope * rel
    rseq1 = (lax.broadcasted_iota(jnp.int32, (nq, LANES), 0) & (SUBLANES - 1)) >> tshift

    out = jnp.zeros((nq, LANES), F32)
    for b in range(nseq):
        kc = jnp.concatenate([kc_ref[b], kn], axis=0).astype(BF16)
        vc = jnp.concatenate([vc_ref[b], v], axis=0).astype(BF16)
        s = jnp.where(ok, _dot_nt(qrows, kc) - bias, NEG)
        mx = jnp.maximum(jnp.max(s, axis=-1, keepdims=True), sk)
        p = jnp.exp(s - mx)
        den = jnp.sum(p, axis=-1, keepdims=True) + jnp.exp(sk - mx)
        o = _dot(p.astype(BF16), vc) / den
        out = jnp.where(rseq1 == b, o, out)
    pairs = []
    for m in range(ATT_HEADS // 2):
        halves = []
        for e in range(2):
            h = 2 * m + e
            blk = out[h * SUBLANES:(h + 1) * SUBLANES, :]
            halves.append(blk if h // ATT_GROUP == e else pltpu.roll(blk, HEAD_DIM, axis=1))
        pairs.append(jnp.where(lane_lo, halves[0], halves[1]))
    y_ref[...] = jnp.concatenate([y_ssd] + pairs, axis=1)


def _sample_mix(layer, packed, sinks, st, kc, vc, alog, dskip, snorm, qnw, knw, prev_outs, *, tlen):
    rows = packed.shape[0]
    nseq = SUBLANES // tlen
    consts = [alog, dskip, snorm, qnw, knw]
    carried = [] if prev_outs is None else list(prev_outs)
    n_in = 1 + 4 + len(consts)

    step_rows = SUBLANES * _SAMPLE_TILES
    step_seqs = nseq * _SAMPLE_TILES
    assert rows % step_rows == 0

    def slab(shape):
        return pl.BlockSpec((None, step_seqs) + shape, lambda i, *_: (layer, i, 0, 0))

    def out_slab(shape):
        if prev_outs is None:
            return pl.BlockSpec((st.shape[0], step_seqs) + shape, lambda i, *_: (0, i, 0, 0))
        return slab(shape)

    return pl.pallas_call(
        functools.partial(_sample_mix_kernel, tlen=tlen, layer=layer, all_layers=prev_outs is None),
        grid_spec=pltpu.PrefetchScalarGridSpec(
            num_scalar_prefetch=1,
            grid=(rows // step_rows,),
            in_specs=[
                pl.BlockSpec((step_rows, _S_END), lambda i, *_: (i, 0)),
                slab((SSD_INNER, SSD_STATE)), slab((WINDOW, KV_WIDTH)), slab((WINDOW, KV_WIDTH)),
            ] + [_const_spec(a.shape) for a in consts] + [pl.BlockSpec(memory_space=pl.ANY)] * len(carried),
            out_specs=[
                pl.BlockSpec((step_rows, D_MODEL), lambda i, *_: (i, 0)),
                out_slab((SSD_INNER, SSD_STATE)), out_slab((WINDOW, KV_WIDTH)), out_slab((WINDOW, KV_WIDTH)),
            ],
        ),
        out_shape=[
            jax.ShapeDtypeStruct((rows, D_MODEL), F32),
            jax.ShapeDtypeStruct(st.shape, F32),
            jax.ShapeDtypeStruct(kc.shape, F32),
            jax.ShapeDtypeStruct(vc.shape, F32),
        ],
        input_output_aliases={n_in + j: 1 + j for j in range(len(carried))},
        compiler_params=pltpu.CompilerParams(dimension_semantics=("arbitrary",), vmem_limit_bytes=VMEM_LIMIT),
        name="sample_mix",
    )(sinks, packed, st, kc, vc, *consts, *carried)


def _prep_layer(p, i):
    w_in = p['w_in'][i]
    s0, s1, s2, s3, s4 = (SSD_INNER, SSD_INNER + SSD_CONV_DIM, SSD_INNER + SSD_CONV_DIM + SSD_HEADS,
                          SSD_INNER + SSD_CONV_DIM + SSD_HEADS + ATT_WIDTH,
                          SSD_INNER + SSD_CONV_DIM + SSD_HEADS + ATT_WIDTH + KV_WIDTH)
    win = jnp.concatenate([w_in[:, :s1], w_in[:, s2:]], axis=1).astype(BF16)
    wdt = jnp.repeat(w_in[:, s1:s2], HEAD_DIM, axis=1).astype(BF16)

    def per_head(v):
        return jnp.repeat(v, HEAD_DIM)[None, :]

    mixer = (
        p['attn_sinks'][i],
        p['norm1_w'][i][None, :], win, wdt, p['ssd_conv_w'][i], p['ssd_conv_b'][i][None, :],
        per_head(p['dt_bias'][i]), per_head(p['a_log'][i]), per_head(p['d_skip'][i]), p['ssd_norm_w'][i][None, :],
        jnp.tile(p['q_norm_w'][i], ATT_HEADS)[None, :], jnp.tile(p['k_norm_w'][i], ATT_KV_HEADS)[None, :],
        p['w_out'][i].astype(BF16),
    )
    ffn = (
        p['norm2_w'][i][None, :], p['w_up'][i].astype(BF16), p['ffn_conv_w'][i], p['ffn_conv_b'][i][None, :],
        p['w_down'][i].astype(BF16),
    )
    return {'mixer': mixer, 'ffn': ffn}


_TM_MIXER = 256
_TM_FFN = 512


def _to_time_major(a):
    bs, t, c = a.shape
    return a.transpose(1, 0, 2).reshape(t * bs, c)


def _to_seq_major(a, bs):
    return a.reshape(a.shape[0] // bs, bs, a.shape[1]).transpose(1, 0, 2)


def kernel(x_prompt, x_sample, state_ssm, state_ssd_conv, cache_swa_k, cache_swa_v, state_ffn_conv, norm1_w, w_in,
           ssd_conv_w, ssd_conv_b, dt_bias, a_log, d_skip, ssd_norm_w, q_norm_w, k_norm_w, attn_sinks, w_out, norm2_w,
           w_up, ffn_conv_w, ffn_conv_b, w_down):
    p = dict(norm1_w=norm1_w, w_in=w_in, ssd_conv_w=ssd_conv_w, ssd_conv_b=ssd_conv_b, dt_bias=dt_bias, a_log=a_log,
             d_skip=d_skip, ssd_norm_w=ssd_norm_w, q_norm_w=q_norm_w, k_norm_w=k_norm_w, attn_sinks=attn_sinks,
             w_out=w_out, norm2_w=norm2_w, w_up=w_up, ffn_conv_w=ffn_conv_w, ffn_conv_b=ffn_conv_b, w_down=w_down)
    depth = w_in.shape[0]
    b = x_prompt.shape[0]
    bs, tlen, _ = x_sample.shape
    assert cache_swa_k.shape[2] == WINDOW
    head_shape = (SSD_HEADS, HEAD_DIM, SSD_STATE)
    kv_shape = (ATT_KV_HEADS, HEAD_DIM)

    xp = x_prompt
    xs = _to_time_major(x_sample)
    ssm_all = state_ssm.reshape(depth, bs, SSD_INNER, SSD_STATE)
    kc_all = cache_swa_k.reshape(depth, bs, WINDOW, KV_WIDTH)
    vc_all = cache_swa_v.reshape(depth, bs, WINDOW, KV_WIDTH)
    s_outs = None
    p_states, s_states = [], []
    for i in range(depth):
        w = _prep_layer(p, i)
        sinks, n1, win, wdt, cw, cb, dtb, alog, dskip, snorm, qnw, knw, wout = w['mixer']

        x1, ssm, conv, kc, vc = _mixer_prompt(xp, *w['mixer'], tm=_TM_MIXER)
        xp, ffn_tail = _ffn_prompt(x1, *w['ffn'], tm=_TM_FFN)
        p_states.append((ssm.reshape((b,) + head_shape), conv, kc.reshape((b, WINDOW) + kv_shape),
                         vc.reshape((b, WINDOW) + kv_shape), ffn_tail))

        packed, conv_tail = _sample_in(xs, _to_time_major(state_ssd_conv[i]), n1, win, wdt, cw, cb, dtb)
        packed = _to_seq_major(packed, bs).reshape(bs * tlen, -1)
        y, *s_outs = _sample_mix(i, packed, sinks, ssm_all, kc_all, vc_all, alog, dskip, snorm, qnw, knw, s_outs,
                                 tlen=tlen)
        y = _to_time_major(y.reshape(bs, tlen, D_MODEL))
        xs, ffn_tail_s = _ffn_sample(xs, y, wout, _to_time_major(state_ffn_conv[i]), *w['ffn'])
        s_states.append((_to_seq_major(conv_tail, bs), _to_seq_major(ffn_tail_s, bs)))

    def stacked(states, j):
        return jnp.stack([st[j] for st in states])

    ssm_s, k_s, v_s = s_outs
    return (xp, _to_seq_major(xs, bs)) + tuple(stacked(p_states, j) for j in range(5)) + (
        ssm_s.reshape((depth, bs) + head_shape), stacked(s_states, 0),
        k_s.reshape((depth, bs, WINDOW) + kv_shape), v_s.reshape((depth, bs, WINDOW) + kv_shape),
        stacked(s_states, 1))
```

```python
import functools

import numpy as np
import jax
import jax.numpy as jnp
from jax import lax
from jax.experimental import pallas as pl
from jax.experimental.pallas import tpu as pltpu

F32 = jnp.float32
BF16 = jnp.bfloat16

D_MODEL = 1024
HEAD_DIM = 64
SSD_HEADS = 8
SSD_INNER = SSD_HEADS * HEAD_DIM
SSD_GROUPS = 2
SSD_STATE = 128
SSD_CONV = 4
SSD_CHUNK = 128
SSD_CONV_DIM = SSD_INNER + 2 * SSD_GROUPS * SSD_STATE
ATT_HEADS = 8
ATT_KV_HEADS = 2
ATT_GROUP = ATT_HEADS // ATT_KV_HEADS
ATT_WIDTH = ATT_HEADS * HEAD_DIM
KV_WIDTH = ATT_KV_HEADS * HEAD_DIM
WINDOW = 128
D_FF = 2816
FFN_CONV = 3
EPS = 1e-6
NEG = -1e30
LOG2E = 1.4426950408889634

LANES = 128
SUBLANES = 8
VMEM_LIMIT = 56 * 1024 * 1024

_Z0, _X0, _Q0, _K0, _V0, _PEND = 0, 512, 1536, 2048, 2176, 2304
ALIBI = tuple(float(2.0 ** (-8.0 * (h + 1) / ATT_HEADS)) for h in range(ATT_HEADS))


def _rms(x, w):
    ms = jnp.mean(x * x, axis=-1, keepdims=True)
    return x * lax.rsqrt(ms + EPS) * w


def _silu(x):
    return x * jax.nn.sigmoid(x)


def _softplus(x):
    return jnp.maximum(x, 0.0) + jnp.log(1.0 + jnp.exp(-jnp.abs(x)))


def _dot(a, b):
    return jnp.dot(a, b, preferred_element_type=F32)


def _dot_nt(a, b):
    return lax.dot_general(a, b, (((1,), (1,)), ((), ())), preferred_element_type=F32)


def _dot_tn(a, b):
    return lax.dot_general(a, b, (((0,), (0,)), ((), ())), preferred_element_type=F32)


def _const_spec(shape):
    nd = len(shape)
    return pl.BlockSpec(shape, lambda *_: (0,) * nd, pipeline_mode=pl.Buffered(1))


def _full_spec(shape):
    nd = len(shape)
    return pl.BlockSpec(shape, lambda *_: (0,) * nd)


_FFN_COLS = 256


def _ffn_kernel(*refs, tm, pad, shift, carry, out_proj):
    refs = list(refs)
    x_ref = refs.pop(0)
    if out_proj:
        y_ref = refs.pop(0)
        wout_ref = refs.pop(0)
    if not carry:
        prev_ref = refs.pop(0)
    n2_ref, wup_ref, cw_ref, cb_ref, wdn_ref, o_ref, tail_ref, hp_ref, act_ref = refs

    if carry:
        @pl.when(pl.program_id(1) == 0)
        def _():
            hp_ref[0:pad, :] = jnp.zeros((pad, 2 * D_FF), F32)
        x = x_ref[0]
    else:
        hp_ref[0:pad, :] = prev_ref[...]
        x = x_ref[...]

    if out_proj:
        x = x + _dot(y_ref[...].astype(BF16), wout_ref[...])
    hn = _rms(x, n2_ref[...]).astype(BF16)
    hp_ref[pad:pad + tm, :] = _dot(hn, wup_ref[...])

    for j in range(0, D_FF, _FFN_COLS):
        def conv(c0):
            cols = slice(c0, c0 + _FFN_COLS)
            acc = hp_ref[pad:pad + tm, cols] * cw_ref[2:3, cols]
            acc = acc + hp_ref[pad - shift:pad - shift + tm, cols] * cw_ref[1:2, cols]
            acc = acc + hp_ref[pad - 2 * shift:pad - 2 * shift + tm, cols] * cw_ref[0:1, cols]
            return acc + cb_ref[:, cols]
        act_ref[:, j:j + _FFN_COLS] = (_silu(conv(j)) * conv(D_FF + j)).astype(BF16)

    out = x + _dot(act_ref[...], wdn_ref[...])
    tail = hp_ref[tm + pad - 2 * shift:tm + pad, :]
    if carry:
        o_ref[0] = out
        tail_ref[0] = tail
        hp_ref[0:pad, :] = hp_ref[tm:tm + pad, :]
    else:
        o_ref[...] = out
        tail_ref[...] = tail


def _ffn_prompt(x, n2, wup, cw, cb, wdn, *, tm):
    b, s, _ = x.shape
    pad = SUBLANES
    kern = functools.partial(_ffn_kernel, tm=tm, pad=pad, shift=1, carry=True, out_proj=False)
    return pl.pallas_call(
        kern,
        grid=(b, s // tm),
        in_specs=[
            pl.BlockSpec((1, tm, D_MODEL), lambda i, t: (i, t, 0)),
            _const_spec((1, D_MODEL)),
            _const_spec((D_MODEL, 2 * D_FF)),
            _const_spec((FFN_CONV, 2 * D_FF)),
            _const_spec((1, 2 * D_FF)),
            _const_spec((D_FF, D_MODEL)),
        ],
        out_specs=[
            pl.BlockSpec((1, tm, D_MODEL), lambda i, t: (i, t, 0)),
            pl.BlockSpec((1, FFN_CONV - 1, 2 * D_FF), lambda i, t: (i, 0, 0)),
        ],
        out_shape=[
            jax.ShapeDtypeStruct((b, s, D_MODEL), F32),
            jax.ShapeDtypeStruct((b, FFN_CONV - 1, 2 * D_FF), F32),
        ],
        scratch_shapes=[
            pltpu.VMEM((tm + pad, 2 * D_FF), F32),
            pltpu.VMEM((tm, D_FF), BF16),
        ],
        compiler_params=pltpu.CompilerParams(
            dimension_semantics=("arbitrary", "arbitrary"), vmem_limit_bytes=VMEM_LIMIT),
        name="ffn_prompt",
    )(x, n2, wup, cw, cb, wdn)


def _ffn_sample(x, y, wout, prev, n2, wup, cw, cb, wdn):
    rows = x.shape[0]
    bs = prev.shape[0] // (FFN_CONV - 1)
    pad = (FFN_CONV - 1) * bs
    kern = functools.partial(_ffn_kernel, tm=rows, pad=pad, shift=bs, carry=False, out_proj=True)
    return pl.pallas_call(
        kern,
        grid=(1,),
        in_specs=[
            _const_spec((rows, D_MODEL)),
            _const_spec((rows, D_MODEL)),
            _const_spec((D_MODEL, D_MODEL)),
            _const_spec((pad, 2 * D_FF)),
            _const_spec((1, D_MODEL)),
            _const_spec((D_MODEL, 2 * D_FF)),
            _const_spec((FFN_CONV, 2 * D_FF)),
            _const_spec((1, 2 * D_FF)),
            _const_spec((D_FF, D_MODEL)),
        ],
        out_specs=[
            _full_spec((rows, D_MODEL)),
            _full_spec((pad, 2 * D_FF)),
        ],
        out_shape=[
            jax.ShapeDtypeStruct((rows, D_MODEL), F32),
            jax.ShapeDtypeStruct((pad, 2 * D_FF), F32),
        ],
        scratch_shapes=[
            pltpu.VMEM((rows + pad, 2 * D_FF), F32),
            pltpu.VMEM((rows, D_FF), BF16),
        ],
        compiler_params=pltpu.CompilerParams(
            dimension_semantics=("arbitrary",), vmem_limit_bytes=VMEM_LIMIT),
        name="ffn_sample",
    )(x, y, wout, prev, n2, wup, cw, cb, wdn)


def _tri_cumsum(x, ltri):
    hi = x.astype(BF16)
    lo = (x - hi.astype(F32)).astype(BF16)
    return _dot(ltri, hi) + _dot(ltri, lo)


def _half_variants(a, lane_lo):
    ar = pltpu.roll(a, HEAD_DIM, axis=1)
    zero = jnp.zeros_like(a)
    return {
        (0, 0): jnp.where(lane_lo, a, zero).astype(BF16),
        (1, 1): jnp.where(lane_lo, zero, a).astype(BF16),
        (1, 0): jnp.where(lane_lo, ar, zero).astype(BF16),
        (0, 1): jnp.where(lane_lo, zero, ar).astype(BF16),
    }


def _conv_silu(cbuf_ref, cw_ref, cb_ref, r0, cols):
    pad, c = SUBLANES, SSD_CHUNK
    assert SSD_CONV == 4
    xe = cbuf_ref[r0:r0 + pad + c, cols]
    x1 = pltpu.roll(xe, 1, axis=0)
    u = xe[pad:] * cw_ref[3:4, cols] + x1[pad:] * cw_ref[2:3, cols]
    v = xe * cw_ref[1:2, cols] + x1 * cw_ref[0:1, cols]
    return _silu(u + pltpu.roll(v, 2, axis=0)[pad:] + cb_ref[:, cols])


_CHUNK_WEIGHTS = (3.0, 3.0, 4.0) + (1.0,) * 4 + (1.5,) * 4 + (1.5,)


def _chunk_pieces(r0, bufs, kprev, vprev, first_mask, bias_ref, st_ref, sink_ref, cw_ref, cb_ref, a, dskip, snorm,
                  qnw, knw, ltri, causal, lane_lo):
    proj_ref, cbuf_ref, dt_ref, y_ref = bufs
    c = SSD_CHUNK
    rows = slice(r0, r0 + c)
    gw = SSD_INNER // SSD_GROUPS
    qo, ko, vo = _Q0 - SSD_CONV_DIM, _K0 - SSD_CONV_DIM, _V0 - SSD_CONV_DIM

    dt_c = dt_ref[rows, :]
    acum_c = _tri_cumsum(dt_c * a, ltri)
    bc = _conv_silu(cbuf_ref, cw_ref, cb_ref, r0, slice(SSD_INNER, SSD_CONV_DIM))
    bms = [bc[:, g * SSD_STATE:(g + 1) * SSD_STATE].astype(BF16) for g in range(SSD_GROUPS)]
    cms = [bc[:, (SSD_GROUPS + g) * SSD_STATE:(SSD_GROUPS + g + 1) * SSD_STATE].astype(BF16)
           for g in range(SSD_GROUPS)]
    cbs = [_dot_nt(cms[g], bms[g]) for g in range(SSD_GROUPS)]
    yoffs = [_dot(cms[g], st_ref[:, g * gw:(g + 1) * gw].astype(BF16)) for g in range(SSD_GROUPS)]
    yield

    q = proj_ref[rows, qo:qo + ATT_WIDTH]
    kn = _norm_k(proj_ref[rows, ko:ko + KV_WIDTH], knw, lane_lo)
    v = proj_ref[rows, vo:vo + KV_WIDTH]
    kcur = _half_variants(kn, lane_lo)
    vcur = _half_variants(v, lane_lo)
    qn = []
    for m in range(ATT_HEADS // 2):
        qp = q[:, m * LANES:(m + 1) * LANES]
        q2p = qp * qp
        rs_lo = lax.rsqrt(jnp.sum(jnp.where(lane_lo, q2p, 0.0), axis=-1, keepdims=True) / HEAD_DIM + EPS)
        rs_hi = lax.rsqrt(jnp.sum(jnp.where(lane_lo, 0.0, q2p), axis=-1, keepdims=True) / HEAD_DIM + EPS)
        qn.append((qp * qnw[:, m * LANES:(m + 1) * LANES]
                   * (jnp.where(lane_lo, rs_lo, rs_hi) * (HEAD_DIM ** -0.5 * LOG2E))).astype(BF16))
    scores = {}
    for kv in range(ATT_KV_HEADS):
        qs = jnp.concatenate([qn[2 * kv], qn[2 * kv + 1]], axis=0)
        for e in range(2):
            sr = _dot_nt(qs, jnp.concatenate([kprev[(kv, e)], kcur[(kv, e)]], axis=0))
            scores[ATT_GROUP * kv + e] = sr[0:c]
            scores[ATT_GROUP * kv + 2 + e] = sr[c:2 * c]
    vall = {key: jnp.concatenate([vprev[key], vcur[key]], axis=0) for key in vcur}
    yield

    acols = [jnp.broadcast_to(acum_c[:, h:h + 1], (c, LANES)) for h in range(SSD_HEADS)]
    dcols = [jnp.broadcast_to(dt_c[:, h:h + 1], (c, LANES)) for h in range(SSD_HEADS)]
    acum = jnp.concatenate([jnp.where(lane_lo, acols[2 * m], acols[2 * m + 1]) for m in range(SSD_HEADS // 2)],
                           axis=1)
    dt = jnp.concatenate([jnp.where(lane_lo, dcols[2 * m], dcols[2 * m + 1]) for m in range(SSD_HEADS // 2)],
                         axis=1)
    xs = _conv_silu(cbuf_ref, cw_ref, cb_ref, r0, slice(0, SSD_INNER))
    alast = acum[c - 1:c, :]
    xw = (xs * (jnp.exp(alast - acum) * dt)).astype(BF16)
    cdec = jnp.exp(alast)
    for g in range(SSD_GROUPS):
        st_ref[:, g * gw:(g + 1) * gw] = (st_ref[:, g * gw:(g + 1) * gw] * cdec[:, g * gw:(g + 1) * gw]
                                          + _dot_tn(bms[g], xw[:, g * gw:(g + 1) * gw]))
    yield
    eac = jnp.exp(acum)
    xdt = xs * dt
    acum_t = acum_c.T
    ys = []
    for pm in range(SSD_INNER // LANES):
        g, m = divmod(pm, gw // LANES)
        lanes = slice(pm * LANES, (pm + 1) * LANES)
        ms = []
        for e in range(2):
            h = 2 * pm + e
            rowb = jnp.broadcast_to(acum_t[h:h + 1, :], (c, c))
            dec = jnp.exp(jnp.where(causal, acols[h] - rowb, NEG))
            ms.append((cbs[g] * dec).astype(BF16))
        xd = xdt[:, lanes]
        rhs = jnp.concatenate([jnp.where(lane_lo, xd, 0.0), jnp.where(lane_lo, 0.0, xd)], axis=0).astype(BF16)
        yield
        ydiag = _dot(jnp.concatenate(ms, axis=1), rhs)
        ys.append(ydiag + yoffs[g][:, m * LANES:(m + 1) * LANES] * eac[:, lanes])

    outs = []
    for m in range(ATT_HEADS // 2):
        kv = (2 * m) // ATT_GROUP
        ps, invs = [], []
        for e in range(2):
            h = 2 * m + e
            s = scores[h] + bias_ref[h]
            if first_mask is not None:
                s = jnp.where(first_mask, s, NEG)
            sk = sink_ref[h] * LOG2E
            mx = jnp.maximum(jnp.max(s, axis=-1, keepdims=True), sk)
            p = jnp.exp2(s - mx)
            den = jnp.sum(p, axis=-1, keepdims=True) + jnp.exp2(sk - mx)
            ps.append(p.astype(BF16))
            invs.append(1.0 / den)
        yield
        o = _dot(jnp.concatenate(ps, axis=1), jnp.concatenate([vall[(kv, 0)], vall[(kv, 1)]], axis=0))
        outs.append(o * jnp.where(lane_lo, invs[0], invs[1]))

    y = (jnp.concatenate(ys, axis=1) + xs * dskip) * _silu(proj_ref[rows, _Z0:_Z0 + SSD_INNER])
    y_ref[rows, 0:SSD_INNER] = jnp.concatenate(
        [_rms(y[:, g * gw:(g + 1) * gw], snorm[:, g * gw:(g + 1) * gw]) for g in range(SSD_GROUPS)],
        axis=1).astype(BF16)
    y_ref[rows, SSD_INNER:] = jnp.concatenate(outs, axis=1).astype(BF16)
    yield
    return kn, v, kcur, vcur


def _norm_k(k, knw, lane_lo):
    k2 = k * k
    r0 = lax.rsqrt(jnp.sum(jnp.where(lane_lo, k2, 0.0), axis=-1, keepdims=True) / HEAD_DIM + EPS)
    r1 = lax.rsqrt(jnp.sum(jnp.where(lane_lo, 0.0, k2), axis=-1, keepdims=True) / HEAD_DIM + EPS)
    return k * jnp.where(lane_lo, r0, r1) * knw


def _weave(main, weights, side):
    done = 0
    total = float(sum(weights))
    acc = 0.0
    for k in range(len(weights) + 1):
        try:
            next(main)
        except StopIteration as stop:
            for thunk in side[done:]:
                thunk()
            return stop.value
        acc += weights[k]
        while done < len(side) and done * total < acc * len(side):
            side[done]()
            done += 1
    raise AssertionError("main generator has more pieces than declared")


_PROJ_COLS = 256


def _project_pieces(load_x, n1_ref, win_ref, wdt_ref, dtb_ref, bufs, tm):
    proj_ref, cbuf_ref, dt_ref, _ = bufs
    pad = SUBLANES
    box = {}

    def norm():
        box['hn'] = _rms(load_x(), n1_ref[...]).astype(BF16)

    def slab(c0, c1):
        def run():
            res = _dot(box['hn'], win_ref[:, c0:c1])
            if c0 < _X0:
                proj_ref[:, c0:c1] = res
            elif c0 < _Q0:
                cbuf_ref[pad:pad + tm, c0 - _X0:c1 - _X0] = res
            else:
                proj_ref[:, c0 - SSD_CONV_DIM:c1 - SSD_CONV_DIM] = res
        return run

    def step_sizes():
        dt_ref[...] = _softplus(_dot(box['hn'], wdt_ref[...]) + dtb_ref[...])

    slabs = []
    for seg0, seg1 in ((_Z0, _X0), (_X0, _Q0), (_Q0, _PEND)):
        for c0 in range(seg0, seg1, _PROJ_COLS):
            slabs.append(slab(c0, min(c0 + _PROJ_COLS, seg1)))
    return [norm] + slabs + [step_sizes]


def _outproj_pieces(load_x, load_y, wout_ref, store):
    def slab(c0, c1):
        def run():
            store(c0, c1, load_x(c0, c1) + _dot(load_y(), wout_ref[:, c0:c1]))
        return run
    return [slab(c0, c0 + 2 * LANES) for c0 in range(0, D_MODEL, 2 * LANES)]


def _score_bias(bias_ref):
    c = SSD_CHUNK
    qi = lax.broadcasted_iota(jnp.int32, (c, 2 * c), 0)
    sj = lax.broadcasted_iota(jnp.int32, (c, 2 * c), 1)
    reli = qi + c - sj
    rel = reli.astype(F32)
    in_window = (reli >= 0) & (reli < WINDOW)
    for h in range(ATT_HEADS):
        bias_ref[h] = jnp.where(in_window, (-ALIBI[h] * LOG2E) * rel, NEG)


def _mix_tile(bufs, first_col, kprev, vprev, st_ref, bias_ref, sink_ref, cw_ref, cb_ref, alog_ref, dskip_ref,
              snorm_ref, qnw_ref, knw_ref, tm):
    c = SSD_CHUNK
    row = lax.broadcasted_iota(jnp.int32, (c, c), 0)
    col = lax.broadcasted_iota(jnp.int32, (c, c), 1)
    causal = row >= col
    ltri = jnp.where(causal, 1.0, 0.0).astype(BF16)
    lane_lo = col < HEAD_DIM
    a = -jnp.exp(alog_ref[...])

    kn = v = None
    for ci in range(tm // c):
        first_mask = None
        if ci == 0 and first_col is not None:
            first_mask = lax.broadcasted_iota(jnp.int32, (c, 2 * c), 1) >= first_col
        kn, v, kprev, vprev = yield from _chunk_pieces(
            ci * c, bufs, kprev, vprev, first_mask, bias_ref, st_ref, sink_ref, cw_ref, cb_ref, a, dskip_ref[...],
            snorm_ref[...], qnw_ref[...], knw_ref[...], ltri, causal, lane_lo)
    return kn, v, kprev, vprev


_MIX_SEQS = 1
_SCRATCH_PER_SEQ = 11


def _round_robin(gens):
    results = [None] * len(gens)
    live = list(range(len(gens)))
    while live:
        for j in list(live):
            try:
                next(gens[j])
            except StopIteration as stop:
                results[j] = stop.value
                live.remove(j)
                continue
            yield
    return results


def _interleaved(lists):
    return [item for group in zip(*lists) for item in group]


def _mixer_prompt_kernel(sink_ref, xcur_ref, xnext_ref, n1_ref, win_ref, wdt_ref, cw_ref, cb_ref, dtb_ref, alog_ref,
                         dskip_ref, snorm_ref, qnw_ref, knw_ref, wout_ref,
                         o_ref, ssm_ref, conv_ref, kc_ref, vc_ref, *scratch, tm):
    i = pl.program_id(0)
    s = pl.program_id(1)
    ns = pl.num_programs(1)
    pad = SUBLANES
    nq = _MIX_SEQS
    bias_ref = scratch[-1]
    seqs = []
    for q in range(nq):
        r = scratch[q * _SCRATCH_PER_SEQ:(q + 1) * _SCRATCH_PER_SEQ]
        seqs.append(dict(a=tuple(r[0:4]), b=tuple(r[4:8]), st=r[8], kprev=r[9], vprev=r[10]))
    proj_args = (n1_ref, win_ref, wdt_ref, dtb_ref)

    def mix_args(q):
        return (seqs[q]['st'], bias_ref, sink_ref, cw_ref, cb_ref, alog_ref, dskip_ref, snorm_ref, qnw_ref, knw_ref,
                tm)

    @pl.when((i == 0) & (s == 0))
    def _():
        _score_bias(bias_ref)
        for q in range(nq):
            for piece in _project_pieces(lambda q=q: xcur_ref[q, 0:tm, :], *proj_args, seqs[q]['a'], tm):
                piece()

    @pl.when(s == 0)
    def _():
        for q in range(nq):
            seqs[q]['st'][...] = jnp.zeros(seqs[q]['st'].shape, F32)
            seqs[q]['a'][1][0:pad, :] = jnp.zeros((pad, SSD_CONV_DIM), F32)
            seqs[q]['kprev'][...] = jnp.zeros(seqs[q]['kprev'].shape, F32)
            seqs[q]['vprev'][...] = jnp.zeros(seqs[q]['vprev'].shape, F32)

    @pl.when(s > 0)
    def _():
        for q in range(nq):
            seqs[q]['a'][1][0:pad, :] = seqs[q]['b'][1][tm:tm + pad, :]

    lane_lo = lax.broadcasted_iota(jnp.int32, (SSD_CHUNK, LANES), 1) < HEAD_DIM
    first_col = jnp.where(s == 0, SSD_CHUNK, 0)
    weights = tuple(w for w in _CHUNK_WEIGHTS * (tm // SSD_CHUNK) for _ in range(nq))

    def outproj(q, r0, y_ref):
        def store(c0, c1, val):
            o_ref[q, r0:r0 + tm, c0:c1] = val
        return _outproj_pieces(lambda c0, c1: xcur_ref[q, r0:r0 + tm, c0:c1], lambda: y_ref[...], wout_ref, store)

    side = _interleaved([_project_pieces(lambda q=q: xcur_ref[q, tm:2 * tm, :], *proj_args, seqs[q]['b'], tm)
                         for q in range(nq)])
    res_a = _weave(_round_robin(
        [_mix_tile(seqs[q]['a'], first_col, _half_variants(seqs[q]['kprev'][...], lane_lo),
                   _half_variants(seqs[q]['vprev'][...], lane_lo), *mix_args(q)) for q in range(nq)]), weights, side)
    for q in range(nq):
        seqs[q]['b'][1][0:pad, :] = seqs[q]['a'][1][tm:tm + pad, :]
    side = _interleaved([outproj(q, 0, seqs[q]['a'][3])
                         + _project_pieces(lambda q=q: xnext_ref[q], *proj_args, seqs[q]['a'], tm) for q in range(nq)])
    res_b = _weave(_round_robin(
        [_mix_tile(seqs[q]['b'], None, res_a[q][2], res_a[q][3], *mix_args(q)) for q in range(nq)]), weights, side)
    for piece in _interleaved([outproj(q, tm, seqs[q]['b'][3]) for q in range(nq)]):
        piece()
    for q in range(nq):
        seqs[q]['kprev'][...] = res_b[q][0]
        seqs[q]['vprev'][...] = res_b[q][1]

    @pl.when(s == ns - 1)
    def _():
        for q in range(nq):
            ssm_ref[q] = seqs[q]['st'][...].T
            conv_ref[q] = seqs[q]['b'][1][tm + pad - (SSD_CONV - 1):tm + pad, :]
            kc_ref[q] = res_b[q][0]
            vc_ref[q] = res_b[q][1]


def _mixer_prompt(x, sinks, n1, win, wdt, cw, cb, dtb, alog, dskip, snorm, qnw, knw, wout, *, tm):
    b, s, _ = x.shape
    pw = _PEND - SSD_CONV_DIM
    nt = s // tm
    ns = nt // 2
    kern = functools.partial(_mixer_prompt_kernel, tm=tm)
    consts = [n1, win, wdt, cw, cb, dtb, alog, dskip, snorm, qnw, knw, wout]

    nq = _MIX_SEQS
    assert b % nq == 0 and nt % 2 == 0
    groups = b // nq

    def next_tile(i, t, *_):
        flat = jnp.minimum((i * ns + t) * 2 + 2, groups * nt - 1)
        return (flat // nt, flat % nt, 0)

    bufset = [
        pltpu.VMEM((tm, pw), F32),
        pltpu.VMEM((tm + SUBLANES, SSD_CONV_DIM), F32),
        pltpu.VMEM((tm, LANES), F32),
        pltpu.VMEM((tm, D_MODEL), BF16),
    ]
    per_seq = bufset + bufset + [
        pltpu.VMEM((SSD_STATE, SSD_INNER), F32),
        pltpu.VMEM((WINDOW, KV_WIDTH), F32),
        pltpu.VMEM((WINDOW, KV_WIDTH), F32),
    ]
    assert len(per_seq) == _SCRATCH_PER_SEQ
    return pl.pallas_call(
        kern,
        grid_spec=pltpu.PrefetchScalarGridSpec(
            num_scalar_prefetch=1,
            grid=(groups, ns),
            in_specs=[pl.BlockSpec((nq, 2 * tm, D_MODEL), lambda i, t, *_: (i, t, 0)),
                      pl.BlockSpec((nq, tm, D_MODEL), next_tile)]
            + [_const_spec(w.shape) for w in consts],
            out_specs=[
                pl.BlockSpec((nq, 2 * tm, D_MODEL), lambda i, t, *_: (i, t, 0)),
                pl.BlockSpec((nq, SSD_INNER, SSD_STATE), lambda i, t, *_: (i, 0, 0)),
                pl.BlockSpec((nq, SSD_CONV - 1, SSD_CONV_DIM), lambda i, t, *_: (i, 0, 0)),
                pl.BlockSpec((nq, WINDOW, KV_WIDTH), lambda i, t, *_: (i, 0, 0)),
                pl.BlockSpec((nq, WINDOW, KV_WIDTH), lambda i, t, *_: (i, 0, 0)),
            ],
            scratch_shapes=per_seq * nq + [pltpu.VMEM((ATT_HEADS, SSD_CHUNK, 2 * SSD_CHUNK), F32)],
        ),
        out_shape=[
            jax.ShapeDtypeStruct((b, s, D_MODEL), F32),
            jax.ShapeDtypeStruct((b, SSD_INNER, SSD_STATE), F32),
            jax.ShapeDtypeStruct((b, SSD_CONV - 1, SSD_CONV_DIM), F32),
            jax.ShapeDtypeStruct((b, WINDOW, KV_WIDTH), F32),
            jax.ShapeDtypeStruct((b, WINDOW, KV_WIDTH), F32),
        ],
        compiler_params=pltpu.CompilerParams(
            dimension_semantics=("arbitrary", "arbitrary"), vmem_limit_bytes=VMEM_LIMIT),
        name="mixer_prompt",
    )(sinks, x, x, *consts)


_S_Z0, _S_X0, _S_DT0, _S_Q0, _S_K0, _S_V0, _S_END = 0, 512, 1536, 2048, 2560, 2688, 2816


def _sample_in_kernel(x_ref, prev_ref, n1_ref, win_ref, wdt_ref, cw_ref, cb_ref, dtb_ref, o_ref, tail_ref, cbuf_ref,
                      *, rows, bs):
    pad = (SSD_CONV - 1) * bs
    hn = _rms(x_ref[...], n1_ref[...]).astype(BF16)
    o_ref[:, _S_Z0:_S_X0] = _dot(hn, win_ref[:, _Z0:_X0])
    cbuf_ref[0:pad, :] = prev_ref[...]
    cbuf_ref[pad:pad + rows, :] = _dot(hn, win_ref[:, _X0:_Q0])
    acc = cb_ref[...] + cbuf_ref[pad:pad + rows, :] * cw_ref[SSD_CONV - 1:SSD_CONV, :]
    for i in range(1, SSD_CONV):
        acc = acc + cbuf_ref[pad - i * bs:pad - i * bs + rows, :] * cw_ref[SSD_CONV - 1 - i:SSD_CONV - i, :]
    o_ref[:, _S_X0:_S_DT0] = _silu(acc)
    o_ref[:, _S_DT0:_S_Q0] = _softplus(_dot(hn, wdt_ref[...]) + dtb_ref[...])
    o_ref[:, _S_Q0:_S_END] = _dot(hn, win_ref[:, _Q0:_PEND])
    tail_ref[...] = cbuf_ref[rows:rows + pad, :]


def _sample_in(x, prev, n1, win, wdt, cw, cb, dtb):
    rows = x.shape[0]
    pad = prev.shape[0]
    bs = pad // (SSD_CONV - 1)
    args = (x, prev, n1, win, wdt, cw, cb, dtb)
    return pl.pallas_call(
        functools.partial(_sample_in_kernel, rows=rows, bs=bs),
        grid=(1,),
        in_specs=[_const_spec(a.shape) for a in args],
        out_specs=[_full_spec((rows, _S_END)), _full_spec((pad, SSD_CONV_DIM))],
        out_shape=[jax.ShapeDtypeStruct((rows, _S_END), F32), jax.ShapeDtypeStruct((pad, SSD_CONV_DIM), F32)],
        scratch_shapes=[pltpu.VMEM((rows + pad, SSD_CONV_DIM), F32)],
        compiler_params=pltpu.CompilerParams(dimension_semantics=("arbitrary",), vmem_limit_bytes=VMEM_LIMIT),
        name="sample_in",
    )(*args)


_SAMPLE_TILES = 4


def _sample_mix_kernel(sink_ref, p_ref, st_ref, kc_ref, vc_ref, alog_ref, dskip_ref, snorm_ref, qnw_ref, knw_ref,
                       *rest, tlen, layer, all_layers):
    y_ref, sto_ref, ko_ref, vo_ref = rest[-4:]
    if all_layers:
        for ref in (sto_ref, ko_ref, vo_ref):
            for other in range(ref.shape[0]):
                if other != layer:
                    ref[other] = jnp.zeros(ref.shape[1:], F32)
        sto_ref, ko_ref, vo_ref = sto_ref.at[layer], ko_ref.at[layer], vo_ref.at[layer]
    nseq = SUBLANES // tlen
    for j in range(_SAMPLE_TILES):
        rows = pl.ds(j * SUBLANES, SUBLANES)
        seqs = pl.ds(j * nseq, nseq)
        _sample_mix_tile(sink_ref, p_ref.at[rows], st_ref.at[seqs], kc_ref.at[seqs], vc_ref.at[seqs], alog_ref,
                         dskip_ref, snorm_ref, qnw_ref, knw_ref, y_ref.at[rows], sto_ref.at[seqs], ko_ref.at[seqs],
                         vo_ref.at[seqs], tlen)


def _sample_mix_tile(sink_ref, p_ref, st_ref, kc_ref, vc_ref, alog_ref, dskip_ref, snorm_ref, qnw_ref, knw_ref,
                     y_ref, sto_ref, ko_ref, vo_ref, tlen):
    nseq = SUBLANES // tlen
    w = WINDOW
    pk = p_ref[...]
    z = pk[:, _S_Z0:_S_X0]
    xs = pk[:, _S_X0:_S_X0 + SSD_INNER]
    dt = pk[:, _S_DT0:_S_Q0]
    q = pk[:, _S_Q0:_S_K0]
    k = pk[:, _S_K0:_S_V0]
    v = pk[:, _S_V0:_S_END]
    gw = SSD_INNER // SSD_GROUPS
    bms = [pk[:, _S_X0 + SSD_INNER + g * SSD_STATE:_S_X0 + SSD_INNER + (g + 1) * SSD_STATE]
           for g in range(SSD_GROUPS)]
    c0 = _S_X0 + SSD_INNER + SSD_GROUPS * SSD_STATE
    cms = [pk[:, c0 + g * SSD_STATE:c0 + (g + 1) * SSD_STATE] for g in range(SSD_GROUPS)]

    rowi = lax.broadcasted_iota(jnp.int32, (SUBLANES, SSD_INNER), 0)
    assert tlen & (tlen - 1) == 0 and SUBLANES % tlen == 0
    tshift = tlen.bit_length() - 1
    sshift = SUBLANES.bit_length() - 1
    tpos = rowi & (tlen - 1)
    seq = rowi >> tshift
    seq_g = lax.broadcasted_iota(jnp.int32, (SUBLANES, gw), 0) >> tshift
    lane_lo = lax.broadcasted_iota(jnp.int32, (SUBLANES, LANES), 1) < HEAD_DIM

    dta = dt * (-jnp.exp(alog_ref[...]))
    acum = dta
    for s in range(1, tlen):
        acum = acum + jnp.where(tpos >= s, pltpu.roll(dta, s, axis=0), 0.0)
    tot = acum[tlen - 1:tlen, :]
    for b in range(1, nseq):
        tot = jnp.where(seq == b, acum[(b + 1) * tlen - 1:(b + 1) * tlen, :], tot)
    eac = jnp.exp(acum)
    cdec = jnp.exp(tot)
    xdt = xs * dt
    xw = xs * (jnp.exp(tot - acum) * dt)

    ydiag = jnp.zeros((SUBLANES, SSD_INNER), F32)
    for s in range(tlen):
        xsh = pltpu.roll(xdt, s, axis=0) if s else xdt
        ash = pltpu.roll(acum, s, axis=0) if s else acum
        cbs = []
        for g in range(SSD_GROUPS):
            bsh = pltpu.roll(bms[g], s, axis=0) if s else bms[g]
            cbs.append(jnp.broadcast_to(jnp.sum(cms[g] * bsh, axis=-1, keepdims=True), (SUBLANES, gw)))
        term = jnp.concatenate(cbs, axis=1) * jnp.exp(acum - ash) * xsh
        ydiag = ydiag + jnp.where(tpos >= s, term, 0.0)

    yoff = []
    for g in range(SSD_GROUPS):
        cm = cms[g].astype(BF16)
        bm = bms[g].astype(BF16)
        yo = None
        for b in range(nseq):
            h0 = st_ref[b, g * gw:(g + 1) * gw, :]
            yb = _dot_nt(cm, h0.astype(BF16))
            yo = yb if yo is None else jnp.where(seq_g == b, yb, yo)
            xwb = jnp.where(seq_g == b, xw[:, g * gw:(g + 1) * gw], 0.0).astype(BF16)
            cseq = jnp.broadcast_to(cdec[b * tlen:b * tlen + 1, :], (SUBLANES, SSD_INNER))
            blocks = []
            for hh in range(gw // HEAD_DIM):
                pm = (g * gw + hh * HEAD_DIM) // LANES
                cpair = cseq[:, pm * LANES:(pm + 1) * LANES]
                croll = pltpu.roll(cpair, HEAD_DIM, axis=1)
                chead = jnp.where(lane_lo, cpair, croll) if hh % 2 == 0 else jnp.where(lane_lo, croll, cpair)
                blocks += [chead] * (HEAD_DIM // SUBLANES)
            sto_ref[b, g * gw:(g + 1) * gw, :] = h0 * jnp.concatenate(blocks, axis=0) + _dot_tn(xwb, bm)
        yoff.append(yo)
    y = (ydiag + jnp.concatenate(yoff, axis=1) * eac + xs * dskip_ref[...]) * _silu(z)
    y_ssd = jnp.concatenate(
        [_rms(y[:, g * gw:(g + 1) * gw], snorm_ref[:, g * gw:(g + 1) * gw]) for g in range(SSD_GROUPS)], axis=1)

    kn = _norm_k(k, knw_ref[...], lane_lo)
    for b in range(nseq):
        ko_ref[b, 0:w - tlen, :] = kc_ref[b, tlen:w, :]
        vo_ref[b, 0:w - tlen, :] = vc_ref[b, tlen:w, :]
        ko_ref[b, w - tlen:w, :] = kn[b * tlen:(b + 1) * tlen, :]
        vo_ref[b, w - tlen:w, :] = v[b * tlen:(b + 1) * tlen, :]
    qw = q * qnw_ref[...]
    q2 = q * q
    pieces = []
    for h in range(ATT_HEADS):
        kv = h // ATT_GROUP
        lanes = slice((h // 2) * LANES, (h // 2 + 1) * LANES)
        own = lane_lo if h % 2 == 0 else jnp.logical_not(lane_lo)
        ssq = jnp.sum(jnp.where(own, q2[:, lanes], 0.0), axis=-1, keepdims=True)
        qh = jnp.where(own, qw[:, lanes], 0.0) * (lax.rsqrt(ssq / HEAD_DIM + EPS) * (HEAD_DIM ** -0.5))
        pieces.append(qh if h % 2 == kv else pltpu.roll(qh, HEAD_DIM, axis=1))
    qrows = jnp.concatenate(pieces, axis=0).astype(BF16)

    nq = ATT_HEADS * SUBLANES
    ncol = w + SUBLANES
    ri = lax.broadcasted_iota(jnp.int32, (nq, ncol), 0)
    cj = lax.broadcasted_iota(jnp.int32, (nq, ncol), 1)
    rt = ri & (tlen - 1)
    rseq = (ri & (SUBLANES - 1)) >> tshift
    rhead = ri >> sshift
    nj = jnp.maximum(cj - w, 0)
    in_cache = cj < w
    rel = jnp.where(in_cache, rt + w - cj, rt - (nj & (tlen - 1))).astype(F32)
    ok_new = ((nj >> tshift) == rseq) & ((nj & (tlen - 1)) <= rt)
    ok = (in_cache & (cj > rt)) | (jnp.logical_not(in_cache) & ok_new)
    slope = jnp.zeros((nq, ncol), F32)
    for h in range(ATT_HEADS):
        slope = jnp.where(rhead == h, ALIBI[h], slope)
    rh1 = lax.broadcasted_iota(jnp.int32, (nq, 1), 0) >> sshift
    sk = jnp.zeros((nq, 1), F32)
    for h in range(ATT_HEADS):
        sk = jnp.where(rh1 == h, sink_ref[h], sk)
    bias = slope * rel
    rseq1 = (lax.broadcasted_iota(jnp.int32, (nq, LANES), 0) & (SUBLANES - 1)) >> tshift

    out = jnp.zeros((nq, LANES), F32)
    for b in range(nseq):
        kc = jnp.concatenate([kc_ref[b], kn], axis=0).astype(BF16)
        vc = jnp.concatenate([vc_ref[b], v], axis=0).astype(BF16)
        s = jnp.where(ok, _dot_nt(qrows, kc) - bias, NEG)
        mx = jnp.maximum(jnp.max(s, axis=-1, keepdims=True), sk)
        p = jnp.exp(s - mx)
        den = jnp.sum(p, axis=-1, keepdims=True) + jnp.exp(sk - mx)
        o = _dot(p.astype(BF16), vc) / den
        out = jnp.where(rseq1 == b, o, out)
    pairs = []
    for m in range(ATT_HEADS // 2):
        halves = []
        for e in range(2):
            h = 2 * m + e
            blk = out[h * SUBLANES:(h + 1) * SUBLANES, :]
            halves.append(blk if h // ATT_GROUP == e else pltpu.roll(blk, HEAD_DIM, axis=1))
        pairs.append(jnp.where(lane_lo, halves[0], halves[1]))
    y_ref[...] = jnp.concatenate([y_ssd] + pairs, axis=1)


def _sample_mix(layer, packed, sinks, st, kc, vc, alog, dskip, snorm, qnw, knw, prev_outs, *, tlen):
    rows = packed.shape[0]
    nseq = SUBLANES // tlen
    consts = [alog, dskip, snorm, qnw, knw]
    carried = [] if prev_outs is None else list(prev_outs)
    n_in = 1 + 4 + len(consts)

    step_rows = SUBLANES * _SAMPLE_TILES
    step_seqs = nseq * _SAMPLE_TILES
    assert rows % step_rows == 0

    def slab(shape):
        return pl.BlockSpec((None, step_seqs) + shape, lambda i, *_: (layer, i, 0, 0))

    def out_slab(shape):
        if prev_outs is None:
            return pl.BlockSpec((st.shape[0], step_seqs) + shape, lambda i, *_: (0, i, 0, 0))
        return slab(shape)

    return pl.pallas_call(
        functools.partial(_sample_mix_kernel, tlen=tlen, layer=layer, all_layers=prev_outs is None),
        grid_spec=pltpu.PrefetchScalarGridSpec(
            num_scalar_prefetch=1,
            grid=(rows // step_rows,),
            in_specs=[
                pl.BlockSpec((step_rows, _S_END), lambda i, *_: (i, 0)),
                slab((SSD_INNER, SSD_STATE)), slab((WINDOW, KV_WIDTH)), slab((WINDOW, KV_WIDTH)),
            ] + [_const_spec(a.shape) for a in consts] + [pl.BlockSpec(memory_space=pl.ANY)] * len(carried),
            out_specs=[
                pl.BlockSpec((step_rows, D_MODEL), lambda i, *_: (i, 0)),
                out_slab((SSD_INNER, SSD_STATE)), out_slab((WINDOW, KV_WIDTH)), out_slab((WINDOW, KV_WIDTH)),
            ],
        ),
        out_shape=[
            jax.ShapeDtypeStruct((rows, D_MODEL), F32),
            jax.ShapeDtypeStruct(st.shape, F32),
            jax.ShapeDtypeStruct(kc.shape, F32),
            jax.ShapeDtypeStruct(vc.shape, F32),
        ],
        input_output_aliases={n_in + j: 1 + j for j in range(len(carried))},
        compiler_params=pltpu.CompilerParams(dimension_semantics=("arbitrary",), vmem_limit_bytes=VMEM_LIMIT),
        name="sample_mix",
    )(sinks, packed, st, kc, vc, *consts, *carried)


def _prep_layer(p, i):
    w_in = p['w_in'][i]
    s0, s1, s2, s3, s4 = (SSD_INNER, SSD_INNER + SSD_CONV_DIM, SSD_INNER + SSD_CONV_DIM + SSD_HEADS,
                          SSD_INNER + SSD_CONV_DIM + SSD_HEADS + ATT_WIDTH,
                          SSD_INNER + SSD_CONV_DIM + SSD_HEADS + ATT_WIDTH + KV_WIDTH)
    win = jnp.concatenate([w_in[:, :s1], w_in[:, s2:]], axis=1).astype(BF16)
    wdt = jnp.repeat(w_in[:, s1:s2], HEAD_DIM, axis=1).astype(BF16)

    def per_head(v):
        return jnp.repeat(v, HEAD_DIM)[None, :]

    def lane_padded(v):
        return jnp.pad(v, [(0, 0)] * (v.ndim - 1) + [(0, LANES - v.shape[-1])])

    mixer = (
        p['attn_sinks'][i],
        p['norm1_w'][i][None, :], win, wdt, p['ssd_conv_w'][i], p['ssd_conv_b'][i][None, :],
        per_head(p['dt_bias'][i]), per_head(p['a_log'][i]), per_head(p['d_skip'][i]), p['ssd_norm_w'][i][None, :],
        jnp.tile(p['q_norm_w'][i], ATT_HEADS)[None, :], jnp.tile(p['k_norm_w'][i], ATT_KV_HEADS)[None, :],
        p['w_out'][i].astype(BF16),
    )
    compact = (lane_padded(w_in[:, s1:s2]).astype(BF16), lane_padded(p['dt_bias'][i])[None, :],
               lane_padded(p['a_log'][i])[None, :])
    ffn = (
        p['norm2_w'][i][None, :], p['w_up'][i].astype(BF16), p['ffn_conv_w'][i], p['ffn_conv_b'][i][None, :],
        p['w_down'][i].astype(BF16),
    )
    return {'mixer': mixer, 'compact': compact, 'ffn': ffn}


_TM_MIXER = 256
_TM_FFN = 512


def _to_time_major(a):
    bs, t, c = a.shape
    return a.transpose(1, 0, 2).reshape(t * bs, c)


def _to_seq_major(a, bs):
    return a.reshape(a.shape[0] // bs, bs, a.shape[1]).transpose(1, 0, 2)


def kernel(x_prompt, x_sample, state_ssm, state_ssd_conv, cache_swa_k, cache_swa_v, state_ffn_conv, norm1_w, w_in,
           ssd_conv_w, ssd_conv_b, dt_bias, a_log, d_skip, ssd_norm_w, q_norm_w, k_norm_w, attn_sinks, w_out, norm2_w,
           w_up, ffn_conv_w, ffn_conv_b, w_down):
    p = dict(norm1_w=norm1_w, w_in=w_in, ssd_conv_w=ssd_conv_w, ssd_conv_b=ssd_conv_b, dt_bias=dt_bias, a_log=a_log,
             d_skip=d_skip, ssd_norm_w=ssd_norm_w, q_norm_w=q_norm_w, k_norm_w=k_norm_w, attn_sinks=attn_sinks,
             w_out=w_out, norm2_w=norm2_w, w_up=w_up, ffn_conv_w=ffn_conv_w, ffn_conv_b=ffn_conv_b, w_down=w_down)
    depth = w_in.shape[0]
    b = x_prompt.shape[0]
    bs, tlen, _ = x_sample.shape
    assert cache_swa_k.shape[2] == WINDOW
    head_shape = (SSD_HEADS, HEAD_DIM, SSD_STATE)
    kv_shape = (ATT_KV_HEADS, HEAD_DIM)

    xp = x_prompt
    xs = _to_time_major(x_sample)
    ssm_all = state_ssm.reshape(depth, bs, SSD_INNER, SSD_STATE)
    kc_all = cache_swa_k.reshape(depth, bs, WINDOW, KV_WIDTH)
    vc_all = cache_swa_v.reshape(depth, bs, WINDOW, KV_WIDTH)
    s_outs = None
    p_states, s_states = [], []
    for i in range(depth):
        w = _prep_layer(p, i)
        sinks, n1, win, wdt, cw, cb, dtb, alog, dskip, snorm, qnw, knw, wout = w['mixer']

        wdt_c, dtb_c, alog_c = w['compact']
        x1, ssm, conv, kc, vc = _mixer_prompt(xp, sinks, n1, win, wdt_c, cw, cb, dtb_c, alog_c, dskip, snorm, qnw,
                                              knw, wout, tm=_TM_MIXER)
        xp, ffn_tail = _ffn_prompt(x1, *w['ffn'], tm=_TM_FFN)
        p_states.append((ssm.reshape((b,) + head_shape), conv, kc.reshape((b, WINDOW) + kv_shape),
                         vc.reshape((b, WINDOW) + kv_shape), ffn_tail))

        packed, conv_tail = _sample_in(xs, _to_time_major(state_ssd_conv[i]), n1, win, wdt, cw, cb, dtb)
        packed = _to_seq_major(packed, bs).reshape(bs * tlen, -1)
        y, *s_outs = _sample_mix(i, packed, sinks, ssm_all, kc_all, vc_all, alog, dskip, snorm, qnw, knw, s_outs,
                                 tlen=tlen)
        y = _to_time_major(y.reshape(bs, tlen, D_MODEL))
        xs, ffn_tail_s = _ffn_sample(xs, y, wout, _to_time_major(state_ffn_conv[i]), *w['ffn'])
        s_states.append((_to_seq_major(conv_tail, bs), _to_seq_major(ffn_tail_s, bs)))

    def stacked(states, j):
        return jnp.stack([st[j] for st in states])

    ssm_s, k_s, v_s = s_outs
    return (xp, _to_seq_major(xs, bs)) + tuple(stacked(p_states, j) for j in range(5)) + (
        ssm_s.reshape((depth, bs) + head_shape), stacked(s_states, 0),
        k_s.reshape((depth, bs, WINDOW) + kv_shape), v_s.reshape((depth, bs, WINDOW) + kv_shape),
        stacked(s_states, 1))
```

```python
import functools

import numpy as np
import jax
import jax.numpy as jnp
from jax import lax
from jax.experimental import pallas as pl
from jax.experimental.pallas import tpu as pltpu

F32 = jnp.float32
BF16 = jnp.bfloat16

D_MODEL = 1024
HEAD_DIM = 64
SSD_HEADS = 8
SSD_INNER = SSD_HEADS * HEAD_DIM
SSD_GROUPS = 2
SSD_STATE = 128
SSD_CONV = 4
SSD_CHUNK = 128
SSD_CONV_DIM = SSD_INNER + 2 * SSD_GROUPS * SSD_STATE
ATT_HEADS = 8
ATT_KV_HEADS = 2
ATT_GROUP = ATT_HEADS // ATT_KV_HEADS
ATT_WIDTH = ATT_HEADS * HEAD_DIM
KV_WIDTH = ATT_KV_HEADS * HEAD_DIM
WINDOW = 128
D_FF = 2816
FFN_CONV = 3
EPS = 1e-6
NEG = -1e30
LOG2E = 1.4426950408889634

LANES = 128
SUBLANES = 8
VMEM_LIMIT = 56 * 1024 * 1024

_Z0, _X0, _Q0, _K0, _V0, _PEND = 0, 512, 1536, 2048, 2176, 2304
ALIBI = tuple(float(2.0 ** (-8.0 * (h + 1) / ATT_HEADS)) for h in range(ATT_HEADS))


def _rms(x, w):
    ms = jnp.mean(x * x, axis=-1, keepdims=True)
    return x * lax.rsqrt(ms + EPS) * w


def _silu(x):
    return x * jax.nn.sigmoid(x)


def _softplus(x):
    return jnp.maximum(x, 0.0) + jnp.log(1.0 + jnp.exp(-jnp.abs(x)))


def _dot(a, b):
    return jnp.dot(a, b, preferred_element_type=F32)


def _dot_nt(a, b):
    return lax.dot_general(a, b, (((1,), (1,)), ((), ())), preferred_element_type=F32)


def _dot_tn(a, b):
    return lax.dot_general(a, b, (((0,), (0,)), ((), ())), preferred_element_type=F32)


def _const_spec(shape):
    nd = len(shape)
    return pl.BlockSpec(shape, lambda *_: (0,) * nd, pipeline_mode=pl.Buffered(1))


def _full_spec(shape):
    nd = len(shape)
    return pl.BlockSpec(shape, lambda *_: (0,) * nd)


_FFN_COLS = 256


def _ffn_kernel(*refs, tm, pad, shift, carry, out_proj):
    refs = list(refs)
    x_ref = refs.pop(0)
    if out_proj:
        y_ref = refs.pop(0)
        wout_ref = refs.pop(0)
    if not carry:
        prev_ref = refs.pop(0)
    n2_ref, wup_ref, cw_ref, cb_ref, wdn_ref, o_ref, tail_ref, hp_ref, act_ref = refs

    if carry:
        @pl.when(pl.program_id(1) == 0)
        def _():
            hp_ref[0:pad, :] = jnp.zeros((pad, 2 * D_FF), F32)
        x = x_ref[0]
    else:
        hp_ref[0:pad, :] = prev_ref[...]
        x = x_ref[...]

    if out_proj:
        x = x + _dot(y_ref[...].astype(BF16), wout_ref[...])
    hn = _rms(x, n2_ref[...]).astype(BF16)
    hp_ref[pad:pad + tm, :] = _dot(hn, wup_ref[...])

    for j in range(0, D_FF, _FFN_COLS):
        def conv(c0):
            cols = slice(c0, c0 + _FFN_COLS)
            acc = hp_ref[pad:pad + tm, cols] * cw_ref[2:3, cols]
            acc = acc + hp_ref[pad - shift:pad - shift + tm, cols] * cw_ref[1:2, cols]
            acc = acc + hp_ref[pad - 2 * shift:pad - 2 * shift + tm, cols] * cw_ref[0:1, cols]
            return acc + cb_ref[:, cols]
        act_ref[:, j:j + _FFN_COLS] = (_silu(conv(j)) * conv(D_FF + j)).astype(BF16)

    out = x + _dot(act_ref[...], wdn_ref[...])
    tail = hp_ref[tm + pad - 2 * shift:tm + pad, :]
    if carry:
        o_ref[0] = out
        tail_ref[0] = tail
        hp_ref[0:pad, :] = hp_ref[tm:tm + pad, :]
    else:
        o_ref[...] = out
        tail_ref[...] = tail


def _ffn_prompt(x, n2, wup, cw, cb, wdn, *, tm):
    b, s, _ = x.shape
    pad = SUBLANES
    kern = functools.partial(_ffn_kernel, tm=tm, pad=pad, shift=1, carry=True, out_proj=False)
    return pl.pallas_call(
        kern,
        grid=(b, s // tm),
        in_specs=[
            pl.BlockSpec((1, tm, D_MODEL), lambda i, t: (i, t, 0)),
            _const_spec((1, D_MODEL)),
            _const_spec((D_MODEL, 2 * D_FF)),
            _const_spec((FFN_CONV, 2 * D_FF)),
            _const_spec((1, 2 * D_FF)),
            _const_spec((D_FF, D_MODEL)),
        ],
        out_specs=[
            pl.BlockSpec((1, tm, D_MODEL), lambda i, t: (i, t, 0)),
            pl.BlockSpec((1, FFN_CONV - 1, 2 * D_FF), lambda i, t: (i, 0, 0)),
        ],
        out_shape=[
            jax.ShapeDtypeStruct((b, s, D_MODEL), F32),
            jax.ShapeDtypeStruct((b, FFN_CONV - 1, 2 * D_FF), F32),
        ],
        scratch_shapes=[
            pltpu.VMEM((tm + pad, 2 * D_FF), F32),
            pltpu.VMEM((tm, D_FF), BF16),
        ],
        compiler_params=pltpu.CompilerParams(
            dimension_semantics=("arbitrary", "arbitrary"), vmem_limit_bytes=VMEM_LIMIT),
        name="ffn_prompt",
    )(x, n2, wup, cw, cb, wdn)


def _ffn_sample(x, y, wout, prev, n2, wup, cw, cb, wdn):
    rows = x.shape[0]
    bs = prev.shape[0] // (FFN_CONV - 1)
    pad = (FFN_CONV - 1) * bs
    kern = functools.partial(_ffn_kernel, tm=rows, pad=pad, shift=bs, carry=False, out_proj=True)
    return pl.pallas_call(
        kern,
        grid=(1,),
        in_specs=[
            _const_spec((rows, D_MODEL)),
            _const_spec((rows, D_MODEL)),
            _const_spec((D_MODEL, D_MODEL)),
            _const_spec((pad, 2 * D_FF)),
            _const_spec((1, D_MODEL)),
            _const_spec((D_MODEL, 2 * D_FF)),
            _const_spec((FFN_CONV, 2 * D_FF)),
            _const_spec((1, 2 * D_FF)),
            _const_spec((D_FF, D_MODEL)),
        ],
        out_specs=[
            _full_spec((rows, D_MODEL)),
            _full_spec((pad, 2 * D_FF)),
        ],
        out_shape=[
            jax.ShapeDtypeStruct((rows, D_MODEL), F32),
            jax.ShapeDtypeStruct((pad, 2 * D_FF), F32),
        ],
        scratch_shapes=[
            pltpu.VMEM((rows + pad, 2 * D_FF), F32),
            pltpu.VMEM((rows, D_FF), BF16),
        ],
        compiler_params=pltpu.CompilerParams(
            dimension_semantics=("arbitrary",), vmem_limit_bytes=VMEM_LIMIT),
        name="ffn_sample",
    )(x, y, wout, prev, n2, wup, cw, cb, wdn)


def _tri_cumsum(x, ltri):
    hi = x.astype(BF16)
    lo = (x - hi.astype(F32)).astype(BF16)
    return _dot(ltri, hi) + _dot(ltri, lo)


def _half_variants(a, lane_lo):
    ar = pltpu.roll(a, HEAD_DIM, axis=1)
    zero = jnp.zeros_like(a)
    return {
        (0, 0): jnp.where(lane_lo, a, zero).astype(BF16),
        (1, 1): jnp.where(lane_lo, zero, a).astype(BF16),
        (1, 0): jnp.where(lane_lo, ar, zero).astype(BF16),
        (0, 1): jnp.where(lane_lo, zero, ar).astype(BF16),
    }


def _conv_silu(cbuf_ref, cw_ref, cb_ref, r0, cols):
    pad, c = SUBLANES, SSD_CHUNK
    assert SSD_CONV == 4
    xe = cbuf_ref[r0:r0 + pad + c, cols]
    x1 = pltpu.roll(xe, 1, axis=0)
    u = xe[pad:] * cw_ref[3:4, cols] + x1[pad:] * cw_ref[2:3, cols]
    v = xe * cw_ref[1:2, cols] + x1 * cw_ref[0:1, cols]
    return _silu(u + pltpu.roll(v, 2, axis=0)[pad:] + cb_ref[:, cols])


_CHUNK_WEIGHTS = (5.0, 6.0, 5.0) + (1.0,) * 9


def _chunk_pieces(r0, bufs, kprev, vprev, first_mask, bias_ref, st_ref, sink_ref, cw_ref, cb_ref, a, dskip, snorm,
                  qnw, knw, ltri, causal, lane_lo):
    proj_ref, cbuf_ref, dt_ref, y_ref = bufs
    c = SSD_CHUNK
    rows = slice(r0, r0 + c)
    gw = SSD_INNER // SSD_GROUPS
    qo, ko, vo = _Q0 - SSD_CONV_DIM, _K0 - SSD_CONV_DIM, _V0 - SSD_CONV_DIM

    dt_c = dt_ref[rows, :]
    acum_c = _tri_cumsum(dt_c * a, ltri)
    bc = _conv_silu(cbuf_ref, cw_ref, cb_ref, r0, slice(SSD_INNER, SSD_CONV_DIM))
    bms = [bc[:, g * SSD_STATE:(g + 1) * SSD_STATE].astype(BF16) for g in range(SSD_GROUPS)]
    cms = [bc[:, (SSD_GROUPS + g) * SSD_STATE:(SSD_GROUPS + g + 1) * SSD_STATE].astype(BF16)
           for g in range(SSD_GROUPS)]
    cbs = [_dot_nt(cms[g], bms[g]) for g in range(SSD_GROUPS)]
    yoffs = [_dot(cms[g], st_ref[:, g * gw:(g + 1) * gw].astype(BF16)) for g in range(SSD_GROUPS)]
    yield

    q = proj_ref[rows, qo:qo + ATT_WIDTH]
    kn = _norm_k(proj_ref[rows, ko:ko + KV_WIDTH], knw, lane_lo)
    v = proj_ref[rows, vo:vo + KV_WIDTH]
    kcur = _half_variants(kn, lane_lo)
    vcur = _half_variants(v, lane_lo)
    qn = []
    for m in range(ATT_HEADS // 2):
        qp = q[:, m * LANES:(m + 1) * LANES]
        q2p = qp * qp
        rs_lo = lax.rsqrt(jnp.sum(jnp.where(lane_lo, q2p, 0.0), axis=-1, keepdims=True) / HEAD_DIM + EPS)
        rs_hi = lax.rsqrt(jnp.sum(jnp.where(lane_lo, 0.0, q2p), axis=-1, keepdims=True) / HEAD_DIM + EPS)
        qn.append((qp * qnw[:, m * LANES:(m + 1) * LANES]
                   * (jnp.where(lane_lo, rs_lo, rs_hi) * (HEAD_DIM ** -0.5 * LOG2E))).astype(BF16))
    scores = {}
    for kv in range(ATT_KV_HEADS):
        qs = jnp.concatenate([qn[2 * kv], qn[2 * kv + 1]], axis=0)
        for e in range(2):
            sr = _dot_nt(qs, jnp.concatenate([kprev[(kv, e)], kcur[(kv, e)]], axis=0))
            scores[ATT_GROUP * kv + e] = sr[0:c]
            scores[ATT_GROUP * kv + 2 + e] = sr[c:2 * c]
    vall = {key: jnp.concatenate([vprev[key], vcur[key]], axis=0) for key in vcur}
    yield

    acols = [jnp.broadcast_to(acum_c[:, h:h + 1], (c, LANES)) for h in range(SSD_HEADS)]
    dcols = [jnp.broadcast_to(dt_c[:, h:h + 1], (c, LANES)) for h in range(SSD_HEADS)]
    acum = jnp.concatenate([jnp.where(lane_lo, acols[2 * m], acols[2 * m + 1]) for m in range(SSD_HEADS // 2)],
                           axis=1)
    dt = jnp.concatenate([jnp.where(lane_lo, dcols[2 * m], dcols[2 * m + 1]) for m in range(SSD_HEADS // 2)],
                         axis=1)
    xs = _conv_silu(cbuf_ref, cw_ref, cb_ref, r0, slice(0, SSD_INNER))
    alast = acum[c - 1:c, :]
    xw = (xs * (jnp.exp(alast - acum) * dt)).astype(BF16)
    cdec = jnp.exp(alast)
    for g in range(SSD_GROUPS):
        st_ref[:, g * gw:(g + 1) * gw] = (st_ref[:, g * gw:(g + 1) * gw] * cdec[:, g * gw:(g + 1) * gw]
                                          + _dot_tn(bms[g], xw[:, g * gw:(g + 1) * gw]))
    yield
    eac = jnp.exp(acum)
    xdt = xs * dt
    acum_t = acum_c.T
    ys = []
    for pm in range(SSD_INNER // LANES):
        g, m = divmod(pm, gw // LANES)
        lanes = slice(pm * LANES, (pm + 1) * LANES)
        ms = []
        for e in range(2):
            h = 2 * pm + e
            rowb = jnp.broadcast_to(acum_t[h:h + 1, :], (c, c))
            dec = jnp.exp(jnp.where(causal, acols[h] - rowb, NEG))
            ms.append((cbs[g] * dec).astype(BF16))
        xd = xdt[:, lanes]
        rhs = jnp.concatenate([jnp.where(lane_lo, xd, 0.0), jnp.where(lane_lo, 0.0, xd)], axis=0).astype(BF16)
        yield
        ydiag = _dot(jnp.concatenate(ms, axis=1), rhs)
        ys.append(ydiag + yoffs[g][:, m * LANES:(m + 1) * LANES] * eac[:, lanes])

    outs = []
    for m in range(ATT_HEADS // 2):
        kv = (2 * m) // ATT_GROUP
        ps, invs = [], []
        for e in range(2):
            h = 2 * m + e
            s = scores[h] + bias_ref[h]
            if first_mask is not None:
                s = jnp.where(first_mask, s, NEG)
            sk = sink_ref[h] * LOG2E
            mx = jnp.maximum(jnp.max(s, axis=-1, keepdims=True), sk)
            p = jnp.exp2(s - mx)
            den = jnp.sum(p, axis=-1, keepdims=True) + jnp.exp2(sk - mx)
            ps.append(p.astype(BF16))
            invs.append(1.0 / den)
        yield
        o = _dot(jnp.concatenate(ps, axis=1), jnp.concatenate([vall[(kv, 0)], vall[(kv, 1)]], axis=0))
        outs.append(o * jnp.where(lane_lo, invs[0], invs[1]))

    y = (jnp.concatenate(ys, axis=1) + xs * dskip) * _silu(proj_ref[rows, _Z0:_Z0 + SSD_INNER])
    y_ref[rows, 0:SSD_INNER] = jnp.concatenate(
        [_rms(y[:, g * gw:(g + 1) * gw], snorm[:, g * gw:(g + 1) * gw]) for g in range(SSD_GROUPS)],
        axis=1).astype(BF16)
    y_ref[rows, SSD_INNER:] = jnp.concatenate(outs, axis=1).astype(BF16)
    yield
    return kn, v, kcur, vcur


def _norm_k(k, knw, lane_lo):
    k2 = k * k
    r0 = lax.rsqrt(jnp.sum(jnp.where(lane_lo, k2, 0.0), axis=-1, keepdims=True) / HEAD_DIM + EPS)
    r1 = lax.rsqrt(jnp.sum(jnp.where(lane_lo, 0.0, k2), axis=-1, keepdims=True) / HEAD_DIM + EPS)
    return k * jnp.where(lane_lo, r0, r1) * knw


def _weave(main, weights, side):
    done = 0
    total = float(sum(weights))
    acc = 0.0
    for k in range(len(weights) + 1):
        try:
            next(main)
        except StopIteration as stop:
            for thunk in side[done:]:
                thunk()
            return stop.value
        acc += weights[k]
        while done < len(side) and done * total < acc * len(side):
            side[done]()
            done += 1
    raise AssertionError("main generator has more pieces than declared")


_PROJ_COLS = 256


def _project_pieces(load_x, n1_ref, win_ref, wdt_ref, dtb_ref, bufs, tm):
    proj_ref, cbuf_ref, dt_ref, _ = bufs
    pad = SUBLANES
    box = {}

    def norm():
        box['hn'] = _rms(load_x(), n1_ref[...]).astype(BF16)

    def slab(c0, c1):
        def run():
            res = _dot(box['hn'], win_ref[:, c0:c1])
            if c0 < _X0:
                proj_ref[:, c0:c1] = res
            elif c0 < _Q0:
                cbuf_ref[pad:pad + tm, c0 - _X0:c1 - _X0] = res
            else:
                proj_ref[:, c0 - SSD_CONV_DIM:c1 - SSD_CONV_DIM] = res
        return run

    def step_sizes():
        dt_ref[...] = _softplus(_dot(box['hn'], wdt_ref[...]) + dtb_ref[...])

    slabs = []
    for seg0, seg1 in ((_Z0, _X0), (_X0, _Q0), (_Q0, _PEND)):
        for c0 in range(seg0, seg1, _PROJ_COLS):
            slabs.append(slab(c0, min(c0 + _PROJ_COLS, seg1)))
    return [norm] + slabs + [step_sizes]


def _outproj_pieces(load_x, load_y, wout_ref, store):
    def slab(c0, c1):
        def run():
            store(c0, c1, load_x(c0, c1) + _dot(load_y(), wout_ref[:, c0:c1]))
        return run
    return [slab(c0, c0 + 2 * LANES) for c0 in range(0, D_MODEL, 2 * LANES)]


def _score_bias(bias_ref):
    c = SSD_CHUNK
    qi = lax.broadcasted_iota(jnp.int32, (c, 2 * c), 0)
    sj = lax.broadcasted_iota(jnp.int32, (c, 2 * c), 1)
    reli = qi + c - sj
    rel = reli.astype(F32)
    in_window = (reli >= 0) & (reli < WINDOW)
    for h in range(ATT_HEADS):
        bias_ref[h] = jnp.where(in_window, (-ALIBI[h] * LOG2E) * rel, NEG)


def _mix_tile(bufs, first_col, kprev, vprev, st_ref, bias_ref, sink_ref, cw_ref, cb_ref, alog_ref, dskip_ref,
              snorm_ref, qnw_ref, knw_ref, tm):
    c = SSD_CHUNK
    row = lax.broadcasted_iota(jnp.int32, (c, c), 0)
    col = lax.broadcasted_iota(jnp.int32, (c, c), 1)
    causal = row >= col
    ltri = jnp.where(causal, 1.0, 0.0).astype(BF16)
    lane_lo = col < HEAD_DIM
    a = -jnp.exp(alog_ref[...])

    kn = v = None
    for ci in range(tm // c):
        first_mask = None
        if ci == 0 and first_col is not None:
            first_mask = lax.broadcasted_iota(jnp.int32, (c, 2 * c), 1) >= first_col
        kn, v, kprev, vprev = yield from _chunk_pieces(
            ci * c, bufs, kprev, vprev, first_mask, bias_ref, st_ref, sink_ref, cw_ref, cb_ref, a, dskip_ref[...],
            snorm_ref[...], qnw_ref[...], knw_ref[...], ltri, causal, lane_lo)
    return kn, v, kprev, vprev


_MIX_SEQS = 1
_SCRATCH_PER_SEQ = 11


def _round_robin(gens):
    results = [None] * len(gens)
    live = list(range(len(gens)))
    while live:
        for j in list(live):
            try:
                next(gens[j])
            except StopIteration as stop:
                results[j] = stop.value
                live.remove(j)
                continue
            yield
    return results


def _interleaved(lists):
    return [item for group in zip(*lists) for item in group]


def _mixer_prompt_kernel(sink_ref, xcur_ref, xnext_ref, n1_ref, win_ref, wdt_ref, cw_ref, cb_ref, dtb_ref, alog_ref,
                         dskip_ref, snorm_ref, qnw_ref, knw_ref, wout_ref,
                         o_ref, ssm_ref, conv_ref, kc_ref, vc_ref, *scratch, tm):
    i = pl.program_id(0)
    s = pl.program_id(1)
    ns = pl.num_programs(1)
    pad = SUBLANES
    nq = _MIX_SEQS
    bias_ref = scratch[-1]
    seqs = []
    for q in range(nq):
        r = scratch[q * _SCRATCH_PER_SEQ:(q + 1) * _SCRATCH_PER_SEQ]
        seqs.append(dict(a=tuple(r[0:4]), b=tuple(r[4:8]), st=r[8], kprev=r[9], vprev=r[10]))
    proj_args = (n1_ref, win_ref, wdt_ref, dtb_ref)

    def mix_args(q):
        return (seqs[q]['st'], bias_ref, sink_ref, cw_ref, cb_ref, alog_ref, dskip_ref, snorm_ref, qnw_ref, knw_ref,
                tm)

    @pl.when((i == 0) & (s == 0))
    def _():
        _score_bias(bias_ref)
        for q in range(nq):
            for piece in _project_pieces(lambda q=q: xcur_ref[q, 0:tm, :], *proj_args, seqs[q]['a'], tm):
                piece()

    @pl.when(s == 0)
    def _():
        for q in range(nq):
            seqs[q]['st'][...] = jnp.zeros(seqs[q]['st'].shape, F32)
            seqs[q]['a'][1][0:pad, :] = jnp.zeros((pad, SSD_CONV_DIM), F32)
            seqs[q]['kprev'][...] = jnp.zeros(seqs[q]['kprev'].shape, F32)
            seqs[q]['vprev'][...] = jnp.zeros(seqs[q]['vprev'].shape, F32)

    @pl.when(s > 0)
    def _():
        for q in range(nq):
            seqs[q]['a'][1][0:pad, :] = seqs[q]['b'][1][tm:tm + pad, :]

    lane_lo = lax.broadcasted_iota(jnp.int32, (SSD_CHUNK, LANES), 1) < HEAD_DIM
    first_col = jnp.where(s == 0, SSD_CHUNK, 0)
    weights = tuple(w for w in _CHUNK_WEIGHTS * (tm // SSD_CHUNK) for _ in range(nq))

    def outproj(q, r0, y_ref):
        def store(c0, c1, val):
            o_ref[q, r0:r0 + tm, c0:c1] = val
        return _outproj_pieces(lambda c0, c1: xcur_ref[q, r0:r0 + tm, c0:c1], lambda: y_ref[...], wout_ref, store)

    side = _interleaved([_project_pieces(lambda q=q: xcur_ref[q, tm:2 * tm, :], *proj_args, seqs[q]['b'], tm)
                         for q in range(nq)])
    res_a = _weave(_round_robin(
        [_mix_tile(seqs[q]['a'], first_col, _half_variants(seqs[q]['kprev'][...], lane_lo),
                   _half_variants(seqs[q]['vprev'][...], lane_lo), *mix_args(q)) for q in range(nq)]), weights, side)
    for q in range(nq):
        seqs[q]['b'][1][0:pad, :] = seqs[q]['a'][1][tm:tm + pad, :]
    side = _interleaved([outproj(q, 0, seqs[q]['a'][3])
                         + _project_pieces(lambda q=q: xnext_ref[q], *proj_args, seqs[q]['a'], tm) for q in range(nq)])
    res_b = _weave(_round_robin(
        [_mix_tile(seqs[q]['b'], None, res_a[q][2], res_a[q][3], *mix_args(q)) for q in range(nq)]), weights, side)
    for piece in _interleaved([outproj(q, tm, seqs[q]['b'][3]) for q in range(nq)]):
        piece()
    for q in range(nq):
        seqs[q]['kprev'][...] = res_b[q][0]
        seqs[q]['vprev'][...] = res_b[q][1]

    @pl.when(s == ns - 1)
    def _():
        for q in range(nq):
            ssm_ref[q] = seqs[q]['st'][...].T
            conv_ref[q] = seqs[q]['b'][1][tm + pad - (SSD_CONV - 1):tm + pad, :]
            kc_ref[q] = res_b[q][0]
            vc_ref[q] = res_b[q][1]


def _mixer_prompt(x, sinks, n1, win, wdt, cw, cb, dtb, alog, dskip, snorm, qnw, knw, wout, *, tm):
    b, s, _ = x.shape
    pw = _PEND - SSD_CONV_DIM
    nt = s // tm
    ns = nt // 2
    kern = functools.partial(_mixer_prompt_kernel, tm=tm)
    consts = [n1, win, wdt, cw, cb, dtb, alog, dskip, snorm, qnw, knw, wout]

    nq = _MIX_SEQS
    assert b % nq == 0 and nt % 2 == 0
    groups = b // nq

    def next_tile(i, t, *_):
        flat = jnp.minimum((i * ns + t) * 2 + 2, groups * nt - 1)
        return (flat // nt, flat % nt, 0)

    bufset = [
        pltpu.VMEM((tm, pw), F32),
        pltpu.VMEM((tm + SUBLANES, SSD_CONV_DIM), F32),
        pltpu.VMEM((tm, LANES), F32),
        pltpu.VMEM((tm, D_MODEL), BF16),
    ]
    per_seq = bufset + bufset + [
        pltpu.VMEM((SSD_STATE, SSD_INNER), F32),
        pltpu.VMEM((WINDOW, KV_WIDTH), F32),
        pltpu.VMEM((WINDOW, KV_WIDTH), F32),
    ]
    assert len(per_seq) == _SCRATCH_PER_SEQ
    return pl.pallas_call(
        kern,
        grid_spec=pltpu.PrefetchScalarGridSpec(
            num_scalar_prefetch=1,
            grid=(groups, ns),
            in_specs=[pl.BlockSpec((nq, 2 * tm, D_MODEL), lambda i, t, *_: (i, t, 0)),
                      pl.BlockSpec((nq, tm, D_MODEL), next_tile)]
            + [_const_spec(w.shape) for w in consts],
            out_specs=[
                pl.BlockSpec((nq, 2 * tm, D_MODEL), lambda i, t, *_: (i, t, 0)),
                pl.BlockSpec((nq, SSD_INNER, SSD_STATE), lambda i, t, *_: (i, 0, 0)),
                pl.BlockSpec((nq, SSD_CONV - 1, SSD_CONV_DIM), lambda i, t, *_: (i, 0, 0)),
                pl.BlockSpec((nq, WINDOW, KV_WIDTH), lambda i, t, *_: (i, 0, 0)),
                pl.BlockSpec((nq, WINDOW, KV_WIDTH), lambda i, t, *_: (i, 0, 0)),
            ],
            scratch_shapes=per_seq * nq + [pltpu.VMEM((ATT_HEADS, SSD_CHUNK, 2 * SSD_CHUNK), F32)],
        ),
        out_shape=[
            jax.ShapeDtypeStruct((b, s, D_MODEL), F32),
            jax.ShapeDtypeStruct((b, SSD_INNER, SSD_STATE), F32),
            jax.ShapeDtypeStruct((b, SSD_CONV - 1, SSD_CONV_DIM), F32),
            jax.ShapeDtypeStruct((b, WINDOW, KV_WIDTH), F32),
            jax.ShapeDtypeStruct((b, WINDOW, KV_WIDTH), F32),
        ],
        compiler_params=pltpu.CompilerParams(
            dimension_semantics=("arbitrary", "arbitrary"), vmem_limit_bytes=VMEM_LIMIT),
        name="mixer_prompt",
    )(sinks, x, x, *consts)


_S_Z0, _S_X0, _S_DT0, _S_Q0, _S_K0, _S_V0, _S_END = 0, 512, 1536, 2048, 2560, 2688, 2816


def _sample_in_kernel(x_ref, prev_ref, n1_ref, win_ref, wdt_ref, cw_ref, cb_ref, dtb_ref, o_ref, tail_ref, cbuf_ref,
                      *, rows, bs):
    pad = (SSD_CONV - 1) * bs
    hn = _rms(x_ref[...], n1_ref[...]).astype(BF16)
    o_ref[:, _S_Z0:_S_X0] = _dot(hn, win_ref[:, _Z0:_X0])
    cbuf_ref[0:pad, :] = prev_ref[...]
    cbuf_ref[pad:pad + rows, :] = _dot(hn, win_ref[:, _X0:_Q0])
    acc = cb_ref[...] + cbuf_ref[pad:pad + rows, :] * cw_ref[SSD_CONV - 1:SSD_CONV, :]
    for i in range(1, SSD_CONV):
        acc = acc + cbuf_ref[pad - i * bs:pad - i * bs + rows, :] * cw_ref[SSD_CONV - 1 - i:SSD_CONV - i, :]
    o_ref[:, _S_X0:_S_DT0] = _silu(acc)
    o_ref[:, _S_DT0:_S_Q0] = _softplus(_dot(hn, wdt_ref[...]) + dtb_ref[...])
    o_ref[:, _S_Q0:_S_END] = _dot(hn, win_ref[:, _Q0:_PEND])
    tail_ref[...] = cbuf_ref[rows:rows + pad, :]


def _sample_in(x, prev, n1, win, wdt, cw, cb, dtb):
    rows = x.shape[0]
    pad = prev.shape[0]
    bs = pad // (SSD_CONV - 1)
    args = (x, prev, n1, win, wdt, cw, cb, dtb)
    return pl.pallas_call(
        functools.partial(_sample_in_kernel, rows=rows, bs=bs),
        grid=(1,),
        in_specs=[_const_spec(a.shape) for a in args],
        out_specs=[_full_spec((rows, _S_END)), _full_spec((pad, SSD_CONV_DIM))],
        out_shape=[jax.ShapeDtypeStruct((rows, _S_END), F32), jax.ShapeDtypeStruct((pad, SSD_CONV_DIM), F32)],
        scratch_shapes=[pltpu.VMEM((rows + pad, SSD_CONV_DIM), F32)],
        compiler_params=pltpu.CompilerParams(dimension_semantics=("arbitrary",), vmem_limit_bytes=VMEM_LIMIT),
        name="sample_in",
    )(*args)


_SAMPLE_TILES = 8


def _sample_mix_kernel(sink_ref, p_ref, st_ref, kc_ref, vc_ref, alog_ref, dskip_ref, snorm_ref, qnw_ref, knw_ref,
                       *rest, tlen, layer, all_layers):
    y_ref, sto_ref, ko_ref, vo_ref = rest[-4:]
    if all_layers:
        for ref in (sto_ref, ko_ref, vo_ref):
            for other in range(ref.shape[0]):
                if other != layer:
                    ref[other] = jnp.zeros(ref.shape[1:], F32)
        sto_ref, ko_ref, vo_ref = sto_ref.at[layer], ko_ref.at[layer], vo_ref.at[layer]
    nseq = SUBLANES // tlen
    for j in range(_SAMPLE_TILES):
        rows = pl.ds(j * SUBLANES, SUBLANES)
        seqs = pl.ds(j * nseq, nseq)
        _sample_mix_tile(sink_ref, p_ref.at[rows], st_ref.at[seqs], kc_ref.at[seqs], vc_ref.at[seqs], alog_ref,
                         dskip_ref, snorm_ref, qnw_ref, knw_ref, y_ref.at[rows], sto_ref.at[seqs], ko_ref.at[seqs],
                         vo_ref.at[seqs], tlen)


def _sample_mix_tile(sink_ref, p_ref, st_ref, kc_ref, vc_ref, alog_ref, dskip_ref, snorm_ref, qnw_ref, knw_ref,
                     y_ref, sto_ref, ko_ref, vo_ref, tlen):
    nseq = SUBLANES // tlen
    w = WINDOW
    pk = p_ref[...]
    z = pk[:, _S_Z0:_S_X0]
    xs = pk[:, _S_X0:_S_X0 + SSD_INNER]
    dt = pk[:, _S_DT0:_S_Q0]
    q = pk[:, _S_Q0:_S_K0]
    k = pk[:, _S_K0:_S_V0]
    v = pk[:, _S_V0:_S_END]
    gw = SSD_INNER // SSD_GROUPS
    bms = [pk[:, _S_X0 + SSD_INNER + g * SSD_STATE:_S_X0 + SSD_INNER + (g + 1) * SSD_STATE]
           for g in range(SSD_GROUPS)]
    c0 = _S_X0 + SSD_INNER + SSD_GROUPS * SSD_STATE
    cms = [pk[:, c0 + g * SSD_STATE:c0 + (g + 1) * SSD_STATE] for g in range(SSD_GROUPS)]

    rowi = lax.broadcasted_iota(jnp.int32, (SUBLANES, SSD_INNER), 0)
    assert tlen & (tlen - 1) == 0 and SUBLANES % tlen == 0
    tshift = tlen.bit_length() - 1
    sshift = SUBLANES.bit_length() - 1
    tpos = rowi & (tlen - 1)
    seq = rowi >> tshift
    seq_g = lax.broadcasted_iota(jnp.int32, (SUBLANES, gw), 0) >> tshift
    lane_lo = lax.broadcasted_iota(jnp.int32, (SUBLANES, LANES), 1) < HEAD_DIM

    dta = dt * (-jnp.exp(alog_ref[...]))
    acum = dta
    for s in range(1, tlen):
        acum = acum + jnp.where(tpos >= s, pltpu.roll(dta, s, axis=0), 0.0)
    tot = acum[tlen - 1:tlen, :]
    for b in range(1, nseq):
        tot = jnp.where(seq == b, acum[(b + 1) * tlen - 1:(b + 1) * tlen, :], tot)
    eac = jnp.exp(acum)
    cdec = jnp.exp(tot)
    xdt = xs * dt
    xw = xs * (jnp.exp(tot - acum) * dt)

    ydiag = jnp.zeros((SUBLANES, SSD_INNER), F32)
    for s in range(tlen):
        xsh = pltpu.roll(xdt, s, axis=0) if s else xdt
        ash = pltpu.roll(acum, s, axis=0) if s else acum
        cbs = []
        for g in range(SSD_GROUPS):
            bsh = pltpu.roll(bms[g], s, axis=0) if s else bms[g]
            cbs.append(jnp.broadcast_to(jnp.sum(cms[g] * bsh, axis=-1, keepdims=True), (SUBLANES, gw)))
        term = jnp.concatenate(cbs, axis=1) * jnp.exp(acum - ash) * xsh
        ydiag = ydiag + jnp.where(tpos >= s, term, 0.0)

    yoff = []
    for g in range(SSD_GROUPS):
        cm = cms[g].astype(BF16)
        bm = bms[g].astype(BF16)
        yo = None
        for b in range(nseq):
            h0 = st_ref[b, g * gw:(g + 1) * gw, :]
            yb = _dot_nt(cm, h0.astype(BF16))
            yo = yb if yo is None else jnp.where(seq_g == b, yb, yo)
            xwb = jnp.where(seq_g == b, xw[:, g * gw:(g + 1) * gw], 0.0).astype(BF16)
            cseq = jnp.broadcast_to(cdec[b * tlen:b * tlen + 1, :], (SUBLANES, SSD_INNER))
            blocks = []
            for hh in range(gw // HEAD_DIM):
                pm = (g * gw + hh * HEAD_DIM) // LANES
                cpair = cseq[:, pm * LANES:(pm + 1) * LANES]
                croll = pltpu.roll(cpair, HEAD_DIM, axis=1)
                chead = jnp.where(lane_lo, cpair, croll) if hh % 2 == 0 else jnp.where(lane_lo, croll, cpair)
                blocks += [chead] * (HEAD_DIM // SUBLANES)
            sto_ref[b, g * gw:(g + 1) * gw, :] = h0 * jnp.concatenate(blocks, axis=0) + _dot_tn(xwb, bm)
        yoff.append(yo)
    y = (ydiag + jnp.concatenate(yoff, axis=1) * eac + xs * dskip_ref[...]) * _silu(z)
    y_ssd = jnp.concatenate(
        [_rms(y[:, g * gw:(g + 1) * gw], snorm_ref[:, g * gw:(g + 1) * gw]) for g in range(SSD_GROUPS)], axis=1)

    kn = _norm_k(k, knw_ref[...], lane_lo)
    for b in range(nseq):
        ko_ref[b, 0:w - tlen, :] = kc_ref[b, tlen:w, :]
        vo_ref[b, 0:w - tlen, :] = vc_ref[b, tlen:w, :]
        ko_ref[b, w - tlen:w, :] = kn[b * tlen:(b + 1) * tlen, :]
        vo_ref[b, w - tlen:w, :] = v[b * tlen:(b + 1) * tlen, :]
    qw = q * qnw_ref[...]
    q2 = q * q
    pieces = []
    for h in range(ATT_HEADS):
        kv = h // ATT_GROUP
        lanes = slice((h // 2) * LANES, (h // 2 + 1) * LANES)
        own = lane_lo if h % 2 == 0 else jnp.logical_not(lane_lo)
        ssq = jnp.sum(jnp.where(own, q2[:, lanes], 0.0), axis=-1, keepdims=True)
        qh = jnp.where(own, qw[:, lanes], 0.0) * (lax.rsqrt(ssq / HEAD_DIM + EPS) * (HEAD_DIM ** -0.5))
        pieces.append(qh if h % 2 == kv else pltpu.roll(qh, HEAD_DIM, axis=1))
    qrows = jnp.concatenate(pieces, axis=0).astype(BF16)

    nq = ATT_HEADS * SUBLANES
    ncol = w + SUBLANES
    ri = lax.broadcasted_iota(jnp.int32, (nq, ncol), 0)
    cj = lax.broadcasted_iota(jnp.int32, (nq, ncol), 1)
    rt = ri & (tlen - 1)
    rseq = (ri & (SUBLANES - 1)) >> tshift
    rhead = ri >> sshift
    nj = jnp.maximum(cj - w, 0)
    in_cache = cj < w
    rel = jnp.where(in_cache, rt + w - cj, rt - (nj & (tlen - 1))).astype(F32)
    ok_new = ((nj >> tshift) == rseq) & ((nj & (tlen - 1)) <= rt)
    ok = (in_cache & (cj > rt)) | (jnp.logical_not(in_cache) & ok_new)
    slope = jnp.zeros((nq, ncol), F32)
    for h in range(ATT_HEADS):
        slope = jnp.where(rhead == h, ALIBI[h], slope)
    rh1 = lax.broadcasted_iota(jnp.int32, (nq, 1), 0) >> sshift
    sk = jnp.zeros((nq, 1), F32)
    for h in range(ATT_HEADS):
        sk = jnp.where(rh1 == h, sink_ref[h], sk)
    bias = slope * rel
    rseq1 = (lax.broadcasted_iota(jnp.int32, (nq, LANES), 0) & (SUBLANES - 1)) >> tshift

    out = jnp.zeros((nq, LANES), F32)
    for b in range(nseq):
        kc = jnp.concatenate([kc_ref[b], kn], axis=0).astype(BF16)
        vc = jnp.concatenate([vc_ref[b], v], axis=0).astype(BF16)
        s = jnp.where(ok, _dot_nt(qrows, kc) - bias, NEG)
        mx = jnp.maximum(jnp.max(s, axis=-1, keepdims=True), sk)
        p = jnp.exp(s - mx)
        den = jnp.sum(p, axis=-1, keepdims=True) + jnp.exp(sk - mx)
        o = _dot(p.astype(BF16), vc) / den
        out = jnp.where(rseq1 == b, o, out)
    pairs = []
    for m in range(ATT_HEADS // 2):
        halves = []
        for e in range(2):
            h = 2 * m + e
            blk = out[h * SUBLANES:(h + 1) * SUBLANES, :]
            halves.append(blk if h // ATT_GROUP == e else pltpu.roll(blk, HEAD_DIM, axis=1))
        pairs.append(jnp.where(lane_lo, halves[0], halves[1]))
    y_ref[...] = jnp.concatenate([y_ssd] + pairs, axis=1)


def _sample_mix(layer, packed, sinks, st, kc, vc, alog, dskip, snorm, qnw, knw, prev_outs, *, tlen):
    rows = packed.shape[0]
    nseq = SUBLANES // tlen
    consts = [alog, dskip, snorm, qnw, knw]
    carried = [] if prev_outs is None else list(prev_outs)
    n_in = 1 + 4 + len(consts)

    step_rows = SUBLANES * _SAMPLE_TILES
    step_seqs = nseq * _SAMPLE_TILES
    assert rows % step_rows == 0

    def slab(shape):
        return pl.BlockSpec((None, step_seqs) + shape, lambda i, *_: (layer, i, 0, 0))

    def out_slab(shape):
        if prev_outs is None:
            return pl.BlockSpec((st.shape[0], step_seqs) + shape, lambda i, *_: (0, i, 0, 0))
        return slab(shape)

    return pl.pallas_call(
        functools.partial(_sample_mix_kernel, tlen=tlen, layer=layer, all_layers=prev_outs is None),
        grid_spec=pltpu.PrefetchScalarGridSpec(
            num_scalar_prefetch=1,
            grid=(rows // step_rows,),
            in_specs=[
                pl.BlockSpec((step_rows, _S_END), lambda i, *_: (i, 0)),
                slab((SSD_INNER, SSD_STATE)), slab((WINDOW, KV_WIDTH)), slab((WINDOW, KV_WIDTH)),
            ] + [_const_spec(a.shape) for a in consts] + [pl.BlockSpec(memory_space=pl.ANY)] * len(carried),
            out_specs=[
                pl.BlockSpec((step_rows, D_MODEL), lambda i, *_: (i, 0)),
                out_slab((SSD_INNER, SSD_STATE)), out_slab((WINDOW, KV_WIDTH)), out_slab((WINDOW, KV_WIDTH)),
            ],
        ),
        out_shape=[
            jax.ShapeDtypeStruct((rows, D_MODEL), F32),
            jax.ShapeDtypeStruct(st.shape, F32),
            jax.ShapeDtypeStruct(kc.shape, F32),
            jax.ShapeDtypeStruct(vc.shape, F32),
        ],
        input_output_aliases={n_in + j: 1 + j for j in range(len(carried))},
        compiler_params=pltpu.CompilerParams(dimension_semantics=("arbitrary",), vmem_limit_bytes=VMEM_LIMIT),
        name="sample_mix",
    )(sinks, packed, st, kc, vc, *consts, *carried)


def _prep_layer(p, i):
    w_in = p['w_in'][i]
    s0, s1, s2, s3, s4 = (SSD_INNER, SSD_INNER + SSD_CONV_DIM, SSD_INNER + SSD_CONV_DIM + SSD_HEADS,
                          SSD_INNER + SSD_CONV_DIM + SSD_HEADS + ATT_WIDTH,
                          SSD_INNER + SSD_CONV_DIM + SSD_HEADS + ATT_WIDTH + KV_WIDTH)
    win = jnp.concatenate([w_in[:, :s1], w_in[:, s2:]], axis=1).astype(BF16)
    wdt = jnp.repeat(w_in[:, s1:s2], HEAD_DIM, axis=1).astype(BF16)

    def per_head(v):
        return jnp.repeat(v, HEAD_DIM)[None, :]

    def lane_padded(v):
        return jnp.pad(v, [(0, 0)] * (v.ndim - 1) + [(0, LANES - v.shape[-1])])

    mixer = (
        p['attn_sinks'][i],
        p['norm1_w'][i][None, :], win, wdt, p['ssd_conv_w'][i], p['ssd_conv_b'][i][None, :],
        per_head(p['dt_bias'][i]), per_head(p['a_log'][i]), per_head(p['d_skip'][i]), p['ssd_norm_w'][i][None, :],
        jnp.tile(p['q_norm_w'][i], ATT_HEADS)[None, :], jnp.tile(p['k_norm_w'][i], ATT_KV_HEADS)[None, :],
        p['w_out'][i].astype(BF16),
    )
    compact = (lane_padded(w_in[:, s1:s2]).astype(BF16), lane_padded(p['dt_bias'][i])[None, :],
               lane_padded(p['a_log'][i])[None, :])
    ffn = (
        p['norm2_w'][i][None, :], p['w_up'][i].astype(BF16), p['ffn_conv_w'][i], p['ffn_conv_b'][i][None, :],
        p['w_down'][i].astype(BF16),
    )
    return {'mixer': mixer, 'compact': compact, 'ffn': ffn}


_TM_MIXER = 256
_TM_FFN = 512


def _to_time_major(a):
    bs, t, c = a.shape
    return a.transpose(1, 0, 2).reshape(t * bs, c)


def _to_seq_major(a, bs):
    return a.reshape(a.shape[0] // bs, bs, a.shape[1]).transpose(1, 0, 2)


def kernel(x_prompt, x_sample, state_ssm, state_ssd_conv, cache_swa_k, cache_swa_v, state_ffn_conv, norm1_w, w_in,
           ssd_conv_w, ssd_conv_b, dt_bias, a_log, d_skip, ssd_norm_w, q_norm_w, k_norm_w, attn_sinks, w_out, norm2_w,
           w_up, ffn_conv_w, ffn_conv_b, w_down):
    p = dict(norm1_w=norm1_w, w_in=w_in, ssd_conv_w=ssd_conv_w, ssd_conv_b=ssd_conv_b, dt_bias=dt_bias, a_log=a_log,
             d_skip=d_skip, ssd_norm_w=ssd_norm_w, q_norm_w=q_norm_w, k_norm_w=k_norm_w, attn_sinks=attn_sinks,
             w_out=w_out, norm2_w=norm2_w, w_up=w_up, ffn_conv_w=ffn_conv_w, ffn_conv_b=ffn_conv_b, w_down=w_down)
    depth = w_in.shape[0]
    b = x_prompt.shape[0]
    bs, tlen, _ = x_sample.shape
    assert cache_swa_k.shape[2] == WINDOW
    head_shape = (SSD_HEADS, HEAD_DIM, SSD_STATE)
    kv_shape = (ATT_KV_HEADS, HEAD_DIM)

    xp = x_prompt
    xs = _to_time_major(x_sample)
    ssm_all = state_ssm.reshape(depth, bs, SSD_INNER, SSD_STATE)
    kc_all = cache_swa_k.reshape(depth, bs, WINDOW, KV_WIDTH)
    vc_all = cache_swa_v.reshape(depth, bs, WINDOW, KV_WIDTH)
    s_outs = None
    p_states, s_states = [], []
    for i in range(depth):
        w = _prep_layer(p, i)
        sinks, n1, win, wdt, cw, cb, dtb, alog, dskip, snorm, qnw, knw, wout = w['mixer']

        wdt_c, dtb_c, alog_c = w['compact']
        x1, ssm, conv, kc, vc = _mixer_prompt(xp, sinks, n1, win, wdt_c, cw, cb, dtb_c, alog_c, dskip, snorm, qnw,
                                              knw, wout, tm=_TM_MIXER)
        xp, ffn_tail = _ffn_prompt(x1, *w['ffn'], tm=_TM_FFN)
        p_states.append((ssm.reshape((b,) + head_shape), conv, kc.reshape((b, WINDOW) + kv_shape),
                         vc.reshape((b, WINDOW) + kv_shape), ffn_tail))

        packed, conv_tail = _sample_in(xs, _to_time_major(state_ssd_conv[i]), n1, win, wdt, cw, cb, dtb)
        packed = _to_seq_major(packed, bs).reshape(bs * tlen, -1)
        y, *s_outs = _sample_mix(i, packed, sinks, ssm_all, kc_all, vc_all, alog, dskip, snorm, qnw, knw, s_outs,
                                 tlen=tlen)
        y = _to_time_major(y.reshape(bs, tlen, D_MODEL))
        xs, ffn_tail_s = _ffn_sample(xs, y, wout, _to_time_major(state_ffn_conv[i]), *w['ffn'])
        s_states.append((_to_seq_major(conv_tail, bs), _to_seq_major(ffn_tail_s, bs)))

    def stacked(states, j):
        return jnp.stack([st[j] for st in states])

    ssm_s, k_s, v_s = s_outs
    return (xp, _to_seq_major(xs, bs)) + tuple(stacked(p_states, j) for j in range(5)) + (
        ssm_s.reshape((depth, bs) + head_shape), stacked(s_states, 0),
        k_s.reshape((depth, bs, WINDOW) + kv_shape), v_s.reshape((depth, bs, WINDOW) + kv_shape),
        stacked(s_states, 1))
```

```python
import functools

import numpy as np
import jax
import jax.numpy as jnp
from jax import lax
from jax.experimental import pallas as pl
from jax.experimental.pallas import tpu as pltpu

F32 = jnp.float32
BF16 = jnp.bfloat16

D_MODEL = 1024
HEAD_DIM = 64
SSD_HEADS = 8
SSD_INNER = SSD_HEADS * HEAD_DIM
SSD_GROUPS = 2
SSD_STATE = 128
SSD_CONV = 4
SSD_CHUNK = 128
SSD_CONV_DIM = SSD_INNER + 2 * SSD_GROUPS * SSD_STATE
ATT_HEADS = 8
ATT_KV_HEADS = 2
ATT_GROUP = ATT_HEADS // ATT_KV_HEADS
ATT_WIDTH = ATT_HEADS * HEAD_DIM
KV_WIDTH = ATT_KV_HEADS * HEAD_DIM
WINDOW = 128
D_FF = 2816
FFN_CONV = 3
EPS = 1e-6
NEG = -1e30
LOG2E = 1.4426950408889634

LANES = 128
SUBLANES = 8
VMEM_LIMIT = 56 * 1024 * 1024

_Z0, _X0, _Q0, _K0, _V0, _PEND = 0, 512, 1536, 2048, 2176, 2304
ALIBI = tuple(float(2.0 ** (-8.0 * (h + 1) / ATT_HEADS)) for h in range(ATT_HEADS))


def _rms(x, w):
    ms = jnp.mean(x * x, axis=-1, keepdims=True)
    return x * lax.rsqrt(ms + EPS) * w


def _silu(x):
    return x * jax.nn.sigmoid(x)


def _softplus(x):
    return jnp.maximum(x, 0.0) + jnp.log(1.0 + jnp.exp(-jnp.abs(x)))


def _dot(a, b):
    return jnp.dot(a, b, preferred_element_type=F32)


def _dot_nt(a, b):
    return lax.dot_general(a, b, (((1,), (1,)), ((), ())), preferred_element_type=F32)


def _dot_tn(a, b):
    return lax.dot_general(a, b, (((0,), (0,)), ((), ())), preferred_element_type=F32)


def _const_spec(shape):
    nd = len(shape)
    return pl.BlockSpec(shape, lambda *_: (0,) * nd, pipeline_mode=pl.Buffered(1))


def _full_spec(shape):
    nd = len(shape)
    return pl.BlockSpec(shape, lambda *_: (0,) * nd)


_FFN_COLS = 256


def _ffn_kernel(*refs, tm, pad, shift, carry, out_proj):
    refs = list(refs)
    x_ref = refs.pop(0)
    if out_proj:
        y_ref = refs.pop(0)
        wout_ref = refs.pop(0)
    if not carry:
        prev_ref = refs.pop(0)
    n2_ref, wup_ref, cw_ref, cb_ref, wdn_ref, o_ref, tail_ref, hp_ref, act_ref = refs

    if carry:
        @pl.when(pl.program_id(1) == 0)
        def _():
            hp_ref[0:pad, :] = jnp.zeros((pad, 2 * D_FF), F32)
        x = x_ref[0]
    else:
        hp_ref[0:pad, :] = prev_ref[...]
        x = x_ref[...]

    if out_proj:
        x = x + _dot(y_ref[...].astype(BF16), wout_ref[...])
    hn = _rms(x, n2_ref[...]).astype(BF16)
    hp_ref[pad:pad + tm, :] = _dot(hn, wup_ref[...])

    for j in range(0, D_FF, _FFN_COLS):
        def conv(c0):
            cols = slice(c0, c0 + _FFN_COLS)
            acc = hp_ref[pad:pad + tm, cols] * cw_ref[2:3, cols]
            acc = acc + hp_ref[pad - shift:pad - shift + tm, cols] * cw_ref[1:2, cols]
            acc = acc + hp_ref[pad - 2 * shift:pad - 2 * shift + tm, cols] * cw_ref[0:1, cols]
            return acc + cb_ref[:, cols]
        act_ref[:, j:j + _FFN_COLS] = (_silu(conv(j)) * conv(D_FF + j)).astype(BF16)

    out = x + _dot(act_ref[...], wdn_ref[...])
    tail = hp_ref[tm + pad - 2 * shift:tm + pad, :]
    if carry:
        o_ref[0] = out
        tail_ref[0] = tail
        hp_ref[0:pad, :] = hp_ref[tm:tm + pad, :]
    else:
        o_ref[...] = out
        tail_ref[...] = tail


def _ffn_prompt(x, n2, wup, cw, cb, wdn, *, tm):
    b, s, _ = x.shape
    pad = SUBLANES
    kern = functools.partial(_ffn_kernel, tm=tm, pad=pad, shift=1, carry=True, out_proj=False)
    return pl.pallas_call(
        kern,
        grid=(b, s // tm),
        in_specs=[
            pl.BlockSpec((1, tm, D_MODEL), lambda i, t: (i, t, 0)),
            _const_spec((1, D_MODEL)),
            _const_spec((D_MODEL, 2 * D_FF)),
            _const_spec((FFN_CONV, 2 * D_FF)),
            _const_spec((1, 2 * D_FF)),
            _const_spec((D_FF, D_MODEL)),
        ],
        out_specs=[
            pl.BlockSpec((1, tm, D_MODEL), lambda i, t: (i, t, 0)),
            pl.BlockSpec((1, FFN_CONV - 1, 2 * D_FF), lambda i, t: (i, 0, 0)),
        ],
        out_shape=[
            jax.ShapeDtypeStruct((b, s, D_MODEL), F32),
            jax.ShapeDtypeStruct((b, FFN_CONV - 1, 2 * D_FF), F32),
        ],
        scratch_shapes=[
            pltpu.VMEM((tm + pad, 2 * D_FF), F32),
            pltpu.VMEM((tm, D_FF), BF16),
        ],
        compiler_params=pltpu.CompilerParams(
            dimension_semantics=("arbitrary", "arbitrary"), vmem_limit_bytes=VMEM_LIMIT),
        name="ffn_prompt",
    )(x, n2, wup, cw, cb, wdn)


def _ffn_sample(x, y, wout, prev, n2, wup, cw, cb, wdn):
    rows = x.shape[0]
    bs = prev.shape[0] // (FFN_CONV - 1)
    pad = (FFN_CONV - 1) * bs
    kern = functools.partial(_ffn_kernel, tm=rows, pad=pad, shift=bs, carry=False, out_proj=True)
    return pl.pallas_call(
        kern,
        grid=(1,),
        in_specs=[
            _const_spec((rows, D_MODEL)),
            _const_spec((rows, D_MODEL)),
            _const_spec((D_MODEL, D_MODEL)),
            _const_spec((pad, 2 * D_FF)),
            _const_spec((1, D_MODEL)),
            _const_spec((D_MODEL, 2 * D_FF)),
            _const_spec((FFN_CONV, 2 * D_FF)),
            _const_spec((1, 2 * D_FF)),
            _const_spec((D_FF, D_MODEL)),
        ],
        out_specs=[
            _full_spec((rows, D_MODEL)),
            _full_spec((pad, 2 * D_FF)),
        ],
        out_shape=[
            jax.ShapeDtypeStruct((rows, D_MODEL), F32),
            jax.ShapeDtypeStruct((pad, 2 * D_FF), F32),
        ],
        scratch_shapes=[
            pltpu.VMEM((rows + pad, 2 * D_FF), F32),
            pltpu.VMEM((rows, D_FF), BF16),
        ],
        compiler_params=pltpu.CompilerParams(
            dimension_semantics=("arbitrary",), vmem_limit_bytes=VMEM_LIMIT),
        name="ffn_sample",
    )(x, y, wout, prev, n2, wup, cw, cb, wdn)


def _tri_cumsum(x, ltri):
    hi = x.astype(BF16)
    lo = (x - hi.astype(F32)).astype(BF16)
    return _dot(ltri, hi) + _dot(ltri, lo)


def _half_variants(a, lane_lo):
    ar = pltpu.roll(a, HEAD_DIM, axis=1)
    zero = jnp.zeros_like(a)
    return {
        (0, 0): jnp.where(lane_lo, a, zero).astype(BF16),
        (1, 1): jnp.where(lane_lo, zero, a).astype(BF16),
        (1, 0): jnp.where(lane_lo, ar, zero).astype(BF16),
        (0, 1): jnp.where(lane_lo, zero, ar).astype(BF16),
    }


def _conv_silu(cbuf_ref, cw_ref, cb_ref, r0, cols):
    pad, c = SUBLANES, SSD_CHUNK
    assert SSD_CONV == 4
    xe = cbuf_ref[r0:r0 + pad + c, cols]
    x1 = pltpu.roll(xe, 1, axis=0)
    u = xe[pad:] * cw_ref[3:4, cols] + x1[pad:] * cw_ref[2:3, cols]
    v = xe * cw_ref[1:2, cols] + x1 * cw_ref[0:1, cols]
    return _silu(u + pltpu.roll(v, 2, axis=0)[pad:] + cb_ref[:, cols])


_CHUNK_WEIGHTS = (5.0, 6.0, 5.0) + (1.5,) * 4 + (0.5,) * 4 + (1.0,)


def _chunk_pieces(r0, bufs, kprev, vprev, first_mask, bias_ref, st_ref, sink_ref, cw_ref, cb_ref, a, dskip, snorm,
                  qnw, knw, ltri, causal, lane_lo):
    proj_ref, cbuf_ref, dt_ref, y_ref = bufs
    c = SSD_CHUNK
    rows = slice(r0, r0 + c)
    gw = SSD_INNER // SSD_GROUPS
    qo, ko, vo = _Q0 - SSD_CONV_DIM, _K0 - SSD_CONV_DIM, _V0 - SSD_CONV_DIM

    dt_c = dt_ref[rows, :]
    acum_c = _tri_cumsum(dt_c * a, ltri)
    bc = _conv_silu(cbuf_ref, cw_ref, cb_ref, r0, slice(SSD_INNER, SSD_CONV_DIM))
    bms = [bc[:, g * SSD_STATE:(g + 1) * SSD_STATE].astype(BF16) for g in range(SSD_GROUPS)]
    cms = [bc[:, (SSD_GROUPS + g) * SSD_STATE:(SSD_GROUPS + g + 1) * SSD_STATE].astype(BF16)
           for g in range(SSD_GROUPS)]
    cbs = [_dot_nt(cms[g], bms[g]) for g in range(SSD_GROUPS)]
    yoffs = [_dot(cms[g], st_ref[:, g * gw:(g + 1) * gw].astype(BF16)) for g in range(SSD_GROUPS)]
    yield

    q = proj_ref[rows, qo:qo + ATT_WIDTH]
    kn = _norm_k(proj_ref[rows, ko:ko + KV_WIDTH], knw, lane_lo)
    v = proj_ref[rows, vo:vo + KV_WIDTH]
    kcur = _half_variants(kn, lane_lo)
    vcur = _half_variants(v, lane_lo)
    qn = []
    for m in range(ATT_HEADS // 2):
        qp = q[:, m * LANES:(m + 1) * LANES]
        q2p = qp * qp
        rs_lo = lax.rsqrt(jnp.sum(jnp.where(lane_lo, q2p, 0.0), axis=-1, keepdims=True) / HEAD_DIM + EPS)
        rs_hi = lax.rsqrt(jnp.sum(jnp.where(lane_lo, 0.0, q2p), axis=-1, keepdims=True) / HEAD_DIM + EPS)
        qn.append((qp * qnw[:, m * LANES:(m + 1) * LANES]
                   * (jnp.where(lane_lo, rs_lo, rs_hi) * (HEAD_DIM ** -0.5 * LOG2E))).astype(BF16))
    scores = {}
    for kv in range(ATT_KV_HEADS):
        qs = jnp.concatenate([qn[2 * kv], qn[2 * kv + 1]], axis=0)
        for e in range(2):
            sr = _dot_nt(qs, jnp.concatenate([kprev[(kv, e)], kcur[(kv, e)]], axis=0))
            scores[ATT_GROUP * kv + e] = sr[0:c]
            scores[ATT_GROUP * kv + 2 + e] = sr[c:2 * c]
    vall = {key: jnp.concatenate([vprev[key], vcur[key]], axis=0) for key in vcur}
    yield

    acols = [jnp.broadcast_to(acum_c[:, h:h + 1], (c, LANES)) for h in range(SSD_HEADS)]
    dcols = [jnp.broadcast_to(dt_c[:, h:h + 1], (c, LANES)) for h in range(SSD_HEADS)]
    acum = jnp.concatenate([jnp.where(lane_lo, acols[2 * m], acols[2 * m + 1]) for m in range(SSD_HEADS // 2)],
                           axis=1)
    dt = jnp.concatenate([jnp.where(lane_lo, dcols[2 * m], dcols[2 * m + 1]) for m in range(SSD_HEADS // 2)],
                         axis=1)
    xs = _conv_silu(cbuf_ref, cw_ref, cb_ref, r0, slice(0, SSD_INNER))
    alast = acum[c - 1:c, :]
    xw = (xs * (jnp.exp(alast - acum) * dt)).astype(BF16)
    cdec = jnp.exp(alast)
    for g in range(SSD_GROUPS):
        st_ref[:, g * gw:(g + 1) * gw] = (st_ref[:, g * gw:(g + 1) * gw] * cdec[:, g * gw:(g + 1) * gw]
                                          + _dot_tn(bms[g], xw[:, g * gw:(g + 1) * gw]))
    yield
    eac = jnp.exp(acum)
    xdt = xs * dt
    acum_t = acum_c.T
    ys = []
    for pm in range(SSD_INNER // LANES):
        g, m = divmod(pm, gw // LANES)
        lanes = slice(pm * LANES, (pm + 1) * LANES)
        ms = []
        for e in range(2):
            h = 2 * pm + e
            rowb = jnp.broadcast_to(acum_t[h:h + 1, :], (c, c))
            dec = jnp.exp(jnp.where(causal, acols[h] - rowb, NEG))
            ms.append((cbs[g] * dec).astype(BF16))
        xd = xdt[:, lanes]
        rhs = jnp.concatenate([jnp.where(lane_lo, xd, 0.0), jnp.where(lane_lo, 0.0, xd)], axis=0).astype(BF16)
        yield
        ydiag = _dot(jnp.concatenate(ms, axis=1), rhs)
        ys.append(ydiag + yoffs[g][:, m * LANES:(m + 1) * LANES] * eac[:, lanes])

    outs = []
    for m in range(ATT_HEADS // 2):
        kv = (2 * m) // ATT_GROUP
        ps, invs = [], []
        for e in range(2):
            h = 2 * m + e
            s = scores[h] + bias_ref[h]
            if first_mask is not None:
                s = jnp.where(first_mask, s, NEG)
            sk = sink_ref[h] * LOG2E
            mx = jnp.maximum(jnp.max(s, axis=-1, keepdims=True), sk)
            p = jnp.exp2(s - mx)
            den = jnp.sum(p, axis=-1, keepdims=True) + jnp.exp2(sk - mx)
            ps.append(p.astype(BF16))
            invs.append(1.0 / den)
        yield
        o = _dot(jnp.concatenate(ps, axis=1), jnp.concatenate([vall[(kv, 0)], vall[(kv, 1)]], axis=0))
        outs.append(o * jnp.where(lane_lo, invs[0], invs[1]))

    y = (jnp.concatenate(ys, axis=1) + xs * dskip) * _silu(proj_ref[rows, _Z0:_Z0 + SSD_INNER])
    y_ref[rows, 0:SSD_INNER] = jnp.concatenate(
        [_rms(y[:, g * gw:(g + 1) * gw], snorm[:, g * gw:(g + 1) * gw]) for g in range(SSD_GROUPS)],
        axis=1).astype(BF16)
    y_ref[rows, SSD_INNER:] = jnp.concatenate(outs, axis=1).astype(BF16)
    yield
    return kn, v, kcur, vcur


def _norm_k(k, knw, lane_lo):
    k2 = k * k
    r0 = lax.rsqrt(jnp.sum(jnp.where(lane_lo, k2, 0.0), axis=-1, keepdims=True) / HEAD_DIM + EPS)
    r1 = lax.rsqrt(jnp.sum(jnp.where(lane_lo, 0.0, k2), axis=-1, keepdims=True) / HEAD_DIM + EPS)
    return k * jnp.where(lane_lo, r0, r1) * knw


def _weave(main, weights, side):
    done = 0
    total = float(sum(weights))
    acc = 0.0
    for k in range(len(weights) + 1):
        try:
            next(main)
        except StopIteration as stop:
            for thunk in side[done:]:
                thunk()
            return stop.value
        acc += weights[k]
        while done < len(side) and done * total < acc * len(side):
            side[done]()
            done += 1
    raise AssertionError("main generator has more pieces than declared")


_PROJ_COLS = 256


def _project_pieces(load_x, n1_ref, win_ref, wdt_ref, dtb_ref, bufs, tm):
    proj_ref, cbuf_ref, dt_ref, _ = bufs
    pad = SUBLANES
    box = {}

    def norm():
        box['hn'] = _rms(load_x(), n1_ref[...]).astype(BF16)

    def slab(c0, c1):
        def run():
            res = _dot(box['hn'], win_ref[:, c0:c1])
            if c0 < _X0:
                proj_ref[:, c0:c1] = res
            elif c0 < _Q0:
                cbuf_ref[pad:pad + tm, c0 - _X0:c1 - _X0] = res
            else:
                proj_ref[:, c0 - SSD_CONV_DIM:c1 - SSD_CONV_DIM] = res
        return run

    def step_sizes():
        dt_ref[...] = _softplus(_dot(box['hn'], wdt_ref[...]) + dtb_ref[...])

    slabs = []
    for seg0, seg1 in ((_Z0, _X0), (_X0, _Q0), (_Q0, _PEND)):
        for c0 in range(seg0, seg1, _PROJ_COLS):
            slabs.append(slab(c0, min(c0 + _PROJ_COLS, seg1)))
    return [norm] + slabs + [step_sizes]


def _outproj_pieces(load_x, load_y, wout_ref, store):
    def slab(c0, c1):
        def run():
            store(c0, c1, load_x(c0, c1) + _dot(load_y(), wout_ref[:, c0:c1]))
        return run
    return [slab(c0, c0 + 2 * LANES) for c0 in range(0, D_MODEL, 2 * LANES)]


def _score_bias(bias_ref):
    c = SSD_CHUNK
    qi = lax.broadcasted_iota(jnp.int32, (c, 2 * c), 0)
    sj = lax.broadcasted_iota(jnp.int32, (c, 2 * c), 1)
    reli = qi + c - sj
    rel = reli.astype(F32)
    in_window = (reli >= 0) & (reli < WINDOW)
    for h in range(ATT_HEADS):
        bias_ref[h] = jnp.where(in_window, (-ALIBI[h] * LOG2E) * rel, NEG)


def _mix_tile(bufs, first_col, kprev, vprev, st_ref, bias_ref, sink_ref, cw_ref, cb_ref, alog_ref, dskip_ref,
              snorm_ref, qnw_ref, knw_ref, tm):
    c = SSD_CHUNK
    row = lax.broadcasted_iota(jnp.int32, (c, c), 0)
    col = lax.broadcasted_iota(jnp.int32, (c, c), 1)
    causal = row >= col
    ltri = jnp.where(causal, 1.0, 0.0).astype(BF16)
    lane_lo = col < HEAD_DIM
    a = -jnp.exp(alog_ref[...])

    kn = v = None
    for ci in range(tm // c):
        first_mask = None
        if ci == 0 and first_col is not None:
            first_mask = lax.broadcasted_iota(jnp.int32, (c, 2 * c), 1) >= first_col
        kn, v, kprev, vprev = yield from _chunk_pieces(
            ci * c, bufs, kprev, vprev, first_mask, bias_ref, st_ref, sink_ref, cw_ref, cb_ref, a, dskip_ref[...],
            snorm_ref[...], qnw_ref[...], knw_ref[...], ltri, causal, lane_lo)
    return kn, v, kprev, vprev


_MIX_SEQS = 2
_SCRATCH_PER_SEQ = 11


def _round_robin(gens):
    results = [None] * len(gens)
    live = list(range(len(gens)))
    while live:
        for j in list(live):
            try:
                next(gens[j])
            except StopIteration as stop:
                results[j] = stop.value
                live.remove(j)
                continue
            yield
    return results


def _interleaved(lists):
    return [item for group in zip(*lists) for item in group]


def _mixer_prompt_kernel(sink_ref, xcur_ref, xnext_ref, n1_ref, win_ref, wdt_ref, cw_ref, cb_ref, dtb_ref, alog_ref,
                         dskip_ref, snorm_ref, qnw_ref, knw_ref, wout_ref,
                         o_ref, ssm_ref, conv_ref, kc_ref, vc_ref, *scratch, tm):
    i = pl.program_id(0)
    s = pl.program_id(1)
    ns = pl.num_programs(1)
    pad = SUBLANES
    nq = _MIX_SEQS
    bias_ref = scratch[-1]
    seqs = []
    for q in range(nq):
        r = scratch[q * _SCRATCH_PER_SEQ:(q + 1) * _SCRATCH_PER_SEQ]
        seqs.append(dict(a=tuple(r[0:4]), b=tuple(r[4:8]), st=r[8], kprev=r[9], vprev=r[10]))
    proj_args = (n1_ref, win_ref, wdt_ref, dtb_ref)

    def mix_args(q):
        return (seqs[q]['st'], bias_ref, sink_ref, cw_ref, cb_ref, alog_ref, dskip_ref, snorm_ref, qnw_ref, knw_ref,
                tm)

    @pl.when((i == 0) & (s == 0))
    def _():
        _score_bias(bias_ref)
        for q in range(nq):
            for piece in _project_pieces(lambda q=q: xcur_ref[q, 0:tm, :], *proj_args, seqs[q]['a'], tm):
                piece()

    @pl.when(s == 0)
    def _():
        for q in range(nq):
            seqs[q]['st'][...] = jnp.zeros(seqs[q]['st'].shape, F32)
            seqs[q]['a'][1][0:pad, :] = jnp.zeros((pad, SSD_CONV_DIM), F32)
            seqs[q]['kprev'][...] = jnp.zeros(seqs[q]['kprev'].shape, F32)
            seqs[q]['vprev'][...] = jnp.zeros(seqs[q]['vprev'].shape, F32)

    @pl.when(s > 0)
    def _():
        for q in range(nq):
            seqs[q]['a'][1][0:pad, :] = seqs[q]['b'][1][tm:tm + pad, :]

    lane_lo = lax.broadcasted_iota(jnp.int32, (SSD_CHUNK, LANES), 1) < HEAD_DIM
    first_col = jnp.where(s == 0, SSD_CHUNK, 0)
    weights = tuple(w for w in _CHUNK_WEIGHTS * (tm // SSD_CHUNK) for _ in range(nq))

    def outproj(q, r0, y_ref):
        def store(c0, c1, val):
            o_ref[q, r0:r0 + tm, c0:c1] = val
        return _outproj_pieces(lambda c0, c1: xcur_ref[q, r0:r0 + tm, c0:c1], lambda: y_ref[...], wout_ref, store)

    side = _interleaved([_project_pieces(lambda q=q: xcur_ref[q, tm:2 * tm, :], *proj_args, seqs[q]['b'], tm)
                         for q in range(nq)])
    res_a = _weave(_round_robin(
        [_mix_tile(seqs[q]['a'], first_col, _half_variants(seqs[q]['kprev'][...], lane_lo),
                   _half_variants(seqs[q]['vprev'][...], lane_lo), *mix_args(q)) for q in range(nq)]), weights, side)
    for q in range(nq):
        seqs[q]['b'][1][0:pad, :] = seqs[q]['a'][1][tm:tm + pad, :]
    side = _interleaved([outproj(q, 0, seqs[q]['a'][3])
                         + _project_pieces(lambda q=q: xnext_ref[q], *proj_args, seqs[q]['a'], tm) for q in range(nq)])
    res_b = _weave(_round_robin(
        [_mix_tile(seqs[q]['b'], None, res_a[q][2], res_a[q][3], *mix_args(q)) for q in range(nq)]), weights, side)
    for piece in _interleaved([outproj(q, tm, seqs[q]['b'][3]) for q in range(nq)]):
        piece()
    for q in range(nq):
        seqs[q]['kprev'][...] = res_b[q][0]
        seqs[q]['vprev'][...] = res_b[q][1]

    @pl.when(s == ns - 1)
    def _():
        for q in range(nq):
            ssm_ref[q] = seqs[q]['st'][...].T
            conv_ref[q] = seqs[q]['b'][1][tm + pad - (SSD_CONV - 1):tm + pad, :]
            kc_ref[q] = res_b[q][0]
            vc_ref[q] = res_b[q][1]


def _mixer_prompt(x, sinks, n1, win, wdt, cw, cb, dtb, alog, dskip, snorm, qnw, knw, wout, *, tm):
    b, s, _ = x.shape
    pw = _PEND - SSD_CONV_DIM
    nt = s // tm
    ns = nt // 2
    kern = functools.partial(_mixer_prompt_kernel, tm=tm)
    consts = [n1, win, wdt, cw, cb, dtb, alog, dskip, snorm, qnw, knw, wout]

    nq = _MIX_SEQS
    assert b % nq == 0 and nt % 2 == 0
    groups = b // nq

    def next_tile(i, t, *_):
        flat = jnp.minimum((i * ns + t) * 2 + 2, groups * nt - 1)
        return (flat // nt, flat % nt, 0)

    bufset = [
        pltpu.VMEM((tm, pw), F32),
        pltpu.VMEM((tm + SUBLANES, SSD_CONV_DIM), F32),
        pltpu.VMEM((tm, LANES), F32),
        pltpu.VMEM((tm, D_MODEL), BF16),
    ]
    per_seq = bufset + bufset + [
        pltpu.VMEM((SSD_STATE, SSD_INNER), F32),
        pltpu.VMEM((WINDOW, KV_WIDTH), F32),
        pltpu.VMEM((WINDOW, KV_WIDTH), F32),
    ]
    assert len(per_seq) == _SCRATCH_PER_SEQ
    return pl.pallas_call(
        kern,
        grid_spec=pltpu.PrefetchScalarGridSpec(
            num_scalar_prefetch=1,
            grid=(groups, ns),
            in_specs=[pl.BlockSpec((nq, 2 * tm, D_MODEL), lambda i, t, *_: (i, t, 0)),
                      pl.BlockSpec((nq, tm, D_MODEL), next_tile)]
            + [_const_spec(w.shape) for w in consts],
            out_specs=[
                pl.BlockSpec((nq, 2 * tm, D_MODEL), lambda i, t, *_: (i, t, 0)),
                pl.BlockSpec((nq, SSD_INNER, SSD_STATE), lambda i, t, *_: (i, 0, 0)),
                pl.BlockSpec((nq, SSD_CONV - 1, SSD_CONV_DIM), lambda i, t, *_: (i, 0, 0)),
                pl.BlockSpec((nq, WINDOW, KV_WIDTH), lambda i, t, *_: (i, 0, 0)),
                pl.BlockSpec((nq, WINDOW, KV_WIDTH), lambda i, t, *_: (i, 0, 0)),
            ],
            scratch_shapes=per_seq * nq + [pltpu.VMEM((ATT_HEADS, SSD_CHUNK, 2 * SSD_CHUNK), F32)],
        ),
        out_shape=[
            jax.ShapeDtypeStruct((b, s, D_MODEL), F32),
            jax.ShapeDtypeStruct((b, SSD_INNER, SSD_STATE), F32),
            jax.ShapeDtypeStruct((b, SSD_CONV - 1, SSD_CONV_DIM), F32),
            jax.ShapeDtypeStruct((b, WINDOW, KV_WIDTH), F32),
            jax.ShapeDtypeStruct((b, WINDOW, KV_WIDTH), F32),
        ],
        compiler_params=pltpu.CompilerParams(
            dimension_semantics=("arbitrary", "arbitrary"), vmem_limit_bytes=VMEM_LIMIT),
        name="mixer_prompt",
    )(sinks, x, x, *consts)


_S_Z0, _S_X0, _S_DT0, _S_Q0, _S_K0, _S_V0, _S_END = 0, 512, 1536, 2048, 2560, 2688, 2816


def _sample_in_kernel(x_ref, prev_ref, n1_ref, win_ref, wdt_ref, cw_ref, cb_ref, dtb_ref, o_ref, tail_ref, cbuf_ref,
                      *, rows, bs):
    pad = (SSD_CONV - 1) * bs
    hn = _rms(x_ref[...], n1_ref[...]).astype(BF16)
    o_ref[:, _S_Z0:_S_X0] = _dot(hn, win_ref[:, _Z0:_X0])
    cbuf_ref[0:pad, :] = prev_ref[...]
    cbuf_ref[pad:pad + rows, :] = _dot(hn, win_ref[:, _X0:_Q0])
    acc = cb_ref[...] + cbuf_ref[pad:pad + rows, :] * cw_ref[SSD_CONV - 1:SSD_CONV, :]
    for i in range(1, SSD_CONV):
        acc = acc + cbuf_ref[pad - i * bs:pad - i * bs + rows, :] * cw_ref[SSD_CONV - 1 - i:SSD_CONV - i, :]
    o_ref[:, _S_X0:_S_DT0] = _silu(acc)
    o_ref[:, _S_DT0:_S_Q0] = _softplus(_dot(hn, wdt_ref[...]) + dtb_ref[...])
    o_ref[:, _S_Q0:_S_END] = _dot(hn, win_ref[:, _Q0:_PEND])
    tail_ref[...] = cbuf_ref[rows:rows + pad, :]


def _sample_in(x, prev, n1, win, wdt, cw, cb, dtb):
    rows = x.shape[0]
    pad = prev.shape[0]
    bs = pad // (SSD_CONV - 1)
    args = (x, prev, n1, win, wdt, cw, cb, dtb)
    return pl.pallas_call(
        functools.partial(_sample_in_kernel, rows=rows, bs=bs),
        grid=(1,),
        in_specs=[_const_spec(a.shape) for a in args],
        out_specs=[_full_spec((rows, _S_END)), _full_spec((pad, SSD_CONV_DIM))],
        out_shape=[jax.ShapeDtypeStruct((rows, _S_END), F32), jax.ShapeDtypeStruct((pad, SSD_CONV_DIM), F32)],
        scratch_shapes=[pltpu.VMEM((rows + pad, SSD_CONV_DIM), F32)],
        compiler_params=pltpu.CompilerParams(dimension_semantics=("arbitrary",), vmem_limit_bytes=VMEM_LIMIT),
        name="sample_in",
    )(*args)


_SAMPLE_TILES = 8


def _sample_mix_kernel(sink_ref, p_ref, st_ref, kc_ref, vc_ref, alog_ref, dskip_ref, snorm_ref, qnw_ref, knw_ref,
                       *rest, tlen, layer, all_layers):
    y_ref, sto_ref, ko_ref, vo_ref = rest[-4:]
    if all_layers:
        for ref in (sto_ref, ko_ref, vo_ref):
            for other in range(ref.shape[0]):
                if other != layer:
                    ref[other] = jnp.zeros(ref.shape[1:], F32)
        sto_ref, ko_ref, vo_ref = sto_ref.at[layer], ko_ref.at[layer], vo_ref.at[layer]
    nseq = SUBLANES // tlen
    for j in range(_SAMPLE_TILES):
        rows = pl.ds(j * SUBLANES, SUBLANES)
        seqs = pl.ds(j * nseq, nseq)
        _sample_mix_tile(sink_ref, p_ref.at[rows], st_ref.at[seqs], kc_ref.at[seqs], vc_ref.at[seqs], alog_ref,
                         dskip_ref, snorm_ref, qnw_ref, knw_ref, y_ref.at[rows], sto_ref.at[seqs], ko_ref.at[seqs],
                         vo_ref.at[seqs], tlen)


def _sample_mix_tile(sink_ref, p_ref, st_ref, kc_ref, vc_ref, alog_ref, dskip_ref, snorm_ref, qnw_ref, knw_ref,
                     y_ref, sto_ref, ko_ref, vo_ref, tlen):
    nseq = SUBLANES // tlen
    w = WINDOW
    pk = p_ref[...]
    z = pk[:, _S_Z0:_S_X0]
    xs = pk[:, _S_X0:_S_X0 + SSD_INNER]
    dt = pk[:, _S_DT0:_S_Q0]
    q = pk[:, _S_Q0:_S_K0]
    k = pk[:, _S_K0:_S_V0]
    v = pk[:, _S_V0:_S_END]
    gw = SSD_INNER // SSD_GROUPS
    bms = [pk[:, _S_X0 + SSD_INNER + g * SSD_STATE:_S_X0 + SSD_INNER + (g + 1) * SSD_STATE]
           for g in range(SSD_GROUPS)]
    c0 = _S_X0 + SSD_INNER + SSD_GROUPS * SSD_STATE
    cms = [pk[:, c0 + g * SSD_STATE:c0 + (g + 1) * SSD_STATE] for g in range(SSD_GROUPS)]

    rowi = lax.broadcasted_iota(jnp.int32, (SUBLANES, SSD_INNER), 0)
    assert tlen & (tlen - 1) == 0 and SUBLANES % tlen == 0
    tshift = tlen.bit_length() - 1
    sshift = SUBLANES.bit_length() - 1
    tpos = rowi & (tlen - 1)
    seq = rowi >> tshift
    seq_g = lax.broadcasted_iota(jnp.int32, (SUBLANES, gw), 0) >> tshift
    lane_lo = lax.broadcasted_iota(jnp.int32, (SUBLANES, LANES), 1) < HEAD_DIM

    dta = dt * (-jnp.exp(alog_ref[...]))
    acum = dta
    for s in range(1, tlen):
        acum = acum + jnp.where(tpos >= s, pltpu.roll(dta, s, axis=0), 0.0)
    tot = acum[tlen - 1:tlen, :]
    for b in range(1, nseq):
        tot = jnp.where(seq == b, acum[(b + 1) * tlen - 1:(b + 1) * tlen, :], tot)
    eac = jnp.exp(acum)
    cdec = jnp.exp(tot)
    xdt = xs * dt
    xw = xs * (jnp.exp(tot - acum) * dt)

    ydiag = jnp.zeros((SUBLANES, SSD_INNER), F32)
    for s in range(tlen):
        xsh = pltpu.roll(xdt, s, axis=0) if s else xdt
        ash = pltpu.roll(acum, s, axis=0) if s else acum
        cbs = []
        for g in range(SSD_GROUPS):
            bsh = pltpu.roll(bms[g], s, axis=0) if s else bms[g]
            cbs.append(jnp.broadcast_to(jnp.sum(cms[g] * bsh, axis=-1, keepdims=True), (SUBLANES, gw)))
        term = jnp.concatenate(cbs, axis=1) * jnp.exp(acum - ash) * xsh
        ydiag = ydiag + jnp.where(tpos >= s, term, 0.0)

    yoff = []
    for g in range(SSD_GROUPS):
        cm = cms[g].astype(BF16)
        bm = bms[g].astype(BF16)
        yo = None
        for b in range(nseq):
            h0 = st_ref[b, g * gw:(g + 1) * gw, :]
            yb = _dot_nt(cm, h0.astype(BF16))
            yo = yb if yo is None else jnp.where(seq_g == b, yb, yo)
            xwb = jnp.where(seq_g == b, xw[:, g * gw:(g + 1) * gw], 0.0).astype(BF16)
            cseq = jnp.broadcast_to(cdec[b * tlen:b * tlen + 1, :], (SUBLANES, SSD_INNER))
            blocks = []
            for hh in range(gw // HEAD_DIM):
                pm = (g * gw + hh * HEAD_DIM) // LANES
                cpair = cseq[:, pm * LANES:(pm + 1) * LANES]
                croll = pltpu.roll(cpair, HEAD_DIM, axis=1)
                chead = jnp.where(lane_lo, cpair, croll) if hh % 2 == 0 else jnp.where(lane_lo, croll, cpair)
                blocks += [chead] * (HEAD_DIM // SUBLANES)
            sto_ref[b, g * gw:(g + 1) * gw, :] = h0 * jnp.concatenate(blocks, axis=0) + _dot_tn(xwb, bm)
        yoff.append(yo)
    y = (ydiag + jnp.concatenate(yoff, axis=1) * eac + xs * dskip_ref[...]) * _silu(z)
    y_ssd = jnp.concatenate(
        [_rms(y[:, g * gw:(g + 1) * gw], snorm_ref[:, g * gw:(g + 1) * gw]) for g in range(SSD_GROUPS)], axis=1)

    kn = _norm_k(k, knw_ref[...], lane_lo)
    for b in range(nseq):
        ko_ref[b, 0:w - tlen, :] = kc_ref[b, tlen:w, :]
        vo_ref[b, 0:w - tlen, :] = vc_ref[b, tlen:w, :]
        ko_ref[b, w - tlen:w, :] = kn[b * tlen:(b + 1) * tlen, :]
        vo_ref[b, w - tlen:w, :] = v[b * tlen:(b + 1) * tlen, :]
    qw = q * qnw_ref[...]
    q2 = q * q
    pieces = []
    for h in range(ATT_HEADS):
        kv = h // ATT_GROUP
        lanes = slice((h // 2) * LANES, (h // 2 + 1) * LANES)
        own = lane_lo if h % 2 == 0 else jnp.logical_not(lane_lo)
        ssq = jnp.sum(jnp.where(own, q2[:, lanes], 0.0), axis=-1, keepdims=True)
        qh = jnp.where(own, qw[:, lanes], 0.0) * (lax.rsqrt(ssq / HEAD_DIM + EPS) * (HEAD_DIM ** -0.5))
        pieces.append(qh if h % 2 == kv else pltpu.roll(qh, HEAD_DIM, axis=1))
    qrows = jnp.concatenate(pieces, axis=0).astype(BF16)

    nq = ATT_HEADS * SUBLANES
    ncol = w + SUBLANES
    ri = lax.broadcasted_iota(jnp.int32, (nq, ncol), 0)
    cj = lax.broadcasted_iota(jnp.int32, (nq, ncol), 1)
    rt = ri & (tlen - 1)
    rseq = (ri & (SUBLANES - 1)) >> tshift
    rhead = ri >> sshift
    nj = jnp.maximum(cj - w, 0)
    in_cache = cj < w
    rel = jnp.where(in_cache, rt + w - cj, rt - (nj & (tlen - 1))).astype(F32)
    ok_new = ((nj >> tshift) == rseq) & ((nj & (tlen - 1)) <= rt)
    ok = (in_cache & (cj > rt)) | (jnp.logical_not(in_cache) & ok_new)
    slope = jnp.zeros((nq, ncol), F32)
    for h in range(ATT_HEADS):
        slope = jnp.where(rhead == h, ALIBI[h], slope)
    rh1 = lax.broadcasted_iota(jnp.int32, (nq, 1), 0) >> sshift
    sk = jnp.zeros((nq, 1), F32)
    for h in range(ATT_HEADS):
        sk = jnp.where(rh1 == h, sink_ref[h], sk)
    bias = slope * rel
    rseq1 = (lax.broadcasted_iota(jnp.int32, (nq, LANES), 0) & (SUBLANES - 1)) >> tshift

    out = jnp.zeros((nq, LANES), F32)
    for b in range(nseq):
        kc = jnp.concatenate([kc_ref[b], kn], axis=0).astype(BF16)
        vc = jnp.concatenate([vc_ref[b], v], axis=0).astype(BF16)
        s = jnp.where(ok, _dot_nt(qrows, kc) - bias, NEG)
        mx = jnp.maximum(jnp.max(s, axis=-1, keepdims=True), sk)
        p = jnp.exp(s - mx)
        den = jnp.sum(p, axis=-1, keepdims=True) + jnp.exp(sk - mx)
        o = _dot(p.astype(BF16), vc) / den
        out = jnp.where(rseq1 == b, o, out)
    pairs = []
    for m in range(ATT_HEADS // 2):
        halves = []
        for e in range(2):
            h = 2 * m + e
            blk = out[h * SUBLANES:(h + 1) * SUBLANES, :]
            halves.append(blk if h // ATT_GROUP == e else pltpu.roll(blk, HEAD_DIM, axis=1))
        pairs.append(jnp.where(lane_lo, halves[0], halves[1]))
    y_ref[...] = jnp.concatenate([y_ssd] + pairs, axis=1)


def _sample_mix(layer, packed, sinks, st, kc, vc, alog, dskip, snorm, qnw, knw, prev_outs, *, tlen):
    rows = packed.shape[0]
    nseq = SUBLANES // tlen
    consts = [alog, dskip, snorm, qnw, knw]
    carried = [] if prev_outs is None else list(prev_outs)
    n_in = 1 + 4 + len(consts)

    step_rows = SUBLANES * _SAMPLE_TILES
    step_seqs = nseq * _SAMPLE_TILES
    assert rows % step_rows == 0

    def slab(shape):
        return pl.BlockSpec((None, step_seqs) + shape, lambda i, *_: (layer, i, 0, 0))

    def out_slab(shape):
        if prev_outs is None:
            return pl.BlockSpec((st.shape[0], step_seqs) + shape, lambda i, *_: (0, i, 0, 0))
        return slab(shape)

    return pl.pallas_call(
        functools.partial(_sample_mix_kernel, tlen=tlen, layer=layer, all_layers=prev_outs is None),
        grid_spec=pltpu.PrefetchScalarGridSpec(
            num_scalar_prefetch=1,
            grid=(rows // step_rows,),
            in_specs=[
                pl.BlockSpec((step_rows, _S_END), lambda i, *_: (i, 0)),
                slab((SSD_INNER, SSD_STATE)), slab((WINDOW, KV_WIDTH)), slab((WINDOW, KV_WIDTH)),
            ] + [_const_spec(a.shape) for a in consts] + [pl.BlockSpec(memory_space=pl.ANY)] * len(carried),
            out_specs=[
                pl.BlockSpec((step_rows, D_MODEL), lambda i, *_: (i, 0)),
                out_slab((SSD_INNER, SSD_STATE)), out_slab((WINDOW, KV_WIDTH)), out_slab((WINDOW, KV_WIDTH)),
            ],
        ),
        out_shape=[
            jax.ShapeDtypeStruct((rows, D_MODEL), F32),
            jax.ShapeDtypeStruct(st.shape, F32),
            jax.ShapeDtypeStruct(kc.shape, F32),
            jax.ShapeDtypeStruct(vc.shape, F32),
        ],
        input_output_aliases={n_in + j: 1 + j for j in range(len(carried))},
        compiler_params=pltpu.CompilerParams(dimension_semantics=("arbitrary",), vmem_limit_bytes=VMEM_LIMIT),
        name="sample_mix",
    )(sinks, packed, st, kc, vc, *consts, *carried)


def _prep_layer(p, i):
    w_in = p['w_in'][i]
    s0, s1, s2, s3, s4 = (SSD_INNER, SSD_INNER + SSD_CONV_DIM, SSD_INNER + SSD_CONV_DIM + SSD_HEADS,
                          SSD_INNER + SSD_CONV_DIM + SSD_HEADS + ATT_WIDTH,
                          SSD_INNER + SSD_CONV_DIM + SSD_HEADS + ATT_WIDTH + KV_WIDTH)
    win = jnp.concatenate([w_in[:, :s1], w_in[:, s2:]], axis=1).astype(BF16)
    wdt = jnp.repeat(w_in[:, s1:s2], HEAD_DIM, axis=1).astype(BF16)

    def per_head(v):
        return jnp.repeat(v, HEAD_DIM)[None, :]

    def lane_padded(v):
        return jnp.pad(v, [(0, 0)] * (v.ndim - 1) + [(0, LANES - v.shape[-1])])

    mixer = (
        p['attn_sinks'][i],
        p['norm1_w'][i][None, :], win, wdt, p['ssd_conv_w'][i], p['ssd_conv_b'][i][None, :],
        per_head(p['dt_bias'][i]), per_head(p['a_log'][i]), per_head(p['d_skip'][i]), p['ssd_norm_w'][i][None, :],
        jnp.tile(p['q_norm_w'][i], ATT_HEADS)[None, :], jnp.tile(p['k_norm_w'][i], ATT_KV_HEADS)[None, :],
        p['w_out'][i].astype(BF16),
    )
    compact = (lane_padded(w_in[:, s1:s2]).astype(BF16), lane_padded(p['dt_bias'][i])[None, :],
               lane_padded(p['a_log'][i])[None, :])
    ffn = (
        p['norm2_w'][i][None, :], p['w_up'][i].astype(BF16), p['ffn_conv_w'][i], p['ffn_conv_b'][i][None, :],
        p['w_down'][i].astype(BF16),
    )
    return {'mixer': mixer, 'compact': compact, 'ffn': ffn}


_TM_MIXER = 256
_TM_FFN = 512


def _to_time_major(a):
    bs, t, c = a.shape
    return a.transpose(1, 0, 2).reshape(t * bs, c)


def _to_seq_major(a, bs):
    return a.reshape(a.shape[0] // bs, bs, a.shape[1]).transpose(1, 0, 2)


def kernel(x_prompt, x_sample, state_ssm, state_ssd_conv, cache_swa_k, cache_swa_v, state_ffn_conv, norm1_w, w_in,
           ssd_conv_w, ssd_conv_b, dt_bias, a_log, d_skip, ssd_norm_w, q_norm_w, k_norm_w, attn_sinks, w_out, norm2_w,
           w_up, ffn_conv_w, ffn_conv_b, w_down):
    p = dict(norm1_w=norm1_w, w_in=w_in, ssd_conv_w=ssd_conv_w, ssd_conv_b=ssd_conv_b, dt_bias=dt_bias, a_log=a_log,
             d_skip=d_skip, ssd_norm_w=ssd_norm_w, q_norm_w=q_norm_w, k_norm_w=k_norm_w, attn_sinks=attn_sinks,
             w_out=w_out, norm2_w=norm2_w, w_up=w_up, ffn_conv_w=ffn_conv_w, ffn_conv_b=ffn_conv_b, w_down=w_down)
    depth = w_in.shape[0]
    b = x_prompt.shape[0]
    bs, tlen, _ = x_sample.shape
    assert cache_swa_k.shape[2] == WINDOW
    head_shape = (SSD_HEADS, HEAD_DIM, SSD_STATE)
    kv_shape = (ATT_KV_HEADS, HEAD_DIM)

    xp = x_prompt
    xs = _to_time_major(x_sample)
    ssm_all = state_ssm.reshape(depth, bs, SSD_INNER, SSD_STATE)
    kc_all = cache_swa_k.reshape(depth, bs, WINDOW, KV_WIDTH)
    vc_all = cache_swa_v.reshape(depth, bs, WINDOW, KV_WIDTH)
    s_outs = None
    p_states, s_states = [], []
    for i in range(depth):
        w = _prep_layer(p, i)
        sinks, n1, win, wdt, cw, cb, dtb, alog, dskip, snorm, qnw, knw, wout = w['mixer']

        wdt_c, dtb_c, alog_c = w['compact']
        x1, ssm, conv, kc, vc = _mixer_prompt(xp, sinks, n1, win, wdt_c, cw, cb, dtb_c, alog_c, dskip, snorm, qnw,
                                              knw, wout, tm=_TM_MIXER)
        xp, ffn_tail = _ffn_prompt(x1, *w['ffn'], tm=_TM_FFN)
        p_states.append((ssm.reshape((b,) + head_shape), conv, kc.reshape((b, WINDOW) + kv_shape),
                         vc.reshape((b, WINDOW) + kv_shape), ffn_tail))

        packed, conv_tail = _sample_in(xs, _to_time_major(state_ssd_conv[i]), n1, win, wdt, cw, cb, dtb)
        packed = _to_seq_major(packed, bs).reshape(bs * tlen, -1)
        y, *s_outs = _sample_mix(i, packed, sinks, ssm_all, kc_all, vc_all, alog, dskip, snorm, qnw, knw, s_outs,
                                 tlen=tlen)
        y = _to_time_major(y.reshape(bs, tlen, D_MODEL))
        xs, ffn_tail_s = _ffn_sample(xs, y, wout, _to_time_major(state_ffn_conv[i]), *w['ffn'])
        s_states.append((_to_seq_major(conv_tail, bs), _to_seq_major(ffn_tail_s, bs)))

    def stacked(states, j):
        return jnp.stack([st[j] for st in states])

    ssm_s, k_s, v_s = s_outs
    return (xp, _to_seq_major(xs, bs)) + tuple(stacked(p_states, j) for j in range(5)) + (
        ssm_s.reshape((depth, bs) + head_shape), stacked(s_states, 0),
        k_s.reshape((depth, bs, WINDOW) + kv_shape), v_s.reshape((depth, bs, WINDOW) + kv_shape),
        stacked(s_states, 1))
```

```python
import functools

import numpy as np
import jax
import jax.numpy as jnp
from jax import lax
from jax.experimental import pallas as pl
from jax.experimental.pallas import tpu as pltpu

F32 = jnp.float32
BF16 = jnp.bfloat16

D_MODEL = 1024
HEAD_DIM = 64
SSD_HEADS = 8
SSD_INNER = SSD_HEADS * HEAD_DIM
SSD_GROUPS = 2
SSD_STATE = 128
SSD_CONV = 4
SSD_CHUNK = 128
SSD_CONV_DIM = SSD_INNER + 2 * SSD_GROUPS * SSD_STATE
ATT_HEADS = 8
ATT_KV_HEADS = 2
ATT_GROUP = ATT_HEADS // ATT_KV_HEADS
ATT_WIDTH = ATT_HEADS * HEAD_DIM
KV_WIDTH = ATT_KV_HEADS * HEAD_DIM
WINDOW = 128
D_FF = 2816
FFN_CONV = 3
EPS = 1e-6
NEG = -1e30
LOG2E = 1.4426950408889634

LANES = 128
SUBLANES = 8
VMEM_LIMIT = 56 * 1024 * 1024

_Z0, _X0, _Q0, _K0, _V0, _PEND = 0, 512, 1536, 2048, 2176, 2304
ALIBI = tuple(float(2.0 ** (-8.0 * (h + 1) / ATT_HEADS)) for h in range(ATT_HEADS))


def _rms(x, w):
    ms = jnp.mean(x * x, axis=-1, keepdims=True)
    return x * lax.rsqrt(ms + EPS) * w


def _silu(x):
    return x * jax.nn.sigmoid(x)


def _softplus(x):
    return jnp.maximum(x, 0.0) + jnp.log(1.0 + jnp.exp(-jnp.abs(x)))


def _dot(a, b):
    return jnp.dot(a, b, preferred_element_type=F32)


def _dot_nt(a, b):
    return lax.dot_general(a, b, (((1,), (1,)), ((), ())), preferred_element_type=F32)


def _dot_tn(a, b):
    return lax.dot_general(a, b, (((0,), (0,)), ((), ())), preferred_element_type=F32)


def _const_spec(shape):
    nd = len(shape)
    return pl.BlockSpec(shape, lambda *_: (0,) * nd, pipeline_mode=pl.Buffered(1))


def _full_spec(shape):
    nd = len(shape)
    return pl.BlockSpec(shape, lambda *_: (0,) * nd)


_FFN_COLS = 256


def _ffn_kernel(*refs, tm, pad, shift, carry, out_proj):
    refs = list(refs)
    x_ref = refs.pop(0)
    if out_proj:
        y_ref = refs.pop(0)
        wout_ref = refs.pop(0)
    if not carry:
        prev_ref = refs.pop(0)
    n2_ref, wup_ref, cw_ref, cb_ref, wdn_ref, o_ref, tail_ref, hp_ref, act_ref = refs

    if carry:
        @pl.when(pl.program_id(1) == 0)
        def _():
            hp_ref[0:pad, :] = jnp.zeros((pad, 2 * D_FF), F32)
        x = x_ref[0]
    else:
        hp_ref[0:pad, :] = prev_ref[...]
        x = x_ref[...]

    if out_proj:
        x = x + _dot(y_ref[...].astype(BF16), wout_ref[...])
    hn = _rms(x, n2_ref[...]).astype(BF16)
    hp_ref[pad:pad + tm, :] = _dot(hn, wup_ref[...])

    for j in range(0, D_FF, _FFN_COLS):
        def conv(c0):
            cols = slice(c0, c0 + _FFN_COLS)
            acc = hp_ref[pad:pad + tm, cols] * cw_ref[2:3, cols]
            acc = acc + hp_ref[pad - shift:pad - shift + tm, cols] * cw_ref[1:2, cols]
            acc = acc + hp_ref[pad - 2 * shift:pad - 2 * shift + tm, cols] * cw_ref[0:1, cols]
            return acc + cb_ref[:, cols]
        act_ref[:, j:j + _FFN_COLS] = (_silu(conv(j)) * conv(D_FF + j)).astype(BF16)

    out = x + _dot(act_ref[...], wdn_ref[...])
    tail = hp_ref[tm + pad - 2 * shift:tm + pad, :]
    if carry:
        o_ref[0] = out
        tail_ref[0] = tail
        hp_ref[0:pad, :] = hp_ref[tm:tm + pad, :]
    else:
        o_ref[...] = out
        tail_ref[...] = tail


def _ffn_prompt(x, n2, wup, cw, cb, wdn, *, tm):
    b, s, _ = x.shape
    pad = SUBLANES
    kern = functools.partial(_ffn_kernel, tm=tm, pad=pad, shift=1, carry=True, out_proj=False)
    return pl.pallas_call(
        kern,
        grid=(b, s // tm),
        in_specs=[
            pl.BlockSpec((1, tm, D_MODEL), lambda i, t: (i, t, 0)),
            _const_spec((1, D_MODEL)),
            _const_spec((D_MODEL, 2 * D_FF)),
            _const_spec((FFN_CONV, 2 * D_FF)),
            _const_spec((1, 2 * D_FF)),
            _const_spec((D_FF, D_MODEL)),
        ],
        out_specs=[
            pl.BlockSpec((1, tm, D_MODEL), lambda i, t: (i, t, 0)),
            pl.BlockSpec((1, FFN_CONV - 1, 2 * D_FF), lambda i, t: (i, 0, 0)),
        ],
        out_shape=[
            jax.ShapeDtypeStruct((b, s, D_MODEL), F32),
            jax.ShapeDtypeStruct((b, FFN_CONV - 1, 2 * D_FF), F32),
        ],
        scratch_shapes=[
            pltpu.VMEM((tm + pad, 2 * D_FF), F32),
            pltpu.VMEM((tm, D_FF), BF16),
        ],
        compiler_params=pltpu.CompilerParams(
            dimension_semantics=("arbitrary", "arbitrary"), vmem_limit_bytes=VMEM_LIMIT),
        name="ffn_prompt",
    )(x, n2, wup, cw, cb, wdn)


def _ffn_sample(x, y, wout, prev, n2, wup, cw, cb, wdn):
    rows = x.shape[0]
    bs = prev.shape[0] // (FFN_CONV - 1)
    pad = (FFN_CONV - 1) * bs
    kern = functools.partial(_ffn_kernel, tm=rows, pad=pad, shift=bs, carry=False, out_proj=True)
    return pl.pallas_call(
        kern,
        grid=(1,),
        in_specs=[
            _const_spec((rows, D_MODEL)),
            _const_spec((rows, D_MODEL)),
            _const_spec((D_MODEL, D_MODEL)),
            _const_spec((pad, 2 * D_FF)),
            _const_spec((1, D_MODEL)),
            _const_spec((D_MODEL, 2 * D_FF)),
            _const_spec((FFN_CONV, 2 * D_FF)),
            _const_spec((1, 2 * D_FF)),
            _const_spec((D_FF, D_MODEL)),
        ],
        out_specs=[
            _full_spec((rows, D_MODEL)),
            _full_spec((pad, 2 * D_FF)),
        ],
        out_shape=[
            jax.ShapeDtypeStruct((rows, D_MODEL), F32),
            jax.ShapeDtypeStruct((pad, 2 * D_FF), F32),
        ],
        scratch_shapes=[
            pltpu.VMEM((rows + pad, 2 * D_FF), F32),
            pltpu.VMEM((rows, D_FF), BF16),
        ],
        compiler_params=pltpu.CompilerParams(
            dimension_semantics=("arbitrary",), vmem_limit_bytes=VMEM_LIMIT),
        name="ffn_sample",
    )(x, y, wout, prev, n2, wup, cw, cb, wdn)


def _tri_cumsum(x, ltri):
    hi = x.astype(BF16)
    lo = (x - hi.astype(F32)).astype(BF16)
    return _dot(ltri, hi) + _dot(ltri, lo)


def _half_variants(a, lane_lo):
    ar = pltpu.roll(a, HEAD_DIM, axis=1)
    zero = jnp.zeros_like(a)
    return {
        (0, 0): jnp.where(lane_lo, a, zero).astype(BF16),
        (1, 1): jnp.where(lane_lo, zero, a).astype(BF16),
        (1, 0): jnp.where(lane_lo, ar, zero).astype(BF16),
        (0, 1): jnp.where(lane_lo, zero, ar).astype(BF16),
    }


def _conv_silu(cbuf_ref, cw_ref, cb_ref, r0, cols):
    pad, c = SUBLANES, SSD_CHUNK
    assert SSD_CONV == 4
    xe = cbuf_ref[r0:r0 + pad + c, cols]
    x1 = pltpu.roll(xe, 1, axis=0)
    u = xe[pad:] * cw_ref[3:4, cols] + x1[pad:] * cw_ref[2:3, cols]
    v = xe * cw_ref[1:2, cols] + x1 * cw_ref[0:1, cols]
    return _silu(u + pltpu.roll(v, 2, axis=0)[pad:] + cb_ref[:, cols])


_CHUNK_WEIGHTS = (5.0, 6.0, 5.0) + (1.5,) * 4 + (0.5,) * 4 + (1.0,)


def _chunk_pieces(r0, bufs, kprev, vprev, first_mask, bias_ref, st_ref, sink_ref, cw_ref, cb_ref, a, dskip, snorm,
                  qnw, knw, ltri, causal, lane_lo):
    proj_ref, cbuf_ref, dt_ref, y_ref, stage_ref = bufs
    c = SSD_CHUNK
    rows = slice(r0, r0 + c)
    gw = SSD_INNER // SSD_GROUPS
    qo, ko, vo = _Q0 - SSD_CONV_DIM, _K0 - SSD_CONV_DIM, _V0 - SSD_CONV_DIM

    dt_c = dt_ref[rows, :]
    acum_c = _tri_cumsum(dt_c * a, ltri)
    bc = _conv_silu(cbuf_ref, cw_ref, cb_ref, r0, slice(SSD_INNER, SSD_CONV_DIM))
    bms = [bc[:, g * SSD_STATE:(g + 1) * SSD_STATE].astype(BF16) for g in range(SSD_GROUPS)]
    cms = [bc[:, (SSD_GROUPS + g) * SSD_STATE:(SSD_GROUPS + g + 1) * SSD_STATE].astype(BF16)
           for g in range(SSD_GROUPS)]
    for g in range(SSD_GROUPS):
        stage_ref[ATT_HEADS + SSD_GROUPS + g, :, 0:c] = _dot_nt(cms[g], bms[g])
        stage_ref[ATT_HEADS + g] = _dot(cms[g], st_ref[:, g * gw:(g + 1) * gw].astype(BF16))
    yield

    q = proj_ref[rows, qo:qo + ATT_WIDTH]
    kn = _norm_k(proj_ref[rows, ko:ko + KV_WIDTH], knw, lane_lo)
    v = proj_ref[rows, vo:vo + KV_WIDTH]
    kcur = _half_variants(kn, lane_lo)
    vcur = _half_variants(v, lane_lo)
    qn = []
    for m in range(ATT_HEADS // 2):
        qp = q[:, m * LANES:(m + 1) * LANES]
        q2p = qp * qp
        rs_lo = lax.rsqrt(jnp.sum(jnp.where(lane_lo, q2p, 0.0), axis=-1, keepdims=True) / HEAD_DIM + EPS)
        rs_hi = lax.rsqrt(jnp.sum(jnp.where(lane_lo, 0.0, q2p), axis=-1, keepdims=True) / HEAD_DIM + EPS)
        qn.append((qp * qnw[:, m * LANES:(m + 1) * LANES]
                   * (jnp.where(lane_lo, rs_lo, rs_hi) * (HEAD_DIM ** -0.5 * LOG2E))).astype(BF16))
    for kv in range(ATT_KV_HEADS):
        qs = jnp.concatenate([qn[2 * kv], qn[2 * kv + 1]], axis=0)
        for e in range(2):
            sr = _dot_nt(qs, jnp.concatenate([kprev[(kv, e)], kcur[(kv, e)]], axis=0))
            stage_ref[ATT_GROUP * kv + e] = sr[0:c]
            stage_ref[ATT_GROUP * kv + 2 + e] = sr[c:2 * c]
    vall = {key: jnp.concatenate([vprev[key], vcur[key]], axis=0) for key in vcur}
    yield

    acols = [jnp.broadcast_to(acum_c[:, h:h + 1], (c, LANES)) for h in range(SSD_HEADS)]
    dcols = [jnp.broadcast_to(dt_c[:, h:h + 1], (c, LANES)) for h in range(SSD_HEADS)]
    acum = jnp.concatenate([jnp.where(lane_lo, acols[2 * m], acols[2 * m + 1]) for m in range(SSD_HEADS // 2)],
                           axis=1)
    dt = jnp.concatenate([jnp.where(lane_lo, dcols[2 * m], dcols[2 * m + 1]) for m in range(SSD_HEADS // 2)],
                         axis=1)
    xs = _conv_silu(cbuf_ref, cw_ref, cb_ref, r0, slice(0, SSD_INNER))
    alast = acum[c - 1:c, :]
    xw = (xs * (jnp.exp(alast - acum) * dt)).astype(BF16)
    cdec = jnp.exp(alast)
    for g in range(SSD_GROUPS):
        st_ref[:, g * gw:(g + 1) * gw] = (st_ref[:, g * gw:(g + 1) * gw] * cdec[:, g * gw:(g + 1) * gw]
                                          + _dot_tn(bms[g], xw[:, g * gw:(g + 1) * gw]))
    yield
    eac = jnp.exp(acum)
    xdt = xs * dt
    acum_t = acum_c.T
    ys = []
    for pm in range(SSD_INNER // LANES):
        g, m = divmod(pm, gw // LANES)
        lanes = slice(pm * LANES, (pm + 1) * LANES)
        ms = []
        for e in range(2):
            h = 2 * pm + e
            rowb = jnp.broadcast_to(acum_t[h:h + 1, :], (c, c))
            dec = jnp.exp(jnp.where(causal, acols[h] - rowb, NEG))
            ms.append((stage_ref[ATT_HEADS + SSD_GROUPS + g, :, 0:c] * dec).astype(BF16))
        xd = xdt[:, lanes]
        rhs = jnp.concatenate([jnp.where(lane_lo, xd, 0.0), jnp.where(lane_lo, 0.0, xd)], axis=0).astype(BF16)
        yield
        ydiag = _dot(jnp.concatenate(ms, axis=1), rhs)
        ys.append(ydiag + stage_ref[ATT_HEADS + g, :, m * LANES:(m + 1) * LANES] * eac[:, lanes])

    outs = []
    for m in range(ATT_HEADS // 2):
        kv = (2 * m) // ATT_GROUP
        ps, invs = [], []
        for e in range(2):
            h = 2 * m + e
            s = stage_ref[h] + bias_ref[h]
            if first_mask is not None:
                s = jnp.where(first_mask, s, NEG)
            sk = sink_ref[h] * LOG2E
            mx = jnp.maximum(jnp.max(s, axis=-1, keepdims=True), sk)
            p = jnp.exp2(s - mx)
            den = jnp.sum(p, axis=-1, keepdims=True) + jnp.exp2(sk - mx)
            ps.append(p.astype(BF16))
            invs.append(1.0 / den)
        yield
        o = _dot(jnp.concatenate(ps, axis=1), jnp.concatenate([vall[(kv, 0)], vall[(kv, 1)]], axis=0))
        outs.append(o * jnp.where(lane_lo, invs[0], invs[1]))

    y = (jnp.concatenate(ys, axis=1) + xs * dskip) * _silu(proj_ref[rows, _Z0:_Z0 + SSD_INNER])
    y_ref[rows, 0:SSD_INNER] = jnp.concatenate(
        [_rms(y[:, g * gw:(g + 1) * gw], snorm[:, g * gw:(g + 1) * gw]) for g in range(SSD_GROUPS)],
        axis=1).astype(BF16)
    y_ref[rows, SSD_INNER:] = jnp.concatenate(outs, axis=1).astype(BF16)
    yield
    return kn, v, kcur, vcur


def _norm_k(k, knw, lane_lo):
    k2 = k * k
    r0 = lax.rsqrt(jnp.sum(jnp.where(lane_lo, k2, 0.0), axis=-1, keepdims=True) / HEAD_DIM + EPS)
    r1 = lax.rsqrt(jnp.sum(jnp.where(lane_lo, 0.0, k2), axis=-1, keepdims=True) / HEAD_DIM + EPS)
    return k * jnp.where(lane_lo, r0, r1) * knw


def _weave(main, weights, side):
    done = 0
    total = float(sum(weights))
    acc = 0.0
    for k in range(len(weights) + 1):
        try:
            next(main)
        except StopIteration as stop:
            for thunk in side[done:]:
                thunk()
            return stop.value
        acc += weights[k]
        while done < len(side) and done * total < acc * len(side):
            side[done]()
            done += 1
    raise AssertionError("main generator has more pieces than declared")


_PROJ_COLS = 256


def _project_pieces(load_x, n1_ref, win_ref, wdt_ref, dtb_ref, bufs, tm):
    proj_ref, cbuf_ref, dt_ref, _, _ = bufs
    pad = SUBLANES
    box = {}

    def norm():
        box['hn'] = _rms(load_x(), n1_ref[...]).astype(BF16)

    def slab(c0, c1):
        def run():
            res = _dot(box['hn'], win_ref[:, c0:c1])
            if c0 < _X0:
                proj_ref[:, c0:c1] = res
            elif c0 < _Q0:
                cbuf_ref[pad:pad + tm, c0 - _X0:c1 - _X0] = res
            else:
                proj_ref[:, c0 - SSD_CONV_DIM:c1 - SSD_CONV_DIM] = res
        return run

    def step_sizes():
        dt_ref[...] = _softplus(_dot(box['hn'], wdt_ref[...]) + dtb_ref[...])

    slabs = []
    for seg0, seg1 in ((_Z0, _X0), (_X0, _Q0), (_Q0, _PEND)):
        for c0 in range(seg0, seg1, _PROJ_COLS):
            slabs.append(slab(c0, min(c0 + _PROJ_COLS, seg1)))
    return [norm] + slabs + [step_sizes]


def _outproj_pieces(load_x, load_y, wout_ref, store):
    def slab(c0, c1):
        def run():
            store(c0, c1, load_x(c0, c1) + _dot(load_y(), wout_ref[:, c0:c1]))
        return run
    return [slab(c0, c0 + 2 * LANES) for c0 in range(0, D_MODEL, 2 * LANES)]


def _score_bias(bias_ref):
    c = SSD_CHUNK
    qi = lax.broadcasted_iota(jnp.int32, (c, 2 * c), 0)
    sj = lax.broadcasted_iota(jnp.int32, (c, 2 * c), 1)
    reli = qi + c - sj
    rel = reli.astype(F32)
    in_window = (reli >= 0) & (reli < WINDOW)
    for h in range(ATT_HEADS):
        bias_ref[h] = jnp.where(in_window, (-ALIBI[h] * LOG2E) * rel, NEG)


def _mix_tile(bufs, first_col, kprev, vprev, st_ref, bias_ref, sink_ref, cw_ref, cb_ref, alog_ref, dskip_ref,
              snorm_ref, qnw_ref, knw_ref, tm):
    c = SSD_CHUNK
    row = lax.broadcasted_iota(jnp.int32, (c, c), 0)
    col = lax.broadcasted_iota(jnp.int32, (c, c), 1)
    causal = row >= col
    ltri = jnp.where(causal, 1.0, 0.0).astype(BF16)
    lane_lo = col < HEAD_DIM
    a = -jnp.exp(alog_ref[...])

    kn = v = None
    for ci in range(tm // c):
        first_mask = None
        if ci == 0 and first_col is not None:
            first_mask = lax.broadcasted_iota(jnp.int32, (c, 2 * c), 1) >= first_col
        kn, v, kprev, vprev = yield from _chunk_pieces(
            ci * c, bufs, kprev, vprev, first_mask, bias_ref, st_ref, sink_ref, cw_ref, cb_ref, a, dskip_ref[...],
            snorm_ref[...], qnw_ref[...], knw_ref[...], ltri, causal, lane_lo)
    return kn, v, kprev, vprev


_MIX_SEQS = 2
_SCRATCH_PER_SEQ = 13


def _round_robin(gens):
    results = [None] * len(gens)
    live = list(range(len(gens)))
    while live:
        for j in list(live):
            try:
                next(gens[j])
            except StopIteration as stop:
                results[j] = stop.value
                live.remove(j)
                continue
            yield
    return results


def _interleaved(lists):
    return [item for group in zip(*lists) for item in group]


def _mixer_prompt_kernel(sink_ref, xcur_ref, xnext_ref, n1_ref, win_ref, wdt_ref, cw_ref, cb_ref, dtb_ref, alog_ref,
                         dskip_ref, snorm_ref, qnw_ref, knw_ref, wout_ref,
                         o_ref, ssm_ref, conv_ref, kc_ref, vc_ref, *scratch, tm):
    i = pl.program_id(0)
    s = pl.program_id(1)
    ns = pl.num_programs(1)
    pad = SUBLANES
    nq = _MIX_SEQS
    bias_ref = scratch[-1]
    seqs = []
    for q in range(nq):
        r = scratch[q * _SCRATCH_PER_SEQ:(q + 1) * _SCRATCH_PER_SEQ]
        seqs.append(dict(a=tuple(r[0:5]), b=tuple(r[5:10]), st=r[10], kprev=r[11], vprev=r[12]))
    proj_args = (n1_ref, win_ref, wdt_ref, dtb_ref)

    def mix_args(q):
        return (seqs[q]['st'], bias_ref, sink_ref, cw_ref, cb_ref, alog_ref, dskip_ref, snorm_ref, qnw_ref, knw_ref,
                tm)

    @pl.when((i == 0) & (s == 0))
    def _():
        _score_bias(bias_ref)
        for q in range(nq):
            for piece in _project_pieces(lambda q=q: xcur_ref[q, 0:tm, :], *proj_args, seqs[q]['a'], tm):
                piece()

    @pl.when(s == 0)
    def _():
        for q in range(nq):
            seqs[q]['st'][...] = jnp.zeros(seqs[q]['st'].shape, F32)
            seqs[q]['a'][1][0:pad, :] = jnp.zeros((pad, SSD_CONV_DIM), F32)
            seqs[q]['kprev'][...] = jnp.zeros(seqs[q]['kprev'].shape, F32)
            seqs[q]['vprev'][...] = jnp.zeros(seqs[q]['vprev'].shape, F32)

    @pl.when(s > 0)
    def _():
        for q in range(nq):
            seqs[q]['a'][1][0:pad, :] = seqs[q]['b'][1][tm:tm + pad, :]

    lane_lo = lax.broadcasted_iota(jnp.int32, (SSD_CHUNK, LANES), 1) < HEAD_DIM
    first_col = jnp.where(s == 0, SSD_CHUNK, 0)
    weights = tuple(w for w in _CHUNK_WEIGHTS * (tm // SSD_CHUNK) for _ in range(nq))

    def outproj(q, r0, y_ref):
        def store(c0, c1, val):
            o_ref[q, r0:r0 + tm, c0:c1] = val
        return _outproj_pieces(lambda c0, c1: xcur_ref[q, r0:r0 + tm, c0:c1], lambda: y_ref[...], wout_ref, store)

    side = _interleaved([_project_pieces(lambda q=q: xcur_ref[q, tm:2 * tm, :], *proj_args, seqs[q]['b'], tm)
                         for q in range(nq)])
    res_a = _weave(_round_robin(
        [_mix_tile(seqs[q]['a'], first_col, _half_variants(seqs[q]['kprev'][...], lane_lo),
                   _half_variants(seqs[q]['vprev'][...], lane_lo), *mix_args(q)) for q in range(nq)]), weights, side)
    for q in range(nq):
        seqs[q]['b'][1][0:pad, :] = seqs[q]['a'][1][tm:tm + pad, :]
    side = _interleaved([outproj(q, 0, seqs[q]['a'][3])
                         + _project_pieces(lambda q=q: xnext_ref[q], *proj_args, seqs[q]['a'], tm) for q in range(nq)])
    res_b = _weave(_round_robin(
        [_mix_tile(seqs[q]['b'], None, res_a[q][2], res_a[q][3], *mix_args(q)) for q in range(nq)]), weights, side)
    for piece in _interleaved([outproj(q, tm, seqs[q]['b'][3]) for q in range(nq)]):
        piece()
    for q in range(nq):
        seqs[q]['kprev'][...] = res_b[q][0]
        seqs[q]['vprev'][...] = res_b[q][1]

    @pl.when(s == ns - 1)
    def _():
        for q in range(nq):
            ssm_ref[q] = seqs[q]['st'][...].T
            conv_ref[q] = seqs[q]['b'][1][tm + pad - (SSD_CONV - 1):tm + pad, :]
            kc_ref[q] = res_b[q][0]
            vc_ref[q] = res_b[q][1]


def _mixer_prompt(x, sinks, n1, win, wdt, cw, cb, dtb, alog, dskip, snorm, qnw, knw, wout, *, tm):
    b, s, _ = x.shape
    pw = _PEND - SSD_CONV_DIM
    nt = s // tm
    ns = nt // 2
    kern = functools.partial(_mixer_prompt_kernel, tm=tm)
    consts = [n1, win, wdt, cw, cb, dtb, alog, dskip, snorm, qnw, knw, wout]

    nq = _MIX_SEQS
    assert b % nq == 0 and nt % 2 == 0
    groups = b // nq

    def next_tile(i, t, *_):
        flat = jnp.minimum((i * ns + t) * 2 + 2, groups * nt - 1)
        return (flat // nt, flat % nt, 0)

    bufset = [
        pltpu.VMEM((tm, pw), F32),
        pltpu.VMEM((tm + SUBLANES, SSD_CONV_DIM), F32),
        pltpu.VMEM((tm, LANES), F32),
        pltpu.VMEM((tm, D_MODEL), BF16),
        pltpu.VMEM((ATT_HEADS + 2 * SSD_GROUPS, SSD_CHUNK, 2 * SSD_CHUNK), F32),
    ]
    per_seq = bufset + bufset + [
        pltpu.VMEM((SSD_STATE, SSD_INNER), F32),
        pltpu.VMEM((WINDOW, KV_WIDTH), F32),
        pltpu.VMEM((WINDOW, KV_WIDTH), F32),
    ]
    assert len(per_seq) == _SCRATCH_PER_SEQ
    return pl.pallas_call(
        kern,
        grid_spec=pltpu.PrefetchScalarGridSpec(
            num_scalar_prefetch=1,
            grid=(groups, ns),
            in_specs=[pl.BlockSpec((nq, 2 * tm, D_MODEL), lambda i, t, *_: (i, t, 0)),
                      pl.BlockSpec((nq, tm, D_MODEL), next_tile)]
            + [_const_spec(w.shape) for w in consts],
            out_specs=[
                pl.BlockSpec((nq, 2 * tm, D_MODEL), lambda i, t, *_: (i, t, 0)),
                pl.BlockSpec((nq, SSD_INNER, SSD_STATE), lambda i, t, *_: (i, 0, 0)),
                pl.BlockSpec((nq, SSD_CONV - 1, SSD_CONV_DIM), lambda i, t, *_: (i, 0, 0)),
                pl.BlockSpec((nq, WINDOW, KV_WIDTH), lambda i, t, *_: (i, 0, 0)),
                pl.BlockSpec((nq, WINDOW, KV_WIDTH), lambda i, t, *_: (i, 0, 0)),
            ],
            scratch_shapes=per_seq * nq + [pltpu.VMEM((ATT_HEADS, SSD_CHUNK, 2 * SSD_CHUNK), F32)],
        ),
        out_shape=[
            jax.ShapeDtypeStruct((b, s, D_MODEL), F32),
            jax.ShapeDtypeStruct((b, SSD_INNER, SSD_STATE), F32),
            jax.ShapeDtypeStruct((b, SSD_CONV - 1, SSD_CONV_DIM), F32),
            jax.ShapeDtypeStruct((b, WINDOW, KV_WIDTH), F32),
            jax.ShapeDtypeStruct((b, WINDOW, KV_WIDTH), F32),
        ],
        compiler_params=pltpu.CompilerParams(
            dimension_semantics=("arbitrary", "arbitrary"), vmem_limit_bytes=VMEM_LIMIT),
        name="mixer_prompt",
    )(sinks, x, x, *consts)


_S_Z0, _S_X0, _S_DT0, _S_Q0, _S_K0, _S_V0, _S_END = 0, 512, 1536, 2048, 2560, 2688, 2816


def _sample_in_kernel(x_ref, prev_ref, n1_ref, win_ref, wdt_ref, cw_ref, cb_ref, dtb_ref, o_ref, tail_ref, cbuf_ref,
                      *, rows, bs):
    pad = (SSD_CONV - 1) * bs
    hn = _rms(x_ref[...], n1_ref[...]).astype(BF16)
    o_ref[:, _S_Z0:_S_X0] = _dot(hn, win_ref[:, _Z0:_X0])
    cbuf_ref[0:pad, :] = prev_ref[...]
    cbuf_ref[pad:pad + rows, :] = _dot(hn, win_ref[:, _X0:_Q0])
    acc = cb_ref[...] + cbuf_ref[pad:pad + rows, :] * cw_ref[SSD_CONV - 1:SSD_CONV, :]
    for i in range(1, SSD_CONV):
        acc = acc + cbuf_ref[pad - i * bs:pad - i * bs + rows, :] * cw_ref[SSD_CONV - 1 - i:SSD_CONV - i, :]
    o_ref[:, _S_X0:_S_DT0] = _silu(acc)
    o_ref[:, _S_DT0:_S_Q0] = _softplus(_dot(hn, wdt_ref[...]) + dtb_ref[...])
    o_ref[:, _S_Q0:_S_END] = _dot(hn, win_ref[:, _Q0:_PEND])
    tail_ref[...] = cbuf_ref[rows:rows + pad, :]


def _sample_in(x, prev, n1, win, wdt, cw, cb, dtb):
    rows = x.shape[0]
    pad = prev.shape[0]
    bs = pad // (SSD_CONV - 1)
    args = (x, prev, n1, win, wdt, cw, cb, dtb)
    return pl.pallas_call(
        functools.partial(_sample_in_kernel, rows=rows, bs=bs),
        grid=(1,),
        in_specs=[_const_spec(a.shape) for a in args],
        out_specs=[_full_spec((rows, _S_END)), _full_spec((pad, SSD_CONV_DIM))],
        out_shape=[jax.ShapeDtypeStruct((rows, _S_END), F32), jax.ShapeDtypeStruct((pad, SSD_CONV_DIM), F32)],
        scratch_shapes=[pltpu.VMEM((rows + pad, SSD_CONV_DIM), F32)],
        compiler_params=pltpu.CompilerParams(dimension_semantics=("arbitrary",), vmem_limit_bytes=VMEM_LIMIT),
        name="sample_in",
    )(*args)


_SAMPLE_TILES = 8


def _sample_mix_kernel(sink_ref, p_ref, st_ref, kc_ref, vc_ref, alog_ref, dskip_ref, snorm_ref, qnw_ref, knw_ref,
                       *rest, tlen, layer, all_layers):
    y_ref, sto_ref, ko_ref, vo_ref = rest[-4:]
    if all_layers:
        for ref in (sto_ref, ko_ref, vo_ref):
            for other in range(ref.shape[0]):
                if other != layer:
                    ref[other] = jnp.zeros(ref.shape[1:], F32)
        sto_ref, ko_ref, vo_ref = sto_ref.at[layer], ko_ref.at[layer], vo_ref.at[layer]
    nseq = SUBLANES // tlen
    for j in range(_SAMPLE_TILES):
        rows = pl.ds(j * SUBLANES, SUBLANES)
        seqs = pl.ds(j * nseq, nseq)
        _sample_mix_tile(sink_ref, p_ref.at[rows], st_ref.at[seqs], kc_ref.at[seqs], vc_ref.at[seqs], alog_ref,
                         dskip_ref, snorm_ref, qnw_ref, knw_ref, y_ref.at[rows], sto_ref.at[seqs], ko_ref.at[seqs],
                         vo_ref.at[seqs], tlen)


def _sample_mix_tile(sink_ref, p_ref, st_ref, kc_ref, vc_ref, alog_ref, dskip_ref, snorm_ref, qnw_ref, knw_ref,
                     y_ref, sto_ref, ko_ref, vo_ref, tlen):
    nseq = SUBLANES // tlen
    w = WINDOW
    pk = p_ref[...]
    z = pk[:, _S_Z0:_S_X0]
    xs = pk[:, _S_X0:_S_X0 + SSD_INNER]
    dt = pk[:, _S_DT0:_S_Q0]
    q = pk[:, _S_Q0:_S_K0]
    k = pk[:, _S_K0:_S_V0]
    v = pk[:, _S_V0:_S_END]
    gw = SSD_INNER // SSD_GROUPS
    bms = [pk[:, _S_X0 + SSD_INNER + g * SSD_STATE:_S_X0 + SSD_INNER + (g + 1) * SSD_STATE]
           for g in range(SSD_GROUPS)]
    c0 = _S_X0 + SSD_INNER + SSD_GROUPS * SSD_STATE
    cms = [pk[:, c0 + g * SSD_STATE:c0 + (g + 1) * SSD_STATE] for g in range(SSD_GROUPS)]

    rowi = lax.broadcasted_iota(jnp.int32, (SUBLANES, SSD_INNER), 0)
    assert tlen & (tlen - 1) == 0 and SUBLANES % tlen == 0
    tshift = tlen.bit_length() - 1
    sshift = SUBLANES.bit_length() - 1
    tpos = rowi & (tlen - 1)
    seq = rowi >> tshift
    seq_g = lax.broadcasted_iota(jnp.int32, (SUBLANES, gw), 0) >> tshift
    lane_lo = lax.broadcasted_iota(jnp.int32, (SUBLANES, LANES), 1) < HEAD_DIM

    dta = dt * (-jnp.exp(alog_ref[...]))
    acum = dta
    for s in range(1, tlen):
        acum = acum + jnp.where(tpos >= s, pltpu.roll(dta, s, axis=0), 0.0)
    tot = acum[tlen - 1:tlen, :]
    for b in range(1, nseq):
        tot = jnp.where(seq == b, acum[(b + 1) * tlen - 1:(b + 1) * tlen, :], tot)
    eac = jnp.exp(acum)
    cdec = jnp.exp(tot)
    xdt = xs * dt
    xw = xs * (jnp.exp(tot - acum) * dt)

    ydiag = jnp.zeros((SUBLANES, SSD_INNER), F32)
    for s in range(tlen):
        xsh = pltpu.roll(xdt, s, axis=0) if s else xdt
        ash = pltpu.roll(acum, s, axis=0) if s else acum
        cbs = []
        for g in range(SSD_GROUPS):
            bsh = pltpu.roll(bms[g], s, axis=0) if s else bms[g]
            cbs.append(jnp.broadcast_to(jnp.sum(cms[g] * bsh, axis=-1, keepdims=True), (SUBLANES, gw)))
        term = jnp.concatenate(cbs, axis=1) * jnp.exp(acum - ash) * xsh
        ydiag = ydiag + jnp.where(tpos >= s, term, 0.0)

    yoff = []
    for g in range(SSD_GROUPS):
        cm = cms[g].astype(BF16)
        bm = bms[g].astype(BF16)
        yo = None
        for b in range(nseq):
            h0 = st_ref[b, g * gw:(g + 1) * gw, :]
            yb = _dot_nt(cm, h0.astype(BF16))
            yo = yb if yo is None else jnp.where(seq_g == b, yb, yo)
            xwb = jnp.where(seq_g == b, xw[:, g * gw:(g + 1) * gw], 0.0).astype(BF16)
            cseq = jnp.broadcast_to(cdec[b * tlen:b * tlen + 1, :], (SUBLANES, SSD_INNER))
            blocks = []
            for hh in range(gw // HEAD_DIM):
                pm = (g * gw + hh * HEAD_DIM) // LANES
                cpair = cseq[:, pm * LANES:(pm + 1) * LANES]
                croll = pltpu.roll(cpair, HEAD_DIM, axis=1)
                chead = jnp.where(lane_lo, cpair, croll) if hh % 2 == 0 else jnp.where(lane_lo, croll, cpair)
                blocks += [chead] * (HEAD_DIM // SUBLANES)
            sto_ref[b, g * gw:(g + 1) * gw, :] = h0 * jnp.concatenate(blocks, axis=0) + _dot_tn(xwb, bm)
        yoff.append(yo)
    y = (ydiag + jnp.concatenate(yoff, axis=1) * eac + xs * dskip_ref[...]) * _silu(z)
    y_ssd = jnp.concatenate(
        [_rms(y[:, g * gw:(g + 1) * gw], snorm_ref[:, g * gw:(g + 1) * gw]) for g in range(SSD_GROUPS)], axis=1)

    kn = _norm_k(k, knw_ref[...], lane_lo)
    for b in range(nseq):
        ko_ref[b, 0:w - tlen, :] = kc_ref[b, tlen:w, :]
        vo_ref[b, 0:w - tlen, :] = vc_ref[b, tlen:w, :]
        ko_ref[b, w - tlen:w, :] = kn[b * tlen:(b + 1) * tlen, :]
        vo_ref[b, w - tlen:w, :] = v[b * tlen:(b + 1) * tlen, :]
    qw = q * qnw_ref[...]
    q2 = q * q
    pieces = []
    for h in range(ATT_HEADS):
        kv = h // ATT_GROUP
        lanes = slice((h // 2) * LANES, (h // 2 + 1) * LANES)
        own = lane_lo if h % 2 == 0 else jnp.logical_not(lane_lo)
        ssq = jnp.sum(jnp.where(own, q2[:, lanes], 0.0), axis=-1, keepdims=True)
        qh = jnp.where(own, qw[:, lanes], 0.0) * (lax.rsqrt(ssq / HEAD_DIM + EPS) * (HEAD_DIM ** -0.5))
        pieces.append(qh if h % 2 == kv else pltpu.roll(qh, HEAD_DIM, axis=1))
    qrows = jnp.concatenate(pieces, axis=0).astype(BF16)

    nq = ATT_HEADS * SUBLANES
    ncol = w + SUBLANES
    ri = lax.broadcasted_iota(jnp.int32, (nq, ncol), 0)
    cj = lax.broadcasted_iota(jnp.int32, (nq, ncol), 1)
    rt = ri & (tlen - 1)
    rseq = (ri & (SUBLANES - 1)) >> tshift
    rhead = ri >> sshift
    nj = jnp.maximum(cj - w, 0)
    in_cache = cj < w
    rel = jnp.where(in_cache, rt + w - cj, rt - (nj & (tlen - 1))).astype(F32)
    ok_new = ((nj >> tshift) == rseq) & ((nj & (tlen - 1)) <= rt)
    ok = (in_cache & (cj > rt)) | (jnp.logical_not(in_cache) & ok_new)
    slope = jnp.zeros((nq, ncol), F32)
    for h in range(ATT_HEADS):
        slope = jnp.where(rhead == h, ALIBI[h], slope)
    rh1 = lax.broadcasted_iota(jnp.int32, (nq, 1), 0) >> sshift
    sk = jnp.zeros((nq, 1), F32)
    for h in range(ATT_HEADS):
        sk = jnp.where(rh1 == h, sink_ref[h], sk)
    bias = slope * rel
    rseq1 = (lax.broadcasted_iota(jnp.int32, (nq, LANES), 0) & (SUBLANES - 1)) >> tshift

    out = jnp.zeros((nq, LANES), F32)
    for b in range(nseq):
        kc = jnp.concatenate([kc_ref[b], kn], axis=0).astype(BF16)
        vc = jnp.concatenate([vc_ref[b], v], axis=0).astype(BF16)
        s = jnp.where(ok, _dot_nt(qrows, kc) - bias, NEG)
        mx = jnp.maximum(jnp.max(s, axis=-1, keepdims=True), sk)
        p = jnp.exp(s - mx)
        den = jnp.sum(p, axis=-1, keepdims=True) + jnp.exp(sk - mx)
        o = _dot(p.astype(BF16), vc) / den
        out = jnp.where(rseq1 == b, o, out)
    pairs = []
    for m in range(ATT_HEADS // 2):
        halves = []
        for e in range(2):
            h = 2 * m + e
            blk = out[h * SUBLANES:(h + 1) * SUBLANES, :]
            halves.append(blk if h // ATT_GROUP == e else pltpu.roll(blk, HEAD_DIM, axis=1))
        pairs.append(jnp.where(lane_lo, halves[0], halves[1]))
    y_ref[...] = jnp.concatenate([y_ssd] + pairs, axis=1)


def _sample_mix(layer, packed, sinks, st, kc, vc, alog, dskip, snorm, qnw, knw, prev_outs, *, tlen):
    rows = packed.shape[0]
    nseq = SUBLANES // tlen
    consts = [alog, dskip, snorm, qnw, knw]
    carried = [] if prev_outs is None else list(prev_outs)
    n_in = 1 + 4 + len(consts)

    step_rows = SUBLANES * _SAMPLE_TILES
    step_seqs = nseq * _SAMPLE_TILES
    assert rows % step_rows == 0

    def slab(shape):
        return pl.BlockSpec((None, step_seqs) + shape, lambda i, *_: (layer, i, 0, 0))

    def out_slab(shape):
        if prev_outs is None:
            return pl.BlockSpec((st.shape[0], step_seqs) + shape, lambda i, *_: (0, i, 0, 0))
        return slab(shape)

    return pl.pallas_call(
        functools.partial(_sample_mix_kernel, tlen=tlen, layer=layer, all_layers=prev_outs is None),
        grid_spec=pltpu.PrefetchScalarGridSpec(
            num_scalar_prefetch=1,
            grid=(rows // step_rows,),
            in_specs=[
                pl.BlockSpec((step_rows, _S_END), lambda i, *_: (i, 0)),
                slab((SSD_INNER, SSD_STATE)), slab((WINDOW, KV_WIDTH)), slab((WINDOW, KV_WIDTH)),
            ] + [_const_spec(a.shape) for a in consts] + [pl.BlockSpec(memory_space=pl.ANY)] * len(carried),
            out_specs=[
                pl.BlockSpec((step_rows, D_MODEL), lambda i, *_: (i, 0)),
                out_slab((SSD_INNER, SSD_STATE)), out_slab((WINDOW, KV_WIDTH)), out_slab((WINDOW, KV_WIDTH)),
            ],
        ),
        out_shape=[
            jax.ShapeDtypeStruct((rows, D_MODEL), F32),
            jax.ShapeDtypeStruct(st.shape, F32),
            jax.ShapeDtypeStruct(kc.shape, F32),
            jax.ShapeDtypeStruct(vc.shape, F32),
        ],
        input_output_aliases={n_in + j: 1 + j for j in range(len(carried))},
        compiler_params=pltpu.CompilerParams(dimension_semantics=("arbitrary",), vmem_limit_bytes=VMEM_LIMIT),
        name="sample_mix",
    )(sinks, packed, st, kc, vc, *consts, *carried)


def _prep_layer(p, i):
    w_in = p['w_in'][i]
    s0, s1, s2, s3, s4 = (SSD_INNER, SSD_INNER + SSD_CONV_DIM, SSD_INNER + SSD_CONV_DIM + SSD_HEADS,
                          SSD_INNER + SSD_CONV_DIM + SSD_HEADS + ATT_WIDTH,
                          SSD_INNER + SSD_CONV_DIM + SSD_HEADS + ATT_WIDTH + KV_WIDTH)
    win = jnp.concatenate([w_in[:, :s1], w_in[:, s2:]], axis=1).astype(BF16)
    wdt = jnp.repeat(w_in[:, s1:s2], HEAD_DIM, axis=1).astype(BF16)

    def per_head(v):
        return jnp.repeat(v, HEAD_DIM)[None, :]

    def lane_padded(v):
        return jnp.pad(v, [(0, 0)] * (v.ndim - 1) + [(0, LANES - v.shape[-1])])

    mixer = (
        p['attn_sinks'][i],
        p['norm1_w'][i][None, :], win, wdt, p['ssd_conv_w'][i], p['ssd_conv_b'][i][None, :],
        per_head(p['dt_bias'][i]), per_head(p['a_log'][i]), per_head(p['d_skip'][i]), p['ssd_norm_w'][i][None, :],
        jnp.tile(p['q_norm_w'][i], ATT_HEADS)[None, :], jnp.tile(p['k_norm_w'][i], ATT_KV_HEADS)[None, :],
        p['w_out'][i].astype(BF16),
    )
    compact = (lane_padded(w_in[:, s1:s2]).astype(BF16), lane_padded(p['dt_bias'][i])[None, :],
               lane_padded(p['a_log'][i])[None, :])
    ffn = (
        p['norm2_w'][i][None, :], p['w_up'][i].astype(BF16), p['ffn_conv_w'][i], p['ffn_conv_b'][i][None, :],
        p['w_down'][i].astype(BF16),
    )
    return {'mixer': mixer, 'compact': compact, 'ffn': ffn}


_TM_MIXER = 256
_TM_FFN = 512


def _to_time_major(a):
    bs, t, c = a.shape
    return a.transpose(1, 0, 2).reshape(t * bs, c)


def _to_seq_major(a, bs):
    return a.reshape(a.shape[0] // bs, bs, a.shape[1]).transpose(1, 0, 2)


def kernel(x_prompt, x_sample, state_ssm, state_ssd_conv, cache_swa_k, cache_swa_v, state_ffn_conv, norm1_w, w_in,
           ssd_conv_w, ssd_conv_b, dt_bias, a_log, d_skip, ssd_norm_w, q_norm_w, k_norm_w, attn_sinks, w_out, norm2_w,
           w_up, ffn_conv_w, ffn_conv_b, w_down):
    p = dict(norm1_w=norm1_w, w_in=w_in, ssd_conv_w=ssd_conv_w, ssd_conv_b=ssd_conv_b, dt_bias=dt_bias, a_log=a_log,
             d_skip=d_skip, ssd_norm_w=ssd_norm_w, q_norm_w=q_norm_w, k_norm_w=k_norm_w, attn_sinks=attn_sinks,
             w_out=w_out, norm2_w=norm2_w, w_up=w_up, ffn_conv_w=ffn_conv_w, ffn_conv_b=ffn_conv_b, w_down=w_down)
    depth = w_in.shape[0]
    b = x_prompt.shape[0]
    bs, tlen, _ = x_sample.shape
    assert cache_swa_k.shape[2] == WINDOW
    head_shape = (SSD_HEADS, HEAD_DIM, SSD_STATE)
    kv_shape = (ATT_KV_HEADS, HEAD_DIM)

    xp = x_prompt
    xs = _to_time_major(x_sample)
    ssm_all = state_ssm.reshape(depth, bs, SSD_INNER, SSD_STATE)
    kc_all = cache_swa_k.reshape(depth, bs, WINDOW, KV_WIDTH)
    vc_all = cache_swa_v.reshape(depth, bs, WINDOW, KV_WIDTH)
    s_outs = None
    p_states, s_states = [], []
    for i in range(depth):
        w = _prep_layer(p, i)
        sinks, n1, win, wdt, cw, cb, dtb, alog, dskip, snorm, qnw, knw, wout = w['mixer']

        wdt_c, dtb_c, alog_c = w['compact']
        x1, ssm, conv, kc, vc = _mixer_prompt(xp, sinks, n1, win, wdt_c, cw, cb, dtb_c, alog_c, dskip, snorm, qnw,
                                              knw, wout, tm=_TM_MIXER)
        xp, ffn_tail = _ffn_prompt(x1, *w['ffn'], tm=_TM_FFN)
        p_states.append((ssm.reshape((b,) + head_shape), conv, kc.reshape((b, WINDOW) + kv_shape),
                         vc.reshape((b, WINDOW) + kv_shape), ffn_tail))

        packed, conv_tail = _sample_in(xs, _to_time_major(state_ssd_conv[i]), n1, win, wdt, cw, cb, dtb)
        packed = _to_seq_major(packed, bs).reshape(bs * tlen, -1)
        y, *s_outs = _sample_mix(i, packed, sinks, ssm_all, kc_all, vc_all, alog, dskip, snorm, qnw, knw, s_outs,
                                 tlen=tlen)
        y = _to_time_major(y.reshape(bs, tlen, D_MODEL))
        xs, ffn_tail_s = _ffn_sample(xs, y, wout, _to_time_major(state_ffn_conv[i]), *w['ffn'])
        s_states.append((_to_seq_major(conv_tail, bs), _to_seq_major(ffn_tail_s, bs)))

    def stacked(states, j):
        return jnp.stack([st[j] for st in states])

    ssm_s, k_s, v_s = s_outs
    return (xp, _to_seq_major(xs, bs)) + tuple(stacked(p_states, j) for j in range(5)) + (
        ssm_s.reshape((depth, bs) + head_shape), stacked(s_states, 0),
        k_s.reshape((depth, bs, WINDOW) + kv_shape), v_s.reshape((depth, bs, WINDOW) + kv_shape),
        stacked(s_states, 1))
```

```python
import functools

import numpy as np
import jax
import jax.numpy as jnp
from jax import lax
from jax.experimental import pallas as pl
from jax.experimental.pallas import tpu as pltpu

F32 = jnp.float32
BF16 = jnp.bfloat16

D_MODEL = 1024
HEAD_DIM = 64
SSD_HEADS = 8
SSD_INNER = SSD_HEADS * HEAD_DIM
SSD_GROUPS = 2
SSD_STATE = 128
SSD_CONV = 4
SSD_CHUNK = 128
SSD_CONV_DIM = SSD_INNER + 2 * SSD_GROUPS * SSD_STATE
ATT_HEADS = 8
ATT_KV_HEADS = 2
ATT_GROUP = ATT_HEADS // ATT_KV_HEADS
ATT_WIDTH = ATT_HEADS * HEAD_DIM
KV_WIDTH = ATT_KV_HEADS * HEAD_DIM
WINDOW = 128
D_FF = 2816
FFN_CONV = 3
EPS = 1e-6
NEG = -1e30
LOG2E = 1.4426950408889634

LANES = 128
SUBLANES = 8
VMEM_LIMIT = 56 * 1024 * 1024

_Z0, _X0, _Q0, _K0, _V0, _PEND = 0, 512, 1536, 2048, 2176, 2304
ALIBI = tuple(float(2.0 ** (-8.0 * (h + 1) / ATT_HEADS)) for h in range(ATT_HEADS))


def _rms(x, w):
    ms = jnp.mean(x * x, axis=-1, keepdims=True)
    return x * lax.rsqrt(ms + EPS) * w


def _silu(x):
    return x * jax.nn.sigmoid(x)


def _softplus(x):
    return jnp.maximum(x, 0.0) + jnp.log(1.0 + jnp.exp(-jnp.abs(x)))


def _dot(a, b):
    return jnp.dot(a, b, preferred_element_type=F32)


def _dot_nt(a, b):
    return lax.dot_general(a, b, (((1,), (1,)), ((), ())), preferred_element_type=F32)


def _dot_tn(a, b):
    return lax.dot_general(a, b, (((0,), (0,)), ((), ())), preferred_element_type=F32)


def _const_spec(shape):
    nd = len(shape)
    return pl.BlockSpec(shape, lambda *_: (0,) * nd, pipeline_mode=pl.Buffered(1))


def _full_spec(shape):
    nd = len(shape)
    return pl.BlockSpec(shape, lambda *_: (0,) * nd)


_FFN_COLS = 256


def _ffn_kernel(*refs, tm, pad, shift, carry, out_proj):
    refs = list(refs)
    x_ref = refs.pop(0)
    if out_proj:
        y_ref = refs.pop(0)
        wout_ref = refs.pop(0)
    if not carry:
        prev_ref = refs.pop(0)
    n2_ref, wup_ref, cw_ref, cb_ref, wdn_ref, o_ref, tail_ref, hp_ref, act_ref = refs

    if carry:
        @pl.when(pl.program_id(1) == 0)
        def _():
            hp_ref[0:pad, :] = jnp.zeros((pad, 2 * D_FF), F32)
        x = x_ref[0]
    else:
        hp_ref[0:pad, :] = prev_ref[...]
        x = x_ref[...]

    if out_proj:
        x = x + _dot(y_ref[...].astype(BF16), wout_ref[...])
    hn = _rms(x, n2_ref[...]).astype(BF16)
    hp_ref[pad:pad + tm, :] = _dot(hn, wup_ref[...])

    for j in range(0, D_FF, _FFN_COLS):
        def conv(c0):
            cols = slice(c0, c0 + _FFN_COLS)
            acc = hp_ref[pad:pad + tm, cols] * cw_ref[2:3, cols]
            acc = acc + hp_ref[pad - shift:pad - shift + tm, cols] * cw_ref[1:2, cols]
            acc = acc + hp_ref[pad - 2 * shift:pad - 2 * shift + tm, cols] * cw_ref[0:1, cols]
            return acc + cb_ref[:, cols]
        act_ref[:, j:j + _FFN_COLS] = (_silu(conv(j)) * conv(D_FF + j)).astype(BF16)

    out = x + _dot(act_ref[...], wdn_ref[...])
    tail = hp_ref[tm + pad - 2 * shift:tm + pad, :]
    if carry:
        o_ref[0] = out
        tail_ref[0] = tail
        hp_ref[0:pad, :] = hp_ref[tm:tm + pad, :]
    else:
        o_ref[...] = out
        tail_ref[...] = tail


def _ffn_prompt(x, n2, wup, cw, cb, wdn, *, tm):
    b, s, _ = x.shape
    pad = SUBLANES
    kern = functools.partial(_ffn_kernel, tm=tm, pad=pad, shift=1, carry=True, out_proj=False)
    return pl.pallas_call(
        kern,
        grid=(b, s // tm),
        in_specs=[
            pl.BlockSpec((1, tm, D_MODEL), lambda i, t: (i, t, 0)),
            _const_spec((1, D_MODEL)),
            _const_spec((D_MODEL, 2 * D_FF)),
            _const_spec((FFN_CONV, 2 * D_FF)),
            _const_spec((1, 2 * D_FF)),
            _const_spec((D_FF, D_MODEL)),
        ],
        out_specs=[
            pl.BlockSpec((1, tm, D_MODEL), lambda i, t: (i, t, 0)),
            pl.BlockSpec((1, FFN_CONV - 1, 2 * D_FF), lambda i, t: (i, 0, 0)),
        ],
        out_shape=[
            jax.ShapeDtypeStruct((b, s, D_MODEL), F32),
            jax.ShapeDtypeStruct((b, FFN_CONV - 1, 2 * D_FF), F32),
        ],
        scratch_shapes=[
            pltpu.VMEM((tm + pad, 2 * D_FF), F32),
            pltpu.VMEM((tm, D_FF), BF16),
        ],
        compiler_params=pltpu.CompilerParams(
            dimension_semantics=("arbitrary", "arbitrary"), vmem_limit_bytes=VMEM_LIMIT),
        name="ffn_prompt",
    )(x, n2, wup, cw, cb, wdn)


def _ffn_sample(x, y, wout, prev, n2, wup, cw, cb, wdn):
    rows = x.shape[0]
    bs = prev.shape[0] // (FFN_CONV - 1)
    pad = (FFN_CONV - 1) * bs
    kern = functools.partial(_ffn_kernel, tm=rows, pad=pad, shift=bs, carry=False, out_proj=True)
    return pl.pallas_call(
        kern,
        grid=(1,),
        in_specs=[
            _const_spec((rows, D_MODEL)),
            _const_spec((rows, D_MODEL)),
            _const_spec((D_MODEL, D_MODEL)),
            _const_spec((pad, 2 * D_FF)),
            _const_spec((1, D_MODEL)),
            _const_spec((D_MODEL, 2 * D_FF)),
            _const_spec((FFN_CONV, 2 * D_FF)),
            _const_spec((1, 2 * D_FF)),
            _const_spec((D_FF, D_MODEL)),
        ],
        out_specs=[
            _full_spec((rows, D_MODEL)),
            _full_spec((pad, 2 * D_FF)),
        ],
        out_shape=[
            jax.ShapeDtypeStruct((rows, D_MODEL), F32),
            jax.ShapeDtypeStruct((pad, 2 * D_FF), F32),
        ],
        scratch_shapes=[
            pltpu.VMEM((rows + pad, 2 * D_FF), F32),
            pltpu.VMEM((rows, D_FF), BF16),
        ],
        compiler_params=pltpu.CompilerParams(
            dimension_semantics=("arbitrary",), vmem_limit_bytes=VMEM_LIMIT),
        name="ffn_sample",
    )(x, y, wout, prev, n2, wup, cw, cb, wdn)


def _tri_cumsum(x, ltri):
    hi = x.astype(BF16)
    lo = (x - hi.astype(F32)).astype(BF16)
    return _dot(ltri, hi) + _dot(ltri, lo)


def _half_variants(a, lane_lo):
    ar = pltpu.roll(a, HEAD_DIM, axis=1)
    zero = jnp.zeros_like(a)
    return {
        (0, 0): jnp.where(lane_lo, a, zero).astype(BF16),
        (1, 1): jnp.where(lane_lo, zero, a).astype(BF16),
        (1, 0): jnp.where(lane_lo, ar, zero).astype(BF16),
        (0, 1): jnp.where(lane_lo, zero, ar).astype(BF16),
    }


def _conv_silu(cbuf_ref, cw_ref, cb_ref, r0, cols):
    pad, c = SUBLANES, SSD_CHUNK
    assert SSD_CONV == 4
    xe = cbuf_ref[r0:r0 + pad + c, cols]
    x1 = pltpu.roll(xe, 1, axis=0)
    u = xe[pad:] * cw_ref[3:4, cols] + x1[pad:] * cw_ref[2:3, cols]
    v = xe * cw_ref[1:2, cols] + x1 * cw_ref[0:1, cols]
    return _silu(u + pltpu.roll(v, 2, axis=0)[pad:] + cb_ref[:, cols])


_CHUNK_WEIGHTS = (5.0, 6.0, 5.0) + (1.5,) * 4 + (0.5,) * 4 + (1.0,)


def _chunk_pieces(r0, bufs, kprev, vprev, first_mask, bias_ref, st_ref, sink_ref, cw_ref, cb_ref, a, dskip, snorm,
                  qnw, knw, ltri, causal, lane_lo):
    proj_ref, cbuf_ref, dt_ref, y_ref = bufs
    c = SSD_CHUNK
    rows = slice(r0, r0 + c)
    gw = SSD_INNER // SSD_GROUPS
    qo, ko, vo = _Q0 - SSD_CONV_DIM, _K0 - SSD_CONV_DIM, _V0 - SSD_CONV_DIM

    dt_c = dt_ref[rows, :]
    acum_c = _tri_cumsum(dt_c * a, ltri)
    bc = _conv_silu(cbuf_ref, cw_ref, cb_ref, r0, slice(SSD_INNER, SSD_CONV_DIM))
    bms = [bc[:, g * SSD_STATE:(g + 1) * SSD_STATE].astype(BF16) for g in range(SSD_GROUPS)]
    cms = [bc[:, (SSD_GROUPS + g) * SSD_STATE:(SSD_GROUPS + g + 1) * SSD_STATE].astype(BF16)
           for g in range(SSD_GROUPS)]
    cbs = [_dot_nt(cms[g], bms[g]) for g in range(SSD_GROUPS)]
    yoffs = [_dot(cms[g], st_ref[:, g * gw:(g + 1) * gw].astype(BF16)) for g in range(SSD_GROUPS)]
    yield

    q = proj_ref[rows, qo:qo + ATT_WIDTH]
    kn = _norm_k(proj_ref[rows, ko:ko + KV_WIDTH], knw, lane_lo)
    v = proj_ref[rows, vo:vo + KV_WIDTH]
    kcur = _half_variants(kn, lane_lo)
    vcur = _half_variants(v, lane_lo)
    qn = []
    for m in range(ATT_HEADS // 2):
        qp = q[:, m * LANES:(m + 1) * LANES]
        q2p = qp * qp
        rs_lo = lax.rsqrt(jnp.sum(jnp.where(lane_lo, q2p, 0.0), axis=-1, keepdims=True) / HEAD_DIM + EPS)
        rs_hi = lax.rsqrt(jnp.sum(jnp.where(lane_lo, 0.0, q2p), axis=-1, keepdims=True) / HEAD_DIM + EPS)
        qn.append((qp * qnw[:, m * LANES:(m + 1) * LANES]
                   * (jnp.where(lane_lo, rs_lo, rs_hi) * (HEAD_DIM ** -0.5 * LOG2E))).astype(BF16))
    scores = {}
    for kv in range(ATT_KV_HEADS):
        qs = jnp.concatenate([qn[2 * kv], qn[2 * kv + 1]], axis=0)
        for e in range(2):
            sr = _dot_nt(qs, jnp.concatenate([kprev[(kv, e)], kcur[(kv, e)]], axis=0))
            scores[ATT_GROUP * kv + e] = sr[0:c]
            scores[ATT_GROUP * kv + 2 + e] = sr[c:2 * c]
    vall = {key: jnp.concatenate([vprev[key], vcur[key]], axis=0) for key in vcur}
    yield

    acols = [jnp.broadcast_to(acum_c[:, h:h + 1], (c, LANES)) for h in range(SSD_HEADS)]
    dcols = [jnp.broadcast_to(dt_c[:, h:h + 1], (c, LANES)) for h in range(SSD_HEADS)]
    acum = jnp.concatenate([jnp.where(lane_lo, acols[2 * m], acols[2 * m + 1]) for m in range(SSD_HEADS // 2)],
                           axis=1)
    dt = jnp.concatenate([jnp.where(lane_lo, dcols[2 * m], dcols[2 * m + 1]) for m in range(SSD_HEADS // 2)],
                         axis=1)
    xs = _conv_silu(cbuf_ref, cw_ref, cb_ref, r0, slice(0, SSD_INNER))
    alast = acum[c - 1:c, :]
    xw = (xs * (jnp.exp(alast - acum) * dt)).astype(BF16)
    cdec = jnp.exp(alast)
    for g in range(SSD_GROUPS):
        st_ref[:, g * gw:(g + 1) * gw] = (st_ref[:, g * gw:(g + 1) * gw] * cdec[:, g * gw:(g + 1) * gw]
                                          + _dot_tn(bms[g], xw[:, g * gw:(g + 1) * gw]))
    yield
    eac = jnp.exp(acum)
    xdt = xs * dt
    acum_t = acum_c.T
    ys = []
    for pm in range(SSD_INNER // LANES):
        g, m = divmod(pm, gw // LANES)
        lanes = slice(pm * LANES, (pm + 1) * LANES)
        ms = []
        for e in range(2):
            h = 2 * pm + e
            rowb = jnp.broadcast_to(acum_t[h:h + 1, :], (c, c))
            dec = jnp.exp(jnp.where(causal, acols[h] - rowb, NEG))
            ms.append((cbs[g] * dec).astype(BF16))
        xd = xdt[:, lanes]
        rhs = jnp.concatenate([jnp.where(lane_lo, xd, 0.0), jnp.where(lane_lo, 0.0, xd)], axis=0).astype(BF16)
        yield
        ydiag = _dot(jnp.concatenate(ms, axis=1), rhs)
        ys.append(ydiag + yoffs[g][:, m * LANES:(m + 1) * LANES] * eac[:, lanes])

    outs = []
    for m in range(ATT_HEADS // 2):
        kv = (2 * m) // ATT_GROUP
        ps, invs = [], []
        for e in range(2):
            h = 2 * m + e
            s = scores[h] + bias_ref[h]
            if first_mask is not None:
                s = jnp.where(first_mask, s, NEG)
            sk = sink_ref[h] * LOG2E
            mx = jnp.maximum(jnp.max(s, axis=-1, keepdims=True), sk)
            p = jnp.exp2(s - mx)
            den = jnp.sum(p, axis=-1, keepdims=True) + jnp.exp2(sk - mx)
            ps.append(p.astype(BF16))
            invs.append(1.0 / den)
        yield
        o = _dot(jnp.concatenate(ps, axis=1), jnp.concatenate([vall[(kv, 0)], vall[(kv, 1)]], axis=0))
        outs.append(o * jnp.where(lane_lo, invs[0], invs[1]))

    y = (jnp.concatenate(ys, axis=1) + xs * dskip) * _silu(proj_ref[rows, _Z0:_Z0 + SSD_INNER])
    y_ref[rows, 0:SSD_INNER] = jnp.concatenate(
        [_rms(y[:, g * gw:(g + 1) * gw], snorm[:, g * gw:(g + 1) * gw]) for g in range(SSD_GROUPS)],
        axis=1).astype(BF16)
    y_ref[rows, SSD_INNER:] = jnp.concatenate(outs, axis=1).astype(BF16)
    yield
    return kn, v, kcur, vcur


def _norm_k(k, knw, lane_lo):
    k2 = k * k
    r0 = lax.rsqrt(jnp.sum(jnp.where(lane_lo, k2, 0.0), axis=-1, keepdims=True) / HEAD_DIM + EPS)
    r1 = lax.rsqrt(jnp.sum(jnp.where(lane_lo, 0.0, k2), axis=-1, keepdims=True) / HEAD_DIM + EPS)
    return k * jnp.where(lane_lo, r0, r1) * knw


def _weave(main, weights, side):
    done = 0
    total = float(sum(weights))
    acc = 0.0
    for k in range(len(weights) + 1):
        try:
            next(main)
        except StopIteration as stop:
            for thunk in side[done:]:
                thunk()
            return stop.value
        acc += weights[k]
        while done < len(side) and done * total < acc * len(side):
            side[done]()
            done += 1
    raise AssertionError("main generator has more pieces than declared")


_PROJ_COLS = 256


def _project_pieces(load_x, n1_ref, win_ref, wdt_ref, dtb_ref, bufs, tm):
    proj_ref, cbuf_ref, dt_ref, _ = bufs
    pad = SUBLANES
    box = {}

    def norm():
        box['hn'] = _rms(load_x(), n1_ref[...]).astype(BF16)

    def slab(c0, c1):
        def run():
            res = _dot(box['hn'], win_ref[:, c0:c1])
            if c0 < _X0:
                proj_ref[:, c0:c1] = res
            elif c0 < _Q0:
                cbuf_ref[pad:pad + tm, c0 - _X0:c1 - _X0] = res
            else:
                proj_ref[:, c0 - SSD_CONV_DIM:c1 - SSD_CONV_DIM] = res
        return run

    def step_sizes():
        dt_ref[...] = _softplus(_dot(box['hn'], wdt_ref[...]) + dtb_ref[...])

    slabs = []
    for seg0, seg1 in ((_Z0, _X0), (_X0, _Q0), (_Q0, _PEND)):
        for c0 in range(seg0, seg1, _PROJ_COLS):
            slabs.append(slab(c0, min(c0 + _PROJ_COLS, seg1)))
    return [norm] + slabs + [step_sizes]


def _outproj_pieces(load_x, load_y, wout_ref, store):
    def slab(c0, c1):
        def run():
            store(c0, c1, load_x(c0, c1) + _dot(load_y(), wout_ref[:, c0:c1]))
        return run
    return [slab(c0, c0 + 2 * LANES) for c0 in range(0, D_MODEL, 2 * LANES)]


def _score_bias(bias_ref):
    c = SSD_CHUNK
    qi = lax.broadcasted_iota(jnp.int32, (c, 2 * c), 0)
    sj = lax.broadcasted_iota(jnp.int32, (c, 2 * c), 1)
    reli = qi + c - sj
    rel = reli.astype(F32)
    in_window = (reli >= 0) & (reli < WINDOW)
    for h in range(ATT_HEADS):
        bias_ref[h] = jnp.where(in_window, (-ALIBI[h] * LOG2E) * rel, NEG)


def _mix_tile(bufs, first_col, kprev, vprev, st_ref, bias_ref, sink_ref, cw_ref, cb_ref, alog_ref, dskip_ref,
              snorm_ref, qnw_ref, knw_ref, tm):
    c = SSD_CHUNK
    row = lax.broadcasted_iota(jnp.int32, (c, c), 0)
    col = lax.broadcasted_iota(jnp.int32, (c, c), 1)
    causal = row >= col
    ltri = jnp.where(causal, 1.0, 0.0).astype(BF16)
    lane_lo = col < HEAD_DIM
    a = -jnp.exp(alog_ref[...])

    kn = v = None
    for ci in range(tm // c):
        first_mask = None
        if ci == 0 and first_col is not None:
            first_mask = lax.broadcasted_iota(jnp.int32, (c, 2 * c), 1) >= first_col
        kn, v, kprev, vprev = yield from _chunk_pieces(
            ci * c, bufs, kprev, vprev, first_mask, bias_ref, st_ref, sink_ref, cw_ref, cb_ref, a, dskip_ref[...],
            snorm_ref[...], qnw_ref[...], knw_ref[...], ltri, causal, lane_lo)
    return kn, v, kprev, vprev


_MIX_SEQS = 2
_SCRATCH_PER_SEQ = 11


def _round_robin(gens):
    results = [None] * len(gens)
    live = list(range(len(gens)))
    while live:
        for j in list(live):
            try:
                next(gens[j])
            except StopIteration as stop:
                results[j] = stop.value
                live.remove(j)
                continue
            yield
    return results


def _interleaved(lists):
    return [item for group in zip(*lists) for item in group]


def _mixer_prompt_kernel(sink_ref, xcur_ref, xnext_ref, n1_ref, win_ref, wdt_ref, cw_ref, cb_ref, dtb_ref, alog_ref,
                         dskip_ref, snorm_ref, qnw_ref, knw_ref, wout_ref,
                         o_ref, ssm_ref, conv_ref, kc_ref, vc_ref, *scratch, tm):
    i = pl.program_id(0)
    s = pl.program_id(1)
    ns = pl.num_programs(1)
    pad = SUBLANES
    nq = _MIX_SEQS
    bias_ref = scratch[-1]
    seqs = []
    for q in range(nq):
        r = scratch[q * _SCRATCH_PER_SEQ:(q + 1) * _SCRATCH_PER_SEQ]
        seqs.append(dict(a=tuple(r[0:4]), b=tuple(r[4:8]), st=r[8], kprev=r[9], vprev=r[10]))
    proj_args = (n1_ref, win_ref, wdt_ref, dtb_ref)

    def mix_args(q):
        return (seqs[q]['st'], bias_ref, sink_ref, cw_ref, cb_ref, alog_ref, dskip_ref, snorm_ref, qnw_ref, knw_ref,
                tm)

    @pl.when((i == 0) & (s == 0))
    def _():
        _score_bias(bias_ref)
        for q in range(nq):
            for piece in _project_pieces(lambda q=q: xcur_ref[q, 0:tm, :], *proj_args, seqs[q]['a'], tm):
                piece()

    @pl.when(s == 0)
    def _():
        for q in range(nq):
            seqs[q]['st'][...] = jnp.zeros(seqs[q]['st'].shape, F32)
            seqs[q]['a'][1][0:pad, :] = jnp.zeros((pad, SSD_CONV_DIM), F32)
            seqs[q]['kprev'][...] = jnp.zeros(seqs[q]['kprev'].shape, F32)
            seqs[q]['vprev'][...] = jnp.zeros(seqs[q]['vprev'].shape, F32)

    @pl.when(s > 0)
    def _():
        for q in range(nq):
            seqs[q]['a'][1][0:pad, :] = seqs[q]['b'][1][tm:tm + pad, :]

    lane_lo = lax.broadcasted_iota(jnp.int32, (SSD_CHUNK, LANES), 1) < HEAD_DIM
    first_col = jnp.where(s == 0, SSD_CHUNK, 0)
    weights = tuple(w for w in _CHUNK_WEIGHTS * (tm // SSD_CHUNK) for _ in range(nq))

    def outproj(q, r0, y_ref):
        def store(c0, c1, val):
            o_ref[q, r0:r0 + tm, c0:c1] = val
        return _outproj_pieces(lambda c0, c1: xcur_ref[q, r0:r0 + tm, c0:c1], lambda: y_ref[...], wout_ref, store)

    side = _interleaved([_project_pieces(lambda q=q: xcur_ref[q, tm:2 * tm, :], *proj_args, seqs[q]['b'], tm)
                         for q in range(nq)])
    res_a = _weave(_round_robin(
        [_mix_tile(seqs[q]['a'], first_col, _half_variants(seqs[q]['kprev'][...], lane_lo),
                   _half_variants(seqs[q]['vprev'][...], lane_lo), *mix_args(q)) for q in range(nq)]), weights, side)
    for q in range(nq):
        seqs[q]['b'][1][0:pad, :] = seqs[q]['a'][1][tm:tm + pad, :]
    side = _interleaved([outproj(q, 0, seqs[q]['a'][3])
                         + _project_pieces(lambda q=q: xnext_ref[q], *proj_args, seqs[q]['a'], tm) for q in range(nq)])
    res_b = _weave(_round_robin(
        [_mix_tile(seqs[q]['b'], None, res_a[q][2], res_a[q][3], *mix_args(q)) for q in range(nq)]), weights, side)
    for piece in _interleaved([outproj(q, tm, seqs[q]['b'][3]) for q in range(nq)]):
        piece()
    for q in range(nq):
        seqs[q]['kprev'][...] = res_b[q][0]
        seqs[q]['vprev'][...] = res_b[q][1]

    @pl.when(s == ns - 1)
    def _():
        for q in range(nq):
            ssm_ref[q] = seqs[q]['st'][...].T
            conv_ref[q] = seqs[q]['b'][1][tm + pad - (SSD_CONV - 1):tm + pad, :]
            kc_ref[q] = res_b[q][0]
            vc_ref[q] = res_b[q][1]


def _mixer_prompt(x, sinks, n1, win, wdt, cw, cb, dtb, alog, dskip, snorm, qnw, knw, wout, *, tm):
    b, s, _ = x.shape
    pw = _PEND - SSD_CONV_DIM
    nt = s // tm
    ns = nt // 2
    kern = functools.partial(_mixer_prompt_kernel, tm=tm)
    consts = [n1, win, wdt, cw, cb, dtb, alog, dskip, snorm, qnw, knw, wout]

    nq = _MIX_SEQS
    assert b % nq == 0 and nt % 2 == 0
    groups = b // nq

    def next_tile(i, t, *_):
        flat = jnp.minimum((i * ns + t) * 2 + 2, groups * nt - 1)
        return (flat // nt, flat % nt, 0)

    bufset = [
        pltpu.VMEM((tm, pw), F32),
        pltpu.VMEM((tm + SUBLANES, SSD_CONV_DIM), F32),
        pltpu.VMEM((tm, LANES), F32),
        pltpu.VMEM((tm, D_MODEL), BF16),
    ]
    per_seq = bufset + bufset + [
        pltpu.VMEM((SSD_STATE, SSD_INNER), F32),
        pltpu.VMEM((WINDOW, KV_WIDTH), F32),
        pltpu.VMEM((WINDOW, KV_WIDTH), F32),
    ]
    assert len(per_seq) == _SCRATCH_PER_SEQ
    return pl.pallas_call(
        kern,
        grid_spec=pltpu.PrefetchScalarGridSpec(
            num_scalar_prefetch=1,
            grid=(groups, ns),
            in_specs=[pl.BlockSpec((nq, 2 * tm, D_MODEL), lambda i, t, *_: (i, t, 0)),
                      pl.BlockSpec((nq, tm, D_MODEL), next_tile)]
            + [_const_spec(w.shape) for w in consts],
            out_specs=[
                pl.BlockSpec((nq, 2 * tm, D_MODEL), lambda i, t, *_: (i, t, 0)),
                pl.BlockSpec((nq, SSD_INNER, SSD_STATE), lambda i, t, *_: (i, 0, 0)),
                pl.BlockSpec((nq, SSD_CONV - 1, SSD_CONV_DIM), lambda i, t, *_: (i, 0, 0)),
                pl.BlockSpec((nq, WINDOW, KV_WIDTH), lambda i, t, *_: (i, 0, 0)),
                pl.BlockSpec((nq, WINDOW, KV_WIDTH), lambda i, t, *_: (i, 0, 0)),
            ],
            scratch_shapes=per_seq * nq + [pltpu.VMEM((ATT_HEADS, SSD_CHUNK, 2 * SSD_CHUNK), F32)],
        ),
        out_shape=[
            jax.ShapeDtypeStruct((b, s, D_MODEL), F32),
            jax.ShapeDtypeStruct((b, SSD_INNER, SSD_STATE), F32),
            jax.ShapeDtypeStruct((b, SSD_CONV - 1, SSD_CONV_DIM), F32),
            jax.ShapeDtypeStruct((b, WINDOW, KV_WIDTH), F32),
            jax.ShapeDtypeStruct((b, WINDOW, KV_WIDTH), F32),
        ],
        compiler_params=pltpu.CompilerParams(
            dimension_semantics=("arbitrary", "arbitrary"), vmem_limit_bytes=VMEM_LIMIT),
        name="mixer_prompt",
    )(sinks, x, x, *consts)


_S_Z0, _S_X0, _S_DT0, _S_Q0, _S_K0, _S_V0, _S_END = 0, 512, 1536, 2048, 2560, 2688, 2816


def _sample_in_kernel(x_ref, prev_ref, n1_ref, win_ref, wdt_ref, cw_ref, cb_ref, dtb_ref, o_ref, tail_ref, cbuf_ref,
                      *, rows, bs):
    pad = (SSD_CONV - 1) * bs
    hn = _rms(x_ref[...], n1_ref[...]).astype(BF16)
    o_ref[:, _S_Z0:_S_X0] = _dot(hn, win_ref[:, _Z0:_X0])
    cbuf_ref[0:pad, :] = prev_ref[...]
    cbuf_ref[pad:pad + rows, :] = _dot(hn, win_ref[:, _X0:_Q0])
    acc = cb_ref[...] + cbuf_ref[pad:pad + rows, :] * cw_ref[SSD_CONV - 1:SSD_CONV, :]
    for i in range(1, SSD_CONV):
        acc = acc + cbuf_ref[pad - i * bs:pad - i * bs + rows, :] * cw_ref[SSD_CONV - 1 - i:SSD_CONV - i, :]
    o_ref[:, _S_X0:_S_DT0] = _silu(acc)
    o_ref[:, _S_DT0:_S_Q0] = _softplus(_dot(hn, wdt_ref[...]) + dtb_ref[...])
    o_ref[:, _S_Q0:_S_END] = _dot(hn, win_ref[:, _Q0:_PEND])
    tail_ref[...] = cbuf_ref[rows:rows + pad, :]


def _sample_in(x, prev, n1, win, wdt, cw, cb, dtb):
    rows = x.shape[0]
    pad = prev.shape[0]
    bs = pad // (SSD_CONV - 1)
    args = (x, prev, n1, win, wdt, cw, cb, dtb)
    return pl.pallas_call(
        functools.partial(_sample_in_kernel, rows=rows, bs=bs),
        grid=(1,),
        in_specs=[_const_spec(a.shape) for a in args],
        out_specs=[_full_spec((rows, _S_END)), _full_spec((pad, SSD_CONV_DIM))],
        out_shape=[jax.ShapeDtypeStruct((rows, _S_END), F32), jax.ShapeDtypeStruct((pad, SSD_CONV_DIM), F32)],
        scratch_shapes=[pltpu.VMEM((rows + pad, SSD_CONV_DIM), F32)],
        compiler_params=pltpu.CompilerParams(dimension_semantics=("arbitrary",), vmem_limit_bytes=VMEM_LIMIT),
        name="sample_in",
    )(*args)


_SAMPLE_TILES = 8


def _sample_mix_kernel(sink_ref, p_ref, st_ref, kc_ref, vc_ref, alog_ref, dskip_ref, snorm_ref, qnw_ref, knw_ref,
                       *rest, tlen, layer, all_layers):
    y_ref, sto_ref, ko_ref, vo_ref = rest[-4:]
    if all_layers:
        for ref in (sto_ref, ko_ref, vo_ref):
            for other in range(ref.shape[0]):
                if other != layer:
                    ref[other] = jnp.zeros(ref.shape[1:], F32)
        sto_ref, ko_ref, vo_ref = sto_ref.at[layer], ko_ref.at[layer], vo_ref.at[layer]
    nseq = SUBLANES // tlen
    for j in range(_SAMPLE_TILES):
        rows = pl.ds(j * SUBLANES, SUBLANES)
        seqs = pl.ds(j * nseq, nseq)
        _sample_mix_tile(sink_ref, p_ref.at[rows], st_ref.at[seqs], kc_ref.at[seqs], vc_ref.at[seqs], alog_ref,
                         dskip_ref, snorm_ref, qnw_ref, knw_ref, y_ref.at[rows], sto_ref.at[seqs], ko_ref.at[seqs],
                         vo_ref.at[seqs], tlen)


def _sample_mix_tile(sink_ref, p_ref, st_ref, kc_ref, vc_ref, alog_ref, dskip_ref, snorm_ref, qnw_ref, knw_ref,
                     y_ref, sto_ref, ko_ref, vo_ref, tlen):
    nseq = SUBLANES // tlen
    w = WINDOW
    pk = p_ref[...]
    z = pk[:, _S_Z0:_S_X0]
    xs = pk[:, _S_X0:_S_X0 + SSD_INNER]
    dt = pk[:, _S_DT0:_S_Q0]
    q = pk[:, _S_Q0:_S_K0]
    k = pk[:, _S_K0:_S_V0]
    v = pk[:, _S_V0:_S_END]
    gw = SSD_INNER // SSD_GROUPS
    bms = [pk[:, _S_X0 + SSD_INNER + g * SSD_STATE:_S_X0 + SSD_INNER + (g + 1) * SSD_STATE]
           for g in range(SSD_GROUPS)]
    c0 = _S_X0 + SSD_INNER + SSD_GROUPS * SSD_STATE
    cms = [pk[:, c0 + g * SSD_STATE:c0 + (g + 1) * SSD_STATE] for g in range(SSD_GROUPS)]

    rowi = lax.broadcasted_iota(jnp.int32, (SUBLANES, SSD_INNER), 0)
    assert tlen & (tlen - 1) == 0 and SUBLANES % tlen == 0
    tshift = tlen.bit_length() - 1
    sshift = SUBLANES.bit_length() - 1
    tpos = rowi & (tlen - 1)
    seq = rowi >> tshift
    seq_g = lax.broadcasted_iota(jnp.int32, (SUBLANES, gw), 0) >> tshift
    lane_lo = lax.broadcasted_iota(jnp.int32, (SUBLANES, LANES), 1) < HEAD_DIM

    dta = dt * (-jnp.exp(alog_ref[...]))
    acum = dta
    for s in range(1, tlen):
        acum = acum + jnp.where(tpos >= s, pltpu.roll(dta, s, axis=0), 0.0)
    tot = acum[tlen - 1:tlen, :]
    for b in range(1, nseq):
        tot = jnp.where(seq == b, acum[(b + 1) * tlen - 1:(b + 1) * tlen, :], tot)
    eac = jnp.exp(acum)
    cdec = jnp.exp(tot)
    xdt = xs * dt
    xw = xs * (jnp.exp(tot - acum) * dt)

    ydiag = jnp.zeros((SUBLANES, SSD_INNER), F32)
    for s in range(tlen):
        xsh = pltpu.roll(xdt, s, axis=0) if s else xdt
        ash = pltpu.roll(acum, s, axis=0) if s else acum
        cbs = []
        for g in range(SSD_GROUPS):
            bsh = pltpu.roll(bms[g], s, axis=0) if s else bms[g]
            cbs.append(jnp.broadcast_to(jnp.sum(cms[g] * bsh, axis=-1, keepdims=True), (SUBLANES, gw)))
        term = jnp.concatenate(cbs, axis=1) * jnp.exp(acum - ash) * xsh
        ydiag = ydiag + jnp.where(tpos >= s, term, 0.0)

    yoff = []
    for g in range(SSD_GROUPS):
        cm = cms[g].astype(BF16)
        bm = bms[g].astype(BF16)
        yo = None
        for b in range(nseq):
            h0 = st_ref[b, g * gw:(g + 1) * gw, :]
            yb = _dot_nt(cm, h0.astype(BF16))
            yo = yb if yo is None else jnp.where(seq_g == b, yb, yo)
            xwb = jnp.where(seq_g == b, xw[:, g * gw:(g + 1) * gw], 0.0).astype(BF16)
            cseq = jnp.broadcast_to(cdec[b * tlen:b * tlen + 1, :], (SUBLANES, SSD_INNER))
            blocks = []
            for hh in range(gw // HEAD_DIM):
                pm = (g * gw + hh * HEAD_DIM) // LANES
                cpair = cseq[:, pm * LANES:(pm + 1) * LANES]
                croll = pltpu.roll(cpair, HEAD_DIM, axis=1)
                chead = jnp.where(lane_lo, cpair, croll) if hh % 2 == 0 else jnp.where(lane_lo, croll, cpair)
                blocks += [chead] * (HEAD_DIM // SUBLANES)
            sto_ref[b, g * gw:(g + 1) * gw, :] = h0 * jnp.concatenate(blocks, axis=0) + _dot_tn(xwb, bm)
        yoff.append(yo)
    y = (ydiag + jnp.concatenate(yoff, axis=1) * eac + xs * dskip_ref[...]) * _silu(z)
    y_ssd = jnp.concatenate(
        [_rms(y[:, g * gw:(g + 1) * gw], snorm_ref[:, g * gw:(g + 1) * gw]) for g in range(SSD_GROUPS)], axis=1)

    kn = _norm_k(k, knw_ref[...], lane_lo)
    for b in range(nseq):
        ko_ref[b, 0:w - tlen, :] = kc_ref[b, tlen:w, :]
        vo_ref[b, 0:w - tlen, :] = vc_ref[b, tlen:w, :]
        ko_ref[b, w - tlen:w, :] = kn[b * tlen:(b + 1) * tlen, :]
        vo_ref[b, w - tlen:w, :] = v[b * tlen:(b + 1) * tlen, :]
    qw = q * qnw_ref[...]
    q2 = q * q
    pieces = []
    for h in range(ATT_HEADS):
        kv = h // ATT_GROUP
        lanes = slice((h // 2) * LANES, (h // 2 + 1) * LANES)
        own = lane_lo if h % 2 == 0 else jnp.logical_not(lane_lo)
        ssq = jnp.sum(jnp.where(own, q2[:, lanes], 0.0), axis=-1, keepdims=True)
        qh = jnp.where(own, qw[:, lanes], 0.0) * (lax.rsqrt(ssq / HEAD_DIM + EPS) * (HEAD_DIM ** -0.5))
        pieces.append(qh if h % 2 == kv else pltpu.roll(qh, HEAD_DIM, axis=1))
    qrows = jnp.concatenate(pieces, axis=0).astype(BF16)

    nq = ATT_HEADS * SUBLANES
    ncol = w + SUBLANES
    ri = lax.broadcasted_iota(jnp.int32, (nq, ncol), 0)
    cj = lax.broadcasted_iota(jnp.int32, (nq, ncol), 1)
    rt = ri & (tlen - 1)
    rseq = (ri & (SUBLANES - 1)) >> tshift
    rhead = ri >> sshift
    nj = jnp.maximum(cj - w, 0)
    in_cache = cj < w
    rel = jnp.where(in_cache, rt + w - cj, rt - (nj & (tlen - 1))).astype(F32)
    ok_new = ((nj >> tshift) == rseq) & ((nj & (tlen - 1)) <= rt)
    ok = (in_cache & (cj > rt)) | (jnp.logical_not(in_cache) & ok_new)
    slope = jnp.zeros((nq, ncol), F32)
    for h in range(ATT_HEADS):
        slope = jnp.where(rhead == h, ALIBI[h], slope)
    rh1 = lax.broadcasted_iota(jnp.int32, (nq, 1), 0) >> sshift
    sk = jnp.zeros((nq, 1), F32)
    for h in range(ATT_HEADS):
        sk = jnp.where(rh1 == h, sink_ref[h], sk)
    bias = slope * rel
    rseq1 = (lax.broadcasted_iota(jnp.int32, (nq, LANES), 0) & (SUBLANES - 1)) >> tshift

    out = jnp.zeros((nq, LANES), F32)
    for b in range(nseq):
        kc = jnp.concatenate([kc_ref[b], kn], axis=0).astype(BF16)
        vc = jnp.concatenate([vc_ref[b], v], axis=0).astype(BF16)
        s = jnp.where(ok, _dot_nt(qrows, kc) - bias, NEG)
        mx = jnp.maximum(jnp.max(s, axis=-1, keepdims=True), sk)
        p = jnp.exp(s - mx)
        den = jnp.sum(p, axis=-1, keepdims=True) + jnp.exp(sk - mx)
        o = _dot(p.astype(BF16), vc) / den
        out = jnp.where(rseq1 == b, o, out)
    pairs = []
    for m in range(ATT_HEADS // 2):
        halves = []
        for e in range(2):
            h = 2 * m + e
            blk = out[h * SUBLANES:(h + 1) * SUBLANES, :]
            halves.append(blk if h // ATT_GROUP == e else pltpu.roll(blk, HEAD_DIM, axis=1))
        pairs.append(jnp.where(lane_lo, halves[0], halves[1]))
    y_ref[...] = jnp.concatenate([y_ssd] + pairs, axis=1)


def _sample_mix(layer, packed, sinks, st, kc, vc, alog, dskip, snorm, qnw, knw, prev_outs, *, tlen):
    rows = packed.shape[0]
    nseq = SUBLANES // tlen
    consts = [alog, dskip, snorm, qnw, knw]
    carried = [] if prev_outs is None else list(prev_outs)
    n_in = 1 + 4 + len(consts)

    step_rows = SUBLANES * _SAMPLE_TILES
    step_seqs = nseq * _SAMPLE_TILES
    assert rows % step_rows == 0

    def slab(shape):
        return pl.BlockSpec((None, step_seqs) + shape, lambda i, *_: (layer, i, 0, 0))

    def out_slab(shape):
        if prev_outs is None:
            return pl.BlockSpec((st.shape[0], step_seqs) + shape, lambda i, *_: (0, i, 0, 0))
        return slab(shape)

    return pl.pallas_call(
        functools.partial(_sample_mix_kernel, tlen=tlen, layer=layer, all_layers=prev_outs is None),
        grid_spec=pltpu.PrefetchScalarGridSpec(
            num_scalar_prefetch=1,
            grid=(rows // step_rows,),
            in_specs=[
                pl.BlockSpec((step_rows, _S_END), lambda i, *_: (i, 0)),
                slab((SSD_INNER, SSD_STATE)), slab((WINDOW, KV_WIDTH)), slab((WINDOW, KV_WIDTH)),
            ] + [_const_spec(a.shape) for a in consts] + [pl.BlockSpec(memory_space=pl.ANY)] * len(carried),
            out_specs=[
                pl.BlockSpec((step_rows, D_MODEL), lambda i, *_: (i, 0)),
                out_slab((SSD_INNER, SSD_STATE)), out_slab((WINDOW, KV_WIDTH)), out_slab((WINDOW, KV_WIDTH)),
            ],
        ),
        out_shape=[
            jax.ShapeDtypeStruct((rows, D_MODEL), F32),
            jax.ShapeDtypeStruct(st.shape, F32),
            jax.ShapeDtypeStruct(kc.shape, F32),
            jax.ShapeDtypeStruct(vc.shape, F32),
        ],
        input_output_aliases={n_in + j: 1 + j for j in range(len(carried))},
        compiler_params=pltpu.CompilerParams(dimension_semantics=("arbitrary",), vmem_limit_bytes=VMEM_LIMIT),
        name="sample_mix",
    )(sinks, packed, st, kc, vc, *consts, *carried)


def _prep_layer(p, i):
    w_in = p['w_in'][i]
    s0, s1, s2, s3, s4 = (SSD_INNER, SSD_INNER + SSD_CONV_DIM, SSD_INNER + SSD_CONV_DIM + SSD_HEADS,
                          SSD_INNER + SSD_CONV_DIM + SSD_HEADS + ATT_WIDTH,
                          SSD_INNER + SSD_CONV_DIM + SSD_HEADS + ATT_WIDTH + KV_WIDTH)
    win = jnp.concatenate([w_in[:, :s1], w_in[:, s2:]], axis=1).astype(BF16)
    wdt = jnp.repeat(w_in[:, s1:s2], HEAD_DIM, axis=1).astype(BF16)

    def per_head(v):
        return jnp.repeat(v, HEAD_DIM)[None, :]

    def lane_padded(v):
        return jnp.pad(v, [(0, 0)] * (v.ndim - 1) + [(0, LANES - v.shape[-1])])

    mixer = (
        p['attn_sinks'][i],
        p['norm1_w'][i][None, :], win, wdt, p['ssd_conv_w'][i], p['ssd_conv_b'][i][None, :],
        per_head(p['dt_bias'][i]), per_head(p['a_log'][i]), per_head(p['d_skip'][i]), p['ssd_norm_w'][i][None, :],
        jnp.tile(p['q_norm_w'][i], ATT_HEADS)[None, :], jnp.tile(p['k_norm_w'][i], ATT_KV_HEADS)[None, :],
        p['w_out'][i].astype(BF16),
    )
    compact = (lane_padded(w_in[:, s1:s2]).astype(BF16), lane_padded(p['dt_bias'][i])[None, :],
               lane_padded(p['a_log'][i])[None, :])
    ffn = (
        p['norm2_w'][i][None, :], p['w_up'][i].astype(BF16), p['ffn_conv_w'][i], p['ffn_conv_b'][i][None, :],
        p['w_down'][i].astype(BF16),
    )
    return {'mixer': mixer, 'compact': compact, 'ffn': ffn}


_TM_MIXER = 256
_TM_FFN = 512


def _to_time_major(a):
    bs, t, c = a.shape
    return a.transpose(1, 0, 2).reshape(t * bs, c)


def _to_seq_major(a, bs):
    return a.reshape(a.shape[0] // bs, bs, a.shape[1]).transpose(1, 0, 2)


def kernel(x_prompt, x_sample, state_ssm, state_ssd_conv, cache_swa_k, cache_swa_v, state_ffn_conv, norm1_w, w_in,
           ssd_conv_w, ssd_conv_b, dt_bias, a_log, d_skip, ssd_norm_w, q_norm_w, k_norm_w, attn_sinks, w_out, norm2_w,
           w_up, ffn_conv_w, ffn_conv_b, w_down):
    p = dict(norm1_w=norm1_w, w_in=w_in, ssd_conv_w=ssd_conv_w, ssd_conv_b=ssd_conv_b, dt_bias=dt_bias, a_log=a_log,
             d_skip=d_skip, ssd_norm_w=ssd_norm_w, q_norm_w=q_norm_w, k_norm_w=k_norm_w, attn_sinks=attn_sinks,
             w_out=w_out, norm2_w=norm2_w, w_up=w_up, ffn_conv_w=ffn_conv_w, ffn_conv_b=ffn_conv_b, w_down=w_down)
    depth = w_in.shape[0]
    b = x_prompt.shape[0]
    bs, tlen, _ = x_sample.shape
    assert cache_swa_k.shape[2] == WINDOW
    head_shape = (SSD_HEADS, HEAD_DIM, SSD_STATE)
    kv_shape = (ATT_KV_HEADS, HEAD_DIM)

    xp = x_prompt
    xs = _to_time_major(x_sample)
    ssm_all = state_ssm.reshape(depth, bs, SSD_INNER, SSD_STATE)
    kc_all = cache_swa_k.reshape(depth, bs, WINDOW, KV_WIDTH)
    vc_all = cache_swa_v.reshape(depth, bs, WINDOW, KV_WIDTH)
    conv_prev_all = state_ssd_conv.transpose(0, 2, 1, 3).reshape(depth, (SSD_CONV - 1) * bs, SSD_CONV_DIM)
    ffn_prev_all = state_ffn_conv.transpose(0, 2, 1, 3).reshape(depth, (FFN_CONV - 1) * bs, 2 * D_FF)
    s_outs = None
    p_states, s_states = [], []
    for i in range(depth):
        w = _prep_layer(p, i)
        sinks, n1, win, wdt, cw, cb, dtb, alog, dskip, snorm, qnw, knw, wout = w['mixer']

        wdt_c, dtb_c, alog_c = w['compact']
        x1, ssm, conv, kc, vc = _mixer_prompt(xp, sinks, n1, win, wdt_c, cw, cb, dtb_c, alog_c, dskip, snorm, qnw,
                                              knw, wout, tm=_TM_MIXER)
        xp, ffn_tail = _ffn_prompt(x1, *w['ffn'], tm=_TM_FFN)
        p_states.append((ssm.reshape((b,) + head_shape), conv, kc.reshape((b, WINDOW) + kv_shape),
                         vc.reshape((b, WINDOW) + kv_shape), ffn_tail))

        packed, conv_tail = _sample_in(xs, conv_prev_all[i], n1, win, wdt, cw, cb, dtb)
        packed = _to_seq_major(packed, bs).reshape(bs * tlen, -1)
        y, *s_outs = _sample_mix(i, packed, sinks, ssm_all, kc_all, vc_all, alog, dskip, snorm, qnw, knw, s_outs,
                                 tlen=tlen)
        y = _to_time_major(y.reshape(bs, tlen, D_MODEL))
        xs, ffn_tail_s = _ffn_sample(xs, y, wout, ffn_prev_all[i], *w['ffn'])
        s_states.append((conv_tail, ffn_tail_s))

    def stacked(states, j):
        return jnp.stack([st[j] for st in states])

    def seq_major_all(a):
        return a.reshape(depth, a.shape[1] // bs, bs, a.shape[2]).transpose(0, 2, 1, 3)

    ssm_s, k_s, v_s = s_outs
    return (xp, _to_seq_major(xs, bs)) + tuple(stacked(p_states, j) for j in range(5)) + (
        ssm_s.reshape((depth, bs) + head_shape), seq_major_all(stacked(s_states, 0)),
        k_s.reshape((depth, bs, WINDOW) + kv_shape), v_s.reshape((depth, bs, WINDOW) + kv_shape),
        seq_major_all(stacked(s_states, 1)))
```
